```python
import jax, jax.numpy as jnp
from jax import lax
import numpy as np

D_MODEL = 2048
BATCH = 1
SEQ = 8192
DEPTH = 4
DEC_BATCH = 32
DEC_SEQ = 64
PAST_LEN = 2048

CHUNK = 64
N_MIXERS = 2
N_CONV_LAYERS = (DEPTH + 1) // 2
N_RWKV_LAYERS = DEPTH // 2
N_VRES_LAYERS = max(N_RWKV_LAYERS - 1, 0)
CONV_WIDTH = 31
FFN_CONV_WIDTH = 3
D_FF = 5632
HEAD_SIZE = 64
N_HEADS = D_MODEL // HEAD_SIZE
D_DECAY_LORA = 96
D_AAA_LORA = 96
D_MV_LORA = 64
D_GATE_LORA = 256
RMS_EPS = 1e-6
LN_EPS = 1e-5
GN_EPS = 64e-5

kernel_name = 'conformer_rwkv7_convglu_stream_step'


def rmsnorm(x, g):
    xf = x.astype(jnp.float32)
    y = xf * lax.rsqrt(jnp.mean(xf * xf, axis=-1, keepdims=True) + RMS_EPS)
    return (y * g.astype(jnp.float32)).astype(x.dtype)


def layernorm(x, g, b):
    xf = x.astype(jnp.float32)
    mu = jnp.mean(xf, axis=-1, keepdims=True)
    var = jnp.mean(jnp.square(xf - mu), axis=-1, keepdims=True)
    y = (xf - mu) * lax.rsqrt(var + LN_EPS)
    return (y * g.astype(jnp.float32) + b.astype(jnp.float32)).astype(x.dtype)


def causal_dwconv(x, buf, w, b):
    k = w.shape[0]
    xp = jnp.concatenate([buf.astype(x.dtype), x], axis=1)
    y = lax.conv_general_dilated(xp, w[:, None, :].astype(x.dtype), window_strides=(1,), padding='VALID',
                                 dimension_numbers=('NWC', 'WIO', 'NWC'), feature_group_count=x.shape[-1])
    return y + b.astype(x.dtype), xp[:, xp.shape[1] - (k - 1):]


def conformer_conv(h, buf, pw1_w, pw1_b, dw_w, dw_b, ln_g, ln_b, pw2_w, pw2_b):
    a = h @ pw1_w + pw1_b
    u = a[..., :D_MODEL] * jax.nn.sigmoid(a[..., D_MODEL:])
    c, new_buf = causal_dwconv(u, buf, dw_w, dw_b)
    c = layernorm(c, ln_g, ln_b)
    c = c * jax.nn.sigmoid(c)
    return c @ pw2_w + pw2_b, new_buf


def wkv7_scan(r, decay, k, v, kk, a, s0):
    def step(s, inp):
        r_t, w_t, k_t, v_t, kk_t, a_t = inp
        sa = jnp.einsum('bhij,bhj->bhi', s, kk_t)
        s = (s * w_t[:, :, None, :] - sa[..., None] * (kk_t * a_t)[:, :, None, :]
             + v_t[..., None] * k_t[:, :, None, :])
        return s, jnp.einsum('bhij,bhj->bhi', s, r_t)
    xs = tuple(jnp.moveaxis(t, 1, 0) for t in (r, decay, k, v, kk, a))
    s, y = lax.scan(step, s0, xs)
    return jnp.moveaxis(y, 0, 1), s


def rwkv7_time_mix(h, shift, s0, v_first, vres, mix, w_r, w_k, w_v, w_o, w0, w1, w2, a0, a1, a2,
                   g1, g2, k_k, k_a, r_k, ln_g, ln_b):
    b, t, c = h.shape
    prev = jnp.concatenate([shift[:, None, :].astype(h.dtype), h[:, :-1]], axis=1)
    xx = prev - h
    xr = h + xx * mix[0]
    xw = h + xx * mix[1]
    xk = h + xx * mix[2]
    xv = h + xx * mix[3]
    xa = h + xx * mix[4]
    xg = h + xx * mix[5]
    r = xr @ w_r
    k = xk @ w_k
    v = xv @ w_v
    w = -jax.nn.softplus(-(w0 + jnp.tanh(xw @ w1) @ w2)) - 0.5
    a = jax.nn.sigmoid(a0 + (xa @ a1) @ a2)
    g = jax.nn.sigmoid(xg @ g1) @ g2
    if vres is None:
        v_first = v
    else:
        v0, v1, v2 = vres
        v = v + (v_first - v) * jax.nn.sigmoid(v0 + (xv @ v1) @ v2)
    f32 = jnp.float32
    hs = (b, t, N_HEADS, HEAD_SIZE)
    kk = (k * k_k).astype(f32).reshape(hs)
    kk = kk / jnp.maximum(jnp.sqrt(jnp.sum(kk * kk, axis=-1, keepdims=True)), 1e-12)
    k = k * (1 + (a - 1) * k_a)
    rh = r.astype(f32).reshape(hs)
    kh = k.astype(f32).reshape(hs)
    vh = v.astype(f32).reshape(hs)
    ah = a.astype(f32).reshape(hs)
    decay = jnp.exp(-jnp.exp(w.astype(f32))).reshape(hs)
    y, s_new = wkv7_scan(rh, decay, kh, vh, kk, ah, s0.astype(f32))
    mu = jnp.mean(y, axis=-1, keepdims=True)
    var = jnp.mean(jnp.square(y - mu), axis=-1, keepdims=True)
    y = ((y - mu) * lax.rsqrt(var + GN_EPS)).reshape(b, t, c) * ln_g.astype(f32) + ln_b.astype(f32)
    bonus = jnp.sum(rh * kh * r_k.astype(f32), axis=-1, keepdims=True) * vh
    y = (y + bonus.reshape(b, t, c)).astype(h.dtype)
    return (y * g) @ w_o, h[:, -1], s_new.astype(s0.dtype), v_first


def conv_glu_ffn(h, buf, w_in, dw_w, dw_b, w_out):
    a = h @ w_in
    u, gp = a[..., :D_FF], a[..., D_FF:]
    gc, new_buf = causal_dwconv(gp, buf, dw_w, dw_b)
    return (jax.nn.gelu(gc) * u) @ w_out, new_buf


def trunk(x, conv_st, shift_st, wkv_st, ffn_st, p):
    new_conv, new_shift, new_wkv, new_ffn = [], [], [], []
    v_first = None
    for i in range(DEPTH):
        g = p['norm_g'][i]
        h = rmsnorm(x, g[0])
        j = i // N_MIXERS
        if i % N_MIXERS == 0:
            out, cb = conformer_conv(h, conv_st[j], p['conv_pw1_w'][j], p['conv_pw1_b'][j], p['conv_dw_w'][j],
                                     p['conv_dw_b'][j], p['conv_ln_g'][j], p['conv_ln_b'][j],
                                     p['conv_pw2_w'][j], p['conv_pw2_b'][j])
            new_conv.append(cb)
        else:
            vres = None if j == 0 else (p['rwkv_v0'][j - 1], p['rwkv_v1'][j - 1], p['rwkv_v2'][j - 1])
            out, sh, s_new, v_first = rwkv7_time_mix(
                h, shift_st[j], wkv_st[j], v_first, vres, p['rwkv_mix'][j], p['rwkv_w_r'][j], p['rwkv_w_k'][j],
                p['rwkv_w_v'][j], p['rwkv_w_o'][j], p['rwkv_w0'][j], p['rwkv_w1'][j], p['rwkv_w2'][j],
                p['rwkv_a0'][j], p['rwkv_a1'][j], p['rwkv_a2'][j], p['rwkv_g1'][j], p['rwkv_g2'][j],
                p['rwkv_k_k'][j], p['rwkv_k_a'][j], p['rwkv_r_k'][j], p['rwkv_ln_g'][j], p['rwkv_ln_b'][j])
            new_shift.append(sh)
            new_wkv.append(s_new)
        x = x + rmsnorm(out, g[1])
        h = rmsnorm(x, g[2])
        out, fb = conv_glu_ffn(h, ffn_st[i], p['ffn_w_in'][i], p['ffn_dw_w'][i], p['ffn_dw_b'][i], p['ffn_w_out'][i])
        new_ffn.append(fb)
        x = x + rmsnorm(out, g[3])
    return x, jnp.stack(new_conv), jnp.stack(new_shift), jnp.stack(new_wkv), jnp.stack(new_ffn)


def setup_inputs(seed: int = 0) -> dict:
    key = jax.random.key(seed)
    ks = iter(jax.random.split(key, 64))
    f32 = jnp.float32

    def nrm(shape, scale):
        return jax.random.normal(next(ks), shape, f32) * scale

    def unif(shape, lo, hi):
        return jax.random.uniform(next(ks), shape, f32, lo, hi)

    D, NC, NR, NV = D_MODEL, N_CONV_LAYERS, N_RWKV_LAYERS, N_VRES_LAYERS
    sd = D ** -0.5
    return {
        'x_prompt': nrm((BATCH, SEQ, D), 1.0),
        'x_sample': nrm((DEC_BATCH, DEC_SEQ, D), 1.0),
        'state_conv_mix': nrm((NC, DEC_BATCH, CONV_WIDTH - 1, D), 0.5),
        'state_rwkv_shift': nrm((NR, DEC_BATCH, D), 1.0),
        'state_rwkv_wkv': nrm((NR, DEC_BATCH, N_HEADS, HEAD_SIZE, HEAD_SIZE), 0.5),
        'state_ffn_conv': nrm((DEPTH, DEC_BATCH, FFN_CONV_WIDTH - 1, D_FF), 1.0),
        'norm_g': 1.0 + nrm((DEPTH, 4, D), 0.02),
        'conv_pw1_w': nrm((NC, D, 2 * D), sd),
        'conv_pw1_b': nrm((NC, 2 * D), 0.02),
        'conv_dw_w': nrm((NC, CONV_WIDTH, D), CONV_WIDTH ** -0.5),
        'conv_dw_b': nrm((NC, D), 0.02),
        'conv_ln_g': 1.0 + nrm((NC, D), 0.02),
        'conv_ln_b': nrm((NC, D), 0.02),
        'conv_pw2_w': nrm((NC, D, D), sd),
        'conv_pw2_b': nrm((NC, D), 0.02),
        'rwkv_mix': unif((NR, 6, D), 0.0, 1.0),
        'rwkv_w_r': nrm((NR, D, D), sd),
        'rwkv_w_k': nrm((NR, D, D), sd),
        'rwkv_w_v': nrm((NR, D, D), sd),
        'rwkv_w_o': nrm((NR, D, D), sd),
        'rwkv_w0': unif((NR, D), -6.0, -1.0),
        'rwkv_w1': nrm((NR, D, D_DECAY_LORA), sd),
        'rwkv_w2': nrm((NR, D_DECAY_LORA, D), 0.5 * D_DECAY_LORA ** -0.5),
        'rwkv_a0': nrm((NR, D), 0.1),
        'rwkv_a1': nrm((NR, D, D_AAA_LORA), sd),
        'rwkv_a2': nrm((NR, D_AAA_LORA, D), 0.5 * D_AAA_LORA ** -0.5),
        'rwkv_v0': 1.0 + nrm((NV, D), 0.1),
        'rwkv_v1': nrm((NV, D, D_MV_LORA), sd),
        'rwkv_v2': nrm((NV, D_MV_LORA, D), 0.5 * D_MV_LORA ** -0.5),
        'rwkv_g1': nrm((NR, D, D_GATE_LORA), sd),
        'rwkv_g2': nrm((NR, D_GATE_LORA, D), D_GATE_LORA ** -0.5),
        'rwkv_k_k': 0.85 + nrm((NR, D), 0.05),
        'rwkv_k_a': 1.0 + nrm((NR, D), 0.05),
        'rwkv_r_k': nrm((NR, N_HEADS, HEAD_SIZE), 0.1),
        'rwkv_ln_g': 1.0 + nrm((NR, D), 0.02),
        'rwkv_ln_b': nrm((NR, D), 0.02),
        'ffn_w_in': nrm((DEPTH, D, 2 * D_FF), sd),
        'ffn_dw_w': nrm((DEPTH, FFN_CONV_WIDTH, D_FF), FFN_CONV_WIDTH ** -0.5),
        'ffn_dw_b': nrm((DEPTH, D_FF), 0.02),
        'ffn_w_out': nrm((DEPTH, D_FF, D), D_FF ** -0.5),
    }


def reference(x_prompt, x_sample, state_conv_mix, state_rwkv_shift, state_rwkv_wkv, state_ffn_conv, norm_g,
              conv_pw1_w, conv_pw1_b, conv_dw_w, conv_dw_b, conv_ln_g, conv_ln_b, conv_pw2_w, conv_pw2_b,
              rwkv_mix, rwkv_w_r, rwkv_w_k, rwkv_w_v, rwkv_w_o, rwkv_w0, rwkv_w1, rwkv_w2, rwkv_a0, rwkv_a1,
              rwkv_a2, rwkv_v0, rwkv_v1, rwkv_v2, rwkv_g1, rwkv_g2, rwkv_k_k, rwkv_k_a, rwkv_r_k, rwkv_ln_g,
              rwkv_ln_b, ffn_w_in, ffn_dw_w, ffn_dw_b, ffn_w_out):
    p = dict(norm_g=norm_g, conv_pw1_w=conv_pw1_w, conv_pw1_b=conv_pw1_b, conv_dw_w=conv_dw_w,
             conv_dw_b=conv_dw_b, conv_ln_g=conv_ln_g, conv_ln_b=conv_ln_b, conv_pw2_w=conv_pw2_w,
             conv_pw2_b=conv_pw2_b, rwkv_mix=rwkv_mix, rwkv_w_r=rwkv_w_r, rwkv_w_k=rwkv_w_k, rwkv_w_v=rwkv_w_v,
             rwkv_w_o=rwkv_w_o, rwkv_w0=rwkv_w0, rwkv_w1=rwkv_w1, rwkv_w2=rwkv_w2, rwkv_a0=rwkv_a0,
             rwkv_a1=rwkv_a1, rwkv_a2=rwkv_a2, rwkv_v0=rwkv_v0, rwkv_v1=rwkv_v1, rwkv_v2=rwkv_v2,
             rwkv_g1=rwkv_g1, rwkv_g2=rwkv_g2, rwkv_k_k=rwkv_k_k, rwkv_k_a=rwkv_k_a, rwkv_r_k=rwkv_r_k,
             rwkv_ln_g=rwkv_ln_g, rwkv_ln_b=rwkv_ln_b, ffn_w_in=ffn_w_in, ffn_dw_w=ffn_dw_w,
             ffn_dw_b=ffn_dw_b, ffn_w_out=ffn_w_out)
    b, dt = x_prompt.shape[0], x_prompt.dtype
    zc = jnp.zeros((N_CONV_LAYERS, b, CONV_WIDTH - 1, D_MODEL), dt)
    zs = jnp.zeros((N_RWKV_LAYERS, b, D_MODEL), dt)
    zw = jnp.zeros((N_RWKV_LAYERS, b, N_HEADS, HEAD_SIZE, HEAD_SIZE), dt)
    zf = jnp.zeros((DEPTH, b, FFN_CONV_WIDTH - 1, D_FF), dt)
    y_prompt, conv_p, shift_p, wkv_p, ffn_p = trunk(x_prompt, zc, zs, zw, zf, p)
    y_sample, conv_s, shift_s, wkv_s, ffn_s = trunk(x_sample, state_conv_mix, state_rwkv_shift,
                                                    state_rwkv_wkv, state_ffn_conv, p)
    return (y_prompt, y_sample, conv_p, conv_s, shift_p, shift_s, wkv_p, wkv_s, ffn_p, ffn_s)
```

```python
import functools

import jax
import jax.numpy as jnp
from jax import lax
from jax.experimental import pallas as pl
from jax.experimental.pallas import tpu as pltpu

F32 = jnp.float32
BF16 = jnp.bfloat16

RMS_EPS = 1e-6
LN_EPS = 1e-5
GN_EPS = 64e-5
HEAD_SIZE = 64
CONV_WIDTH = 31
FFN_CONV_WIDTH = 3

LANES = 128
SUBLANES = 8
VMEM_LIMIT_BYTES = 56 * 1024 * 1024

TOKEN_TILE = 512
WKV_CHUNK = 64


def _params(*semantics):
    return pltpu.CompilerParams(dimension_semantics=semantics, vmem_limit_bytes=VMEM_LIMIT_BYTES)


def _rmsnorm(x, g):
    return x * lax.rsqrt(jnp.mean(x * x, axis=-1, keepdims=True) + RMS_EPS) * g


def _row(v):
    return v.reshape(1, -1)


def _ffn_kernel(*refs, S, L, carry):
    if carry:
        (x_ref, g2_ref, g3_ref, wu_ref, wg_ref, dww_ref, dwb_ref, wo_ref,
         o_ref, nb_ref, h_scr, gpx_scr, carry_scr) = refs
    else:
        (x_ref, g2_ref, g3_ref, wu_ref, wg_ref, dww_ref, dwb_ref, wo_ref, halo_ref,
         o_ref, nb_ref, h_scr, gpx_scr) = refs
    i = pl.program_id(0)
    j = pl.program_id(1)
    nj = pl.num_programs(1)
    K = FFN_CONV_WIDTH - 1

    @pl.when(j == 0)
    def _():
        h_scr[...] = _rmsnorm(x_ref[...], g2_ref[...]).astype(BF16)

    h = h_scr[...]
    u = jnp.dot(h, wu_ref[...], preferred_element_type=F32)
    gp = jnp.dot(h, wg_ref[...], preferred_element_type=F32)
    tf = gp.shape[-1]
    gp3 = gp.reshape(S, L, tf)

    gpx_scr[:, SUBLANES:SUBLANES + L, :] = gp3
    if carry:
        @pl.when(i == 0)
        def _():
            carry_scr[j] = jnp.zeros((SUBLANES, tf), F32)
        gpx_scr[0, 0:SUBLANES, :] = carry_scr[j]
    else:
        gpx_scr[:, SUBLANES - K:SUBLANES, :] = halo_ref[...]
    w = dww_ref[...]
    gc = (gpx_scr[:, SUBLANES - 2:SUBLANES - 2 + L, :] * w[0:1, :]
          + gpx_scr[:, SUBLANES - 1:SUBLANES - 1 + L, :] * w[1:2, :]
          + gp3 * w[2:3, :] + dwb_ref[...])
    act = (jax.nn.gelu(gc) * u.reshape(S, L, tf)).reshape(S * L, tf).astype(BF16)
    contrib = jnp.dot(act, wo_ref[...], preferred_element_type=F32)

    @pl.when(j == 0)
    def _():
        o_ref[...] = contrib

    @pl.when(j > 0)
    def _():
        o_ref[...] += contrib

    nb_ref[...] = gpx_scr[:, SUBLANES + L - K:SUBLANES + L, :]
    if carry:
        carry_scr[j] = gpx_scr[0, L:L + SUBLANES, :]

    @pl.when(j == nj - 1)
    def _():
        o_ref[...] = x_ref[...] + _rmsnorm(o_ref[...], g3_ref[...])


def _ffn_chunk(d_ff):
    for tf in (512, 256, 128):
        if d_ff % tf == 0:
            return tf
    raise ValueError(f"d_ff={d_ff} is not a multiple of {LANES}")


def _ffn_call(x, g2, g3, w_in, dw_w, dw_b, w_out, halo, *, seq_len):
    T, D = x.shape
    d_ff = w_out.shape[0]
    tf = _ffn_chunk(d_ff)
    nj = d_ff // tf
    carry = halo is None
    TM = min(TOKEN_TILE, T)
    L = TM if carry else seq_len
    S = TM // L
    assert T % TM == 0 and TM % L == 0 and (carry or L == seq_len)
    nt = T // TM
    in_specs = [
        pl.BlockSpec((TM, D), lambda i, j: (i, 0)),
        pl.BlockSpec((1, D), lambda i, j: (0, 0)),
        pl.BlockSpec((1, D), lambda i, j: (0, 0)),
        pl.BlockSpec((D, tf), lambda i, j: (0, j)),
        pl.BlockSpec((D, tf), lambda i, j: (0, nj + j)),
        pl.BlockSpec((FFN_CONV_WIDTH, tf), lambda i, j: (0, j)),
        pl.BlockSpec((1, tf), lambda i, j: (0, j)),
        pl.BlockSpec((tf, D), lambda i, j: (j, 0)),
    ]
    args = [x, _row(g2), _row(g3), w_in, w_in, dw_w, _row(dw_b), w_out]
    scratch = [pltpu.VMEM((TM, D), BF16), pltpu.VMEM((S, SUBLANES + L, tf), F32)]
    if carry:
        nb_spec = pl.BlockSpec((1, FFN_CONV_WIDTH - 1, tf), lambda i, j: (i, 0, j))
        scratch.append(pltpu.VMEM((nj, SUBLANES, tf), F32))
    else:
        in_specs.append(pl.BlockSpec((S, FFN_CONV_WIDTH - 1, tf), lambda i, j: (i, 0, j)))
        args.append(halo)
        nb_spec = pl.BlockSpec((S, FFN_CONV_WIDTH - 1, tf), lambda i, j: (i, 0, j))
    out, nb = pl.pallas_call(
        functools.partial(_ffn_kernel, S=S, L=L, carry=carry),
        grid=(nt, nj),
        in_specs=in_specs,
        out_specs=[pl.BlockSpec((TM, D), lambda i, j: (i, 0)), nb_spec],
        out_shape=[jax.ShapeDtypeStruct((T, D), F32),
                   jax.ShapeDtypeStruct((nt * S, FFN_CONV_WIDTH - 1, d_ff), F32)],
        scratch_shapes=scratch,
        compiler_params=_params("arbitrary", "arbitrary"),
        name="ffn_carry" if carry else "ffn_state",
    )(*args)
    return out, (nb[-1:] if carry else nb)


def _tiling(T, seq_len, carry, tile=TOKEN_TILE):
    TM = min(tile, T)
    L = TM if carry else seq_len
    assert T % TM == 0 and TM % L == 0 and L % SUBLANES == 0
    return TM, L, TM // L


def _glu_kernel(x_ref, g_ref, wa_ref, wb_ref, ba_ref, bb_ref, u_ref, h_scr):
    @pl.when(pl.program_id(1) == 0)
    def _():
        h_scr[...] = _rmsnorm(x_ref[...], g_ref[...]).astype(BF16)

    h = h_scr[...]
    a = jnp.dot(h, wa_ref[...], preferred_element_type=F32) + ba_ref[...]
    b = jnp.dot(h, wb_ref[...], preferred_element_type=F32) + bb_ref[...]
    u_ref[...] = a * jax.nn.sigmoid(b)


def _glu_call(x, g, w1, b1):
    T, D = x.shape
    TM = min(TOKEN_TILE, T)
    tn = min(512, D)
    nj = D // tn
    assert T % TM == 0 and D % tn == 0
    return pl.pallas_call(
        _glu_kernel,
        grid=(T // TM, nj),
        in_specs=[
            pl.BlockSpec((TM, D), lambda i, j: (i, 0)),
            pl.BlockSpec((1, D), lambda i, j: (0, 0)),
            pl.BlockSpec((D, tn), lambda i, j: (0, j)),
            pl.BlockSpec((D, tn), lambda i, j: (0, nj + j)),
            pl.BlockSpec((1, tn), lambda i, j: (0, j)),
            pl.BlockSpec((1, tn), lambda i, j: (0, nj + j)),
        ],
        out_specs=pl.BlockSpec((TM, tn), lambda i, j: (i, j)),
        out_shape=jax.ShapeDtypeStruct((T, D), F32),
        scratch_shapes=[pltpu.VMEM((TM, D), BF16)],
        compiler_params=_params("arbitrary", "arbitrary"),
        name="conv_glu",
    )(x, _row(g), w1, w1, _row(b1), _row(b1))


HIST_PAD = 32
DW_ROWS = 64


def _dwconv_kernel(*refs, S, L, carry):
    if carry:
        u_ref, w_ref, b_ref, c_ref, ux_scr, carry_scr = refs
    else:
        u_ref, w_ref, b_ref, st_ref, c_ref, ux_scr = refs
    i = pl.program_id(0)
    j = pl.program_id(1)
    tc = u_ref.shape[-1]
    K = CONV_WIDTH - 1
    ux_scr[:, HIST_PAD:HIST_PAD + L, :] = u_ref[...].reshape(S, L, tc)
    if carry:
        @pl.when(i == 0)
        def _():
            carry_scr[j] = jnp.zeros((HIST_PAD, tc), F32)
        ux_scr[0, 0:HIST_PAD, :] = carry_scr[j]
    else:
        ux_scr[:, HIST_PAD - K:HIST_PAD, :] = st_ref[...]
    w = w_ref[...]
    rb = min(DW_ROWS, L)
    for s in range(S):
        for r0 in range(0, L, rb):
            acc = jnp.broadcast_to(b_ref[...], (rb, tc))
            for k in range(CONV_WIDTH):
                lo = HIST_PAD - K + k + r0
                acc = acc + ux_scr[s, lo:lo + rb, :] * w[k:k + 1, :]
            c_ref[s * L + r0:s * L + r0 + rb, :] = acc
    if carry:
        carry_scr[j] = ux_scr[0, L:L + HIST_PAD, :]


def _dwconv_call(u, w, b, state, *, seq_len):
    T, D = u.shape
    carry = state is None
    TM, L, S = _tiling(T, seq_len, carry)
    tc = min(256, D)
    nc = D // tc
    in_specs = [
        pl.BlockSpec((TM, tc), lambda i, j: (i, j)),
        pl.BlockSpec((CONV_WIDTH, tc), lambda i, j: (0, j)),
        pl.BlockSpec((1, tc), lambda i, j: (0, j)),
    ]
    args = [u, w, _row(b)]
    scratch = [pltpu.VMEM((S, HIST_PAD + L, tc), F32)]
    if carry:
        scratch.append(pltpu.VMEM((nc, HIST_PAD, tc), F32))
    else:
        in_specs.append(pl.BlockSpec((S, CONV_WIDTH - 1, tc), lambda i, j: (i, 0, j)))
        args.append(state)
    return pl.pallas_call(
        functools.partial(_dwconv_kernel, S=S, L=L, carry=carry),
        grid=(T // TM, nc),
        in_specs=in_specs,
        out_specs=pl.BlockSpec((TM, tc), lambda i, j: (i, j)),
        out_shape=jax.ShapeDtypeStruct((T, D), F32),
        scratch_shapes=scratch,
        compiler_params=_params("arbitrary", "arbitrary"),
        name="conv_dw_carry" if carry else "conv_dw_state",
    )(*args)


def _proj_kernel(*refs, ln, bias):
    refs = list(refs)
    z_ref, x_ref, w_ref, g_ref = refs[:4]
    rest = refs[4:]
    z = z_ref[...].astype(F32)
    if ln:
        lg_ref, lb_ref = rest[:2]
        rest = rest[2:]
        mu = jnp.mean(z, axis=-1, keepdims=True)
        zc = z - mu
        var = jnp.mean(zc * zc, axis=-1, keepdims=True)
        z = zc * lax.rsqrt(var + LN_EPS) * lg_ref[...] + lb_ref[...]
        z = z * jax.nn.sigmoid(z)
    out = jnp.dot(z.astype(BF16), w_ref[...], preferred_element_type=F32)
    if bias:
        out = out + rest[0][...]
        rest = rest[1:]
    o_ref, = rest
    o_ref[...] = x_ref[...] + _rmsnorm(out, g_ref[...])


PROJ_TILE = 256


def _head_ones():
    r = lax.broadcasted_iota(jnp.int32, (LANES, LANES), 0) // HEAD_SIZE
    c = lax.broadcasted_iota(jnp.int32, (LANES, LANES), 1) // HEAD_SIZE
    return (r == c).astype(BF16)


def _head_sum(x, ones):
    outs = []
    for p in range(x.shape[1] // LANES):
        xp = x[:, p * LANES:(p + 1) * LANES]
        hi = xp.astype(BF16)
        lo = (xp - hi.astype(F32)).astype(BF16)
        outs.append(jnp.dot(hi, ones, preferred_element_type=F32) + jnp.dot(lo, ones, preferred_element_type=F32))
    return outs[0] if len(outs) == 1 else jnp.concatenate(outs, axis=1)


LORA_PAD = 128
RWKV_IN_TILE = 256


def _softplus(z):
    return jnp.maximum(z, 0.0) + jnp.log(1.0 + jnp.exp(-jnp.abs(z)))


def _rwkv_in_kernel(*refs, S, L, carry, vres):
    refs = list(refs)
    x_ref, g_ref, mix_ref, wr_ref, wk_ref, wv_ref, w1_ref, a1_ref, g1_ref = refs[:9]
    w2_ref, a2_ref, g2_ref, w0_ref, a0_ref, kk_ref, ka_ref = refs[9:16]
    rest = refs[16:]
    if vres:
        v1_ref, v2_ref, v0_ref, vf_ref = rest[:4]
        rest = rest[4:]
    if not carry:
        sh_ref = rest[0]
        rest = rest[1:]
    r_ref, k_ref, v_ref, kkn_ref, a_ref, lw_ref, go_ref, hl_ref = rest[:8]
    rest = rest[8:]
    mx_scr, tw_scr, ta_scr, tg_scr, hx_scr = rest[:5]
    rest = rest[5:]
    if vres:
        tv_scr = rest[0]
        rest = rest[1:]
    if carry:
        carry_scr, = rest
    i = pl.program_id(0)
    j = pl.program_id(1)
    TM, D = x_ref.shape

    @pl.when(j == 0)
    def _():
        h = _rmsnorm(x_ref[...], g_ref[...])
        hx_scr[:, SUBLANES:SUBLANES + L, :] = h.reshape(S, L, D)
        if carry:
            @pl.when(i == 0)
            def _():
                carry_scr[...] = jnp.zeros((SUBLANES, D), F32)
            hx_scr[0, 0:SUBLANES, :] = carry_scr[...]
            carry_scr[...] = hx_scr[0, L:L + SUBLANES, :]
        else:
            hx_scr[:, SUBLANES - 1:SUBLANES, :] = sh_ref[...]
        hl_ref[...] = hx_scr[:, SUBLANES + L - 1:SUBLANES + L, :]
        xx = hx_scr[:, SUBLANES - 1:SUBLANES - 1 + L, :].reshape(TM, D) - h
        for m in range(6):
            mx_scr[m] = (h + xx * mix_ref[m:m + 1, :]).astype(BF16)
        tw_scr[...] = jnp.tanh(jnp.dot(mx_scr[1], w1_ref[...], preferred_element_type=F32)).astype(BF16)
        ta_scr[...] = jnp.dot(mx_scr[4], a1_ref[...], preferred_element_type=F32).astype(BF16)
        tg_scr[...] = jax.nn.sigmoid(jnp.dot(mx_scr[5], g1_ref[...], preferred_element_type=F32)).astype(BF16)
        if vres:
            tv_scr[...] = jnp.dot(mx_scr[3], v1_ref[...], preferred_element_type=F32).astype(BF16)

    r = jnp.dot(mx_scr[0], wr_ref[...], preferred_element_type=F32)
    k = jnp.dot(mx_scr[2], wk_ref[...], preferred_element_type=F32)
    v = jnp.dot(mx_scr[3], wv_ref[...], preferred_element_type=F32)
    wl = jnp.dot(tw_scr[...], w2_ref[...], preferred_element_type=F32)
    al = jnp.dot(ta_scr[...], a2_ref[...], preferred_element_type=F32)
    go_ref[...] = jnp.dot(tg_scr[...], g2_ref[...], preferred_element_type=F32)
    w = -_softplus(-(w0_ref[...] + wl)) - 0.5
    lw_ref[...] = -jnp.exp(w)
    a = jax.nn.sigmoid(a0_ref[...] + al)
    if vres:
        vl = jnp.dot(tv_scr[...], v2_ref[...], preferred_element_type=F32)
        v = v + (vf_ref[...] - v) * jax.nn.sigmoid(v0_ref[...] + vl)
    kk = k * kk_ref[...]
    norm = jnp.sqrt(_head_sum(kk * kk, _head_ones()))
    kkn_ref[...] = kk / jnp.maximum(norm, 1e-12)
    r_ref[...] = r
    k_ref[...] = k * (1.0 + (a - 1.0) * ka_ref[...])
    v_ref[...] = v
    a_ref[...] = a


WKV_PAIRS = 4


def _mm(a, b):
    return jnp.dot(a.astype(BF16), b.astype(BF16), preferred_element_type=F32)


def _mm_nt(a, b):
    return lax.dot_general(a.astype(BF16), b.astype(BF16), (((1,), (1,)), ((), ())), preferred_element_type=F32)


def _mm_tn(a, b):
    return lax.dot_general(a.astype(BF16), b.astype(BF16), (((0,), (0,)), ((), ())), preferred_element_type=F32)


def _wkv_kernel(r_ref, k_ref, v_ref, kk_ref, a_ref, lw_ref, g_ref, rk_ref, lng_ref, lnb_ref, s_ref,
                z_ref, so_ref, st_scr, *, PP):
    c = pl.program_id(2)
    nc = pl.num_programs(2)
    C = r_ref.shape[0]
    assert 2 * C == LANES and C == HEAD_SIZE

    @pl.when(c == 0)
    def _():
        st_scr[...] = s_ref[0]

    row = lax.broadcasted_iota(jnp.int32, (C, LANES), 0)
    lane = lax.broadcasted_iota(jnp.int32, (C, LANES), 1)
    sidx = lane % C
    lo = lane < C
    strict = sidx < row
    incl = sidx <= row
    eye = (sidx == row).astype(F32)
    ones = _head_ones()

    def bd(x):
        z = jnp.zeros_like(x)
        return jnp.concatenate([jnp.where(lo, x, z), jnp.where(lo, z, x)], axis=0)

    lw = lw_ref[...]
    rows_full = lax.broadcasted_iota(jnp.int32, lw.shape, 0)
    lc = lw
    sh = 1
    while sh < C:
        lc = lc + jnp.where(rows_full >= sh, pltpu.roll(lc, sh, axis=0), 0.0)
        sh *= 2

    for pp in range(PP):
        sl = slice(pp * LANES, (pp + 1) * LANES)
        lwp, lp = lw[:, sl], lc[:, sl]
        ltot = lp[C - 1:C, :]
        rp, kp, vp = r_ref[:, sl], k_ref[:, sl], v_ref[:, sl]
        kkp = kk_ref[:, sl]
        bp = kkp * a_ref[:, sl]
        e_neg = jnp.exp(-lp)
        e_rem = jnp.exp(ltot - lp)
        at = -(kkp * jnp.exp(lp - lwp))
        rt = rp * jnp.exp(lp)
        lhs = jnp.concatenate([at, rt], axis=0)
        gm = _mm_nt(lhs, jnp.concatenate([bd(bp * e_neg), bd(kp * e_neg)], axis=0))
        n_ab = jnp.where(strict, gm[:C, :LANES], 0.0)
        a_ak = jnp.where(strict, gm[:C, LANES:], 0.0)
        a_rb = jnp.where(incl, gm[C:, :LANES], 0.0)
        a_rk = jnp.where(incl, gm[C:, LANES:], 0.0)
        pw = n_ab
        tm = eye + n_ab
        pw_bd = bd(pw)
        n = 2
        while n < C:
            pw = _mm(pw, pw_bd)
            pw_bd = bd(pw)
            tm = tm + _mm(tm, pw_bd)
            n *= 2
        sp = st_scr[pp]
        s_prod = _mm_nt(lhs, bd(sp))
        av = _mm(jnp.concatenate([a_ak, a_rk], axis=0), bd(vp))
        u = _mm(tm, bd(s_prod[:C] + av[:C]))
        y = s_prod[C:] + av[C:] + _mm(a_rb, bd(u))
        zz = _mm_tn(jnp.concatenate([u, vp], axis=0), jnp.concatenate([bp * e_rem, kp * e_rem], axis=0))
        st_scr[pp] = sp * jnp.exp(ltot) + jnp.where(lo, zz[:C], zz[C:])
        mu = _head_sum(y, ones) * (1.0 / HEAD_SIZE)
        yc = y - mu
        var = _head_sum(yc * yc, ones) * (1.0 / HEAD_SIZE)
        yn = yc * lax.rsqrt(var + GN_EPS) * lng_ref[:, sl] + lnb_ref[:, sl]
        bonus = _head_sum(rp * kp * rk_ref[:, sl], ones) * vp
        z_ref[:, sl] = (yn + bonus) * g_ref[:, sl]

    @pl.when(c == nc - 1)
    def _():
        so_ref[0] = st_scr[...]


def _pack_state(s):
    B, H, N, _ = s.shape
    return s.reshape(B, H // 2, 2, N, N).transpose(0, 1, 3, 2, 4).reshape(B, H // 2, N, 2 * N)


def _unpack_state(s):
    B, HP, N, _ = s.shape
    return s.reshape(B, HP, N, 2, N).transpose(0, 1, 3, 2, 4).reshape(B, 2 * HP, N, N)


def _wkv_call(rkvkal, gate, r_k, ln_g, ln_b, state, *, seq_len):
    T, D = gate.shape
    B = T // seq_len
    C = WKV_CHUNK
    NC = seq_len // C
    PP = min(WKV_PAIRS, D // LANES)
    Q = D // (PP * LANES)
    assert seq_len % C == 0 and D % (PP * LANES) == 0
    tile = pl.BlockSpec((C, PP * LANES), lambda b, q, c: (b * NC + c, q))
    row = pl.BlockSpec((1, PP * LANES), lambda b, q, c: (0, q))
    st = pl.BlockSpec((1, PP, HEAD_SIZE, LANES), lambda b, q, c: (b, q, 0, 0))
    z, s_new = pl.pallas_call(
        functools.partial(_wkv_kernel, PP=PP),
        grid=(B, Q, NC),
        in_specs=[tile] * 7 + [row] * 3 + [st],
        out_specs=[tile, st],
        out_shape=[jax.ShapeDtypeStruct((T, D), F32),
                   jax.ShapeDtypeStruct((B, D // LANES, HEAD_SIZE, LANES), F32)],
        scratch_shapes=[pltpu.VMEM((PP, HEAD_SIZE, LANES), F32)],
        compiler_params=_params("arbitrary", "arbitrary", "arbitrary"),
        name="rwkv_wkv",
    )(*rkvkal, gate, _row(r_k.reshape(-1)), _row(ln_g), _row(ln_b), _pack_state(state))
    return z, _unpack_state(s_new)


def _pad_cols(w):
    pad = -w.shape[1] % LORA_PAD
    return jnp.pad(w, ((0, 0), (0, pad))) if pad else w


def _pad_rows(w):
    pad = -w.shape[0] % LORA_PAD
    return jnp.pad(w, ((0, pad), (0, 0))) if pad else w


def _rwkv_in_call(x, g, p, shift, v_first, *, seq_len):
    T, D = x.shape
    carry = shift is None
    vres = v_first is not None
    TM, L, S = _tiling(T, seq_len, carry, RWKV_IN_TILE)
    tn = min(512, D)
    nj = D // tn
    row_d = pl.BlockSpec((1, D), lambda i, j: (0, 0))
    row_n = pl.BlockSpec((1, tn), lambda i, j: (0, j))
    col_w = pl.BlockSpec((D, tn), lambda i, j: (0, j))
    tile_n = pl.BlockSpec((TM, tn), lambda i, j: (i, j))

    def whole(a):
        return pl.BlockSpec(a.shape, lambda i, j: (0, 0))

    def lora2(a):
        return pl.BlockSpec((a.shape[0], tn), lambda i, j: (0, j))

    in_specs = [pl.BlockSpec((TM, D), lambda i, j: (i, 0)), row_d, whole(p['mix']), col_w, col_w, col_w,
                whole(p['w1']), whole(p['a1']), whole(p['g1']), lora2(p['w2']), lora2(p['a2']), lora2(p['g2']),
                row_n, row_n, row_n, row_n]
    args = [x, _row(g), p['mix'], p['w_r'], p['w_k'], p['w_v'], p['w1'], p['a1'], p['g1'], p['w2'], p['a2'], p['g2'],
            _row(p['w0']), _row(p['a0']), _row(p['k_k']), _row(p['k_a'])]
    if vres:
        in_specs += [whole(p['v1']), lora2(p['v2']), row_n, tile_n]
        args += [p['v1'], p['v2'], _row(p['v0']), v_first]
    if not carry:
        in_specs.append(pl.BlockSpec((S, 1, D), lambda i, j: (i, 0, 0)))
        args.append(shift.reshape(-1, 1, D))
    nseg = (T // TM) * S
    out_specs = [tile_n] * 7 + [pl.BlockSpec((S, 1, D), lambda i, j: (i, 0, 0))]
    out_shape = [jax.ShapeDtypeStruct((T, D), F32)] * 7 + [jax.ShapeDtypeStruct((nseg, 1, D), F32)]
    scratch = [pltpu.VMEM((6, TM, D), BF16), pltpu.VMEM((TM, p['w1'].shape[1]), BF16),
               pltpu.VMEM((TM, p['a1'].shape[1]), BF16), pltpu.VMEM((TM, p['g1'].shape[1]), BF16),
               pltpu.VMEM((S, SUBLANES + L, D), F32)]
    if vres:
        scratch.append(pltpu.VMEM((TM, p['v1'].shape[1]), BF16))
    if carry:
        scratch.append(pltpu.VMEM((SUBLANES, D), F32))
    outs = pl.pallas_call(
        functools.partial(_rwkv_in_kernel, S=S, L=L, carry=carry, vres=vres),
        grid=(T // TM, nj),
        in_specs=in_specs,
        out_specs=out_specs,
        out_shape=out_shape,
        scratch_shapes=scratch,
        compiler_params=_params("arbitrary", "arbitrary"),
        name="rwkv_in_carry" if carry else "rwkv_in_state",
    )(*args)
    r, k, v, kk, a, lw, gate, hl = outs
    h_last = hl[-1:, 0, :] if carry else hl[:, 0, :]
    return (r, k, v, kk, a, lw), gate, h_last


def _proj_call(z, x, w, g, *, ln_g=None, ln_b=None, b=None):
    T, D = x.shape
    TM = min(PROJ_TILE, T)
    assert T % TM == 0
    row = pl.BlockSpec((1, D), lambda i: (0, 0))
    tile = pl.BlockSpec((TM, D), lambda i: (i, 0))
    in_specs = [tile, tile, pl.BlockSpec((D, D), lambda i: (0, 0)), row]
    args = [z, x, w, _row(g)]
    if ln_g is not None:
        in_specs += [row, row]
        args += [_row(ln_g), _row(ln_b)]
    if b is not None:
        in_specs.append(row)
        args.append(_row(b))
    return pl.pallas_call(
        functools.partial(_proj_kernel, ln=ln_g is not None, bias=b is not None),
        grid=(T // TM,),
        in_specs=in_specs,
        out_specs=tile,
        out_shape=jax.ShapeDtypeStruct((T, D), F32),
        compiler_params=_params("arbitrary"),
        name="proj_ln" if ln_g is not None else "proj",
    )(*args)


def _trunk(x3, conv_st, shift_st, wkv_st, ffn_st, p):
    B, T, D = x3.shape
    x = x3.reshape(B * T, D)
    depth = p['norm_g'].shape[0]
    new_conv, new_shift, new_wkv, new_ffn = [], [], [], []
    v_first = None
    for i in range(depth):
        g = p['norm_g'][i]
        j = i // 2
        if i % 2 == 0:
            u = _glu_call(x, g[0], p['conv_pw1_w'][j], p['conv_pw1_b'][j])
            c = _dwconv_call(u, p['conv_dw_w'][j], p['conv_dw_b'][j],
                             None if conv_st is None else conv_st[j], seq_len=T)
            new_conv.append(u.reshape(B, T, D)[:, T - (CONV_WIDTH - 1):])
            x = _proj_call(c, x, p['conv_pw2_w'][j], g[1], ln_g=p['conv_ln_g'][j], ln_b=p['conv_ln_b'][j],
                           b=p['conv_pw2_b'][j])
        else:
            q = {k[5:]: v[j] for k, v in p.items() if k.startswith('rwkv_') and k[5:] not in ('v0', 'v1', 'v2')}
            if j > 0:
                q.update({k: p['rwkv_' + k][j - 1] for k in ('v0', 'v1', 'v2')})
            rkvkal, gate, h_last = _rwkv_in_call(x, g[0], q, None if shift_st is None else shift_st[j],
                                                 v_first if j > 0 else None, seq_len=T)
            if j == 0:
                v_first = rkvkal[2]
            s0 = jnp.zeros((B, D // HEAD_SIZE, HEAD_SIZE, HEAD_SIZE), F32) if wkv_st is None else wkv_st[j]
            z, s_new = _wkv_call(rkvkal, gate, q['r_k'], q['ln_g'], q['ln_b'], s0, seq_len=T)
            new_shift.append(h_last)
            new_wkv.append(s_new)
            x = _proj_call(z, x, q['w_o'], g[1])
        x, nb = _ffn_call(x, g[2], g[3], p['ffn_w_in'][i], p['ffn_dw_w'][i], p['ffn_dw_b'][i], p['ffn_w_out'][i],
                          None if ffn_st is None else ffn_st[i], seq_len=T)
        new_ffn.append(nb)
    return (x.reshape(B, T, D), jnp.stack(new_conv), jnp.stack(new_shift), jnp.stack(new_wkv), jnp.stack(new_ffn))


def kernel(x_prompt, x_sample, state_conv_mix, state_rwkv_shift, state_rwkv_wkv, state_ffn_conv, norm_g,
           conv_pw1_w, conv_pw1_b, conv_dw_w, conv_dw_b, conv_ln_g, conv_ln_b, conv_pw2_w, conv_pw2_b,
           rwkv_mix, rwkv_w_r, rwkv_w_k, rwkv_w_v, rwkv_w_o, rwkv_w0, rwkv_w1, rwkv_w2, rwkv_a0, rwkv_a1,
           rwkv_a2, rwkv_v0, rwkv_v1, rwkv_v2, rwkv_g1, rwkv_g2, rwkv_k_k, rwkv_k_a, rwkv_r_k, rwkv_ln_g,
           rwkv_ln_b, ffn_w_in, ffn_dw_w, ffn_dw_b, ffn_w_out):
    bf = lambda w: w.astype(BF16)
    pad_c = lambda w: jnp.stack([_pad_cols(m) for m in w]).astype(BF16)
    pad_r = lambda w: jnp.stack([_pad_rows(m) for m in w]).astype(BF16)
    p = dict(norm_g=norm_g, conv_pw1_w=bf(conv_pw1_w), conv_pw1_b=conv_pw1_b, conv_dw_w=conv_dw_w,
             conv_dw_b=conv_dw_b, conv_ln_g=conv_ln_g, conv_ln_b=conv_ln_b, conv_pw2_w=bf(conv_pw2_w),
             conv_pw2_b=conv_pw2_b, rwkv_mix=rwkv_mix, rwkv_w_r=bf(rwkv_w_r), rwkv_w_k=bf(rwkv_w_k),
             rwkv_w_v=bf(rwkv_w_v), rwkv_w_o=bf(rwkv_w_o), rwkv_w0=rwkv_w0, rwkv_w1=pad_c(rwkv_w1),
             rwkv_w2=pad_r(rwkv_w2), rwkv_a0=rwkv_a0, rwkv_a1=pad_c(rwkv_a1), rwkv_a2=pad_r(rwkv_a2),
             rwkv_v0=rwkv_v0, rwkv_v1=pad_c(rwkv_v1), rwkv_v2=pad_r(rwkv_v2), rwkv_g1=pad_c(rwkv_g1),
             rwkv_g2=pad_r(rwkv_g2), rwkv_k_k=rwkv_k_k, rwkv_k_a=rwkv_k_a, rwkv_r_k=rwkv_r_k,
             rwkv_ln_g=rwkv_ln_g, rwkv_ln_b=rwkv_ln_b, ffn_w_in=bf(ffn_w_in), ffn_dw_w=ffn_dw_w,
             ffn_dw_b=ffn_dw_b, ffn_w_out=bf(ffn_w_out))
    y_p, conv_p, shift_p, wkv_p, ffn_p = _trunk(x_prompt, None, None, None, None, p)
    y_s, conv_s, shift_s, wkv_s, ffn_s = _trunk(x_sample, state_conv_mix, state_rwkv_shift, state_rwkv_wkv,
                                                state_ffn_conv, p)
    return (y_p, y_s, conv_p, conv_s, shift_p, shift_s, wkv_p, wkv_s, ffn_p, ffn_s)
```

```python
import functools

import jax
import jax.numpy as jnp
from jax import lax
from jax.experimental import pallas as pl
from jax.experimental.pallas import tpu as pltpu

F32 = jnp.float32
BF16 = jnp.bfloat16

RMS_EPS = 1e-6
LN_EPS = 1e-5
GN_EPS = 64e-5
HEAD_SIZE = 64
CONV_WIDTH = 31
FFN_CONV_WIDTH = 3

LANES = 128
SUBLANES = 8
VMEM_LIMIT_BYTES = 56 * 1024 * 1024

TOKEN_TILE = 512
WKV_CHUNK = 64


def _params(*semantics):
    return pltpu.CompilerParams(dimension_semantics=semantics, vmem_limit_bytes=VMEM_LIMIT_BYTES)


def _rmsnorm(x, g):
    return x * lax.rsqrt(jnp.mean(x * x, axis=-1, keepdims=True) + RMS_EPS) * g


def _row(v):
    return v.reshape(1, -1)


def _ffn_kernel(*refs, S, L, carry):
    if carry:
        (x_ref, g2_ref, g3_ref, wu_ref, wg_ref, dww_ref, dwb_ref, wo_ref,
         o_ref, nb_ref, h_scr, gpx_scr, carry_scr) = refs
    else:
        (x_ref, g2_ref, g3_ref, wu_ref, wg_ref, dww_ref, dwb_ref, wo_ref, halo_ref,
         o_ref, nb_ref, h_scr, gpx_scr) = refs
    i = pl.program_id(0)
    j = pl.program_id(1)
    nj = pl.num_programs(1)
    K = FFN_CONV_WIDTH - 1

    tf = wu_ref.shape[-1]

    @pl.when(j == 0)
    def _():
        h_scr[...] = _rmsnorm(x_ref[...], g2_ref[...]).astype(BF16)
        o_ref[...] = jnp.zeros(o_ref.shape, F32)

    if carry:
        @pl.when(i == 0)
        def _():
            carry_scr[j] = jnp.zeros((SUBLANES, tf), F32)

    h = h_scr[...]
    w = dww_ref[...]
    nsub = 2 if tf % (2 * LANES) == 0 else 1
    ts = tf // nsub
    cols = [slice(s * ts, (s + 1) * ts) for s in range(nsub)]
    up = [(jnp.dot(h, wu_ref[:, cs], preferred_element_type=F32),
           jnp.dot(h, wg_ref[:, cs], preferred_element_type=F32)) for cs in cols]
    contrib = None
    for cs, (u, gp) in zip(cols, up):
        gp3 = gp.reshape(S, L, ts)
        gpx_scr[:, SUBLANES:SUBLANES + L, cs] = gp3
        if carry:
            gpx_scr[0, 0:SUBLANES, cs] = carry_scr[j, :, cs]
        else:
            gpx_scr[:, SUBLANES - K:SUBLANES, cs] = halo_ref[:, :, cs]
        gc = (gpx_scr[:, SUBLANES - 2:SUBLANES - 2 + L, cs] * w[0:1, cs]
              + gpx_scr[:, SUBLANES - 1:SUBLANES - 1 + L, cs] * w[1:2, cs]
              + gp3 * w[2:3, cs] + dwb_ref[:, cs])
        act = (jax.nn.gelu(gc) * u.reshape(S, L, ts)).reshape(S * L, ts).astype(BF16)
        part = jnp.dot(act, wo_ref[cs, :], preferred_element_type=F32)
        contrib = part if contrib is None else contrib + part
    o_ref[...] += contrib

    nb_ref[...] = gpx_scr[:, SUBLANES + L - K:SUBLANES + L, :]
    if carry:
        carry_scr[j] = gpx_scr[0, L:L + SUBLANES, :]

    @pl.when(j == nj - 1)
    def _():
        o_ref[...] = x_ref[...] + _rmsnorm(o_ref[...], g3_ref[...])


def _ffn_chunk(d_ff):
    for tf in (512, 256, 128):
        if d_ff % tf == 0:
            return tf
    raise ValueError(f"d_ff={d_ff} is not a multiple of {LANES}")


def _ffn_call(x, g2, g3, w_in, dw_w, dw_b, w_out, halo, *, seq_len, layer):
    T, D = x.shape
    d_ff = w_out.shape[1]
    tf = _ffn_chunk(d_ff)
    nj = d_ff // tf
    carry = halo is None
    TM = min(TOKEN_TILE, T)
    L = TM if carry else seq_len
    S = TM // L
    assert T % TM == 0 and TM % L == 0 and (carry or L == seq_len)
    nt = T // TM
    in_specs = [
        pl.BlockSpec((TM, D), lambda i, j: (i, 0)),
        pl.BlockSpec((1, D), lambda i, j: (0, 0)),
        pl.BlockSpec((1, D), lambda i, j: (0, 0)),
        pl.BlockSpec((None, D, tf), lambda i, j: (layer, 0, j)),
        pl.BlockSpec((None, D, tf), lambda i, j: (layer, 0, nj + j)),
        pl.BlockSpec((FFN_CONV_WIDTH, tf), lambda i, j: (0, j)),
        pl.BlockSpec((1, tf), lambda i, j: (0, j)),
        pl.BlockSpec((None, tf, D), lambda i, j: (layer, j, 0)),
    ]
    args = [x, _row(g2), _row(g3), w_in, w_in, dw_w, _row(dw_b), w_out]
    scratch = [pltpu.VMEM((TM, D), BF16), pltpu.VMEM((S, SUBLANES + L, tf), F32)]
    if carry:
        nb_spec = pl.BlockSpec((1, FFN_CONV_WIDTH - 1, tf), lambda i, j: (i, 0, j))
        scratch.append(pltpu.VMEM((nj, SUBLANES, tf), F32))
    else:
        in_specs.append(pl.BlockSpec((S, FFN_CONV_WIDTH - 1, tf), lambda i, j: (i, 0, j)))
        args.append(halo)
        nb_spec = pl.BlockSpec((S, FFN_CONV_WIDTH - 1, tf), lambda i, j: (i, 0, j))
    out, nb = pl.pallas_call(
        functools.partial(_ffn_kernel, S=S, L=L, carry=carry),
        grid=(nt, nj),
        in_specs=in_specs,
        out_specs=[pl.BlockSpec((TM, D), lambda i, j: (i, 0)), nb_spec],
        out_shape=[jax.ShapeDtypeStruct((T, D), F32),
                   jax.ShapeDtypeStruct((nt * S, FFN_CONV_WIDTH - 1, d_ff), F32)],
        scratch_shapes=scratch,
        compiler_params=_params("arbitrary", "arbitrary"),
        name="ffn_carry" if carry else "ffn_state",
    )(*args)
    return out, (nb[-1:] if carry else nb)


def _tiling(T, seq_len, carry, tile=TOKEN_TILE):
    TM = min(tile, T)
    L = TM if carry else seq_len
    assert T % TM == 0 and TM % L == 0 and L % SUBLANES == 0
    return TM, L, TM // L


def _glu_kernel(x_ref, g_ref, wa_ref, wb_ref, ba_ref, bb_ref, u_ref, h_scr):
    @pl.when(pl.program_id(1) == 0)
    def _():
        h_scr[...] = _rmsnorm(x_ref[...], g_ref[...]).astype(BF16)

    h = h_scr[...]
    a = jnp.dot(h, wa_ref[...], preferred_element_type=F32) + ba_ref[...]
    b = jnp.dot(h, wb_ref[...], preferred_element_type=F32) + bb_ref[...]
    u_ref[...] = a * jax.nn.sigmoid(b)


def _glu_call(x, g, w1, b1, *, layer):
    T, D = x.shape
    TM = min(TOKEN_TILE, T)
    tn = min(512, D)
    nj = D // tn
    assert T % TM == 0 and D % tn == 0
    return pl.pallas_call(
        _glu_kernel,
        grid=(T // TM, nj),
        in_specs=[
            pl.BlockSpec((TM, D), lambda i, j: (i, 0)),
            pl.BlockSpec((1, D), lambda i, j: (0, 0)),
            pl.BlockSpec((None, D, tn), lambda i, j: (layer, 0, j)),
            pl.BlockSpec((None, D, tn), lambda i, j: (layer, 0, nj + j)),
            pl.BlockSpec((1, tn), lambda i, j: (0, j)),
            pl.BlockSpec((1, tn), lambda i, j: (0, nj + j)),
        ],
        out_specs=pl.BlockSpec((TM, tn), lambda i, j: (i, j)),
        out_shape=jax.ShapeDtypeStruct((T, D), F32),
        scratch_shapes=[pltpu.VMEM((TM, D), BF16)],
        compiler_params=_params("arbitrary", "arbitrary"),
        name="conv_glu",
    )(x, _row(g), w1, w1, _row(b1), _row(b1))


HIST_PAD = 32
DW_ROWS = 64


def _dwconv_kernel(*refs, S, L, carry):
    if carry:
        u_ref, w_ref, b_ref, c_ref, ux_scr, sh_scr, carry_scr = refs
    else:
        u_ref, w_ref, b_ref, st_ref, c_ref, ux_scr, sh_scr = refs
    i = pl.program_id(0)
    j = pl.program_id(1)
    tc = u_ref.shape[-1]
    K = CONV_WIDTH - 1
    ux_scr[:, HIST_PAD:HIST_PAD + L, :] = u_ref[...].reshape(S, L, tc)
    if carry:
        @pl.when(i == 0)
        def _():
            carry_scr[j] = jnp.zeros((HIST_PAD, tc), F32)
        ux_scr[0, 0:HIST_PAD, :] = carry_scr[j]
    else:
        ux_scr[:, HIST_PAD - K:HIST_PAD, :] = st_ref[...]
    nr = HIST_PAD + L - SUBLANES
    for r in range(1, SUBLANES):
        sh_scr[r - 1] = ux_scr[:, r:r + nr, :]
    w = w_ref[...]
    rb = min(DW_ROWS, L)
    for s in range(S):
        for r0 in range(0, L, rb):
            acc = jnp.broadcast_to(b_ref[...], (rb, tc))
            for k in range(CONV_WIDTH):
                m = HIST_PAD - K + k
                r, lo = m % SUBLANES, r0 + m - m % SUBLANES
                tap = ux_scr[s, lo:lo + rb, :] if r == 0 else sh_scr[r - 1, s, lo:lo + rb, :]
                acc = acc + tap * w[k:k + 1, :]
            c_ref[s * L + r0:s * L + r0 + rb, :] = acc
    if carry:
        carry_scr[j] = ux_scr[0, L:L + HIST_PAD, :]


def _dwconv_call(u, w, b, state, *, seq_len):
    T, D = u.shape
    carry = state is None
    TM, L, S = _tiling(T, seq_len, carry)
    tc = min(256, D)
    nc = D // tc
    in_specs = [
        pl.BlockSpec((TM, tc), lambda i, j: (i, j)),
        pl.BlockSpec((CONV_WIDTH, tc), lambda i, j: (0, j)),
        pl.BlockSpec((1, tc), lambda i, j: (0, j)),
    ]
    args = [u, w, _row(b)]
    scratch = [pltpu.VMEM((S, HIST_PAD + L, tc), F32),
               pltpu.VMEM((SUBLANES - 1, S, HIST_PAD + L - SUBLANES, tc), F32)]
    if carry:
        scratch.append(pltpu.VMEM((nc, HIST_PAD, tc), F32))
    else:
        in_specs.append(pl.BlockSpec((S, CONV_WIDTH - 1, tc), lambda i, j: (i, 0, j)))
        args.append(state)
    return pl.pallas_call(
        functools.partial(_dwconv_kernel, S=S, L=L, carry=carry),
        grid=(T // TM, nc),
        in_specs=in_specs,
        out_specs=pl.BlockSpec((TM, tc), lambda i, j: (i, j)),
        out_shape=jax.ShapeDtypeStruct((T, D), F32),
        scratch_shapes=scratch,
        compiler_params=_params("arbitrary", "arbitrary"),
        name="conv_dw_carry" if carry else "conv_dw_state",
    )(*args)


def _proj_kernel(*refs, ln, bias):
    refs = list(refs)
    z_ref, x_ref, w_ref, g_ref = refs[:4]
    rest = refs[4:]
    z = z_ref[...].astype(F32)
    if ln:
        lg_ref, lb_ref = rest[:2]
        rest = rest[2:]
        mu = jnp.mean(z, axis=-1, keepdims=True)
        zc = z - mu
        var = jnp.mean(zc * zc, axis=-1, keepdims=True)
        z = zc * lax.rsqrt(var + LN_EPS) * lg_ref[...] + lb_ref[...]
        z = z * jax.nn.sigmoid(z)
    out = jnp.dot(z.astype(BF16), w_ref[...], preferred_element_type=F32)
    if bias:
        out = out + rest[0][...]
        rest = rest[1:]
    o_ref, = rest
    o_ref[...] = x_ref[...] + _rmsnorm(out, g_ref[...])


PROJ_TILE = 256


def _head_ones():
    r = lax.broadcasted_iota(jnp.int32, (LANES, LANES), 0) // HEAD_SIZE
    c = lax.broadcasted_iota(jnp.int32, (LANES, LANES), 1) // HEAD_SIZE
    return (r == c).astype(BF16)


def _head_sum(x, ones):
    outs = []
    for p in range(x.shape[1] // LANES):
        xp = x[:, p * LANES:(p + 1) * LANES]
        hi = xp.astype(BF16)
        lo = (xp - hi.astype(F32)).astype(BF16)
        outs.append(jnp.dot(hi, ones, preferred_element_type=F32) + jnp.dot(lo, ones, preferred_element_type=F32))
    return outs[0] if len(outs) == 1 else jnp.concatenate(outs, axis=1)


LORA_PAD = 128
RWKV_IN_TILE = 512
RWKV_IN_COLS = 256


def _softplus(z):
    return jnp.maximum(z, 0.0) + jnp.log(1.0 + jnp.exp(-jnp.abs(z)))


MIX_CHUNK = 512


def _rwkv_in_kernel(*refs, S, L, carry, vres):
    refs = list(refs)
    x_ref, g_ref, mix_ref, wr_ref, wk_ref, wv_ref, w1_ref, a1_ref, g1_ref = refs[:9]
    w2_ref, a2_ref, g2_ref, w0_ref, a0_ref, kk_ref, ka_ref = refs[9:16]
    rest = refs[16:]
    if vres:
        v1_ref, v2_ref, v0_ref, vf_ref = rest[:4]
        rest = rest[4:]
    if not carry:
        sh_ref = rest[0]
        rest = rest[1:]
    r_ref, k_ref, v_ref, kkn_ref, a_ref, lw_ref, go_ref, hl_ref = rest[:8]
    rest = rest[8:]
    mx_scr, tw_scr, ta_scr, tg_scr = rest[:4]
    rest = rest[4:]
    if vres:
        tv_scr = rest[0]
        rest = rest[1:]
    if carry:
        carry_scr, = rest
    i = pl.program_id(0)
    j = pl.program_id(1)
    TM, D = x_ref.shape

    @pl.when(j == 0)
    def _():
        x = x_ref[...]
        rs = lax.rsqrt(jnp.mean(x * x, axis=-1, keepdims=True) + RMS_EPS)
        if carry:
            @pl.when(i == 0)
            def _():
                carry_scr[...] = jnp.zeros((1, D), F32)
        mc = min(MIX_CHUNK, D)
        seg_start = lax.broadcasted_iota(jnp.int32, (TM, mc), 0) % L == 0
        lora = {}

        def acc(name, xm, w_ref, cs):
            part = jnp.dot(xm, w_ref[cs, :], preferred_element_type=F32)
            lora[name] = part if name not in lora else lora[name] + part

        for c0 in range(0, D, mc):
            cs = slice(c0, c0 + mc)
            h = x_ref[:, cs] * rs * g_ref[:, cs]
            if carry:
                first = jnp.broadcast_to(carry_scr[:, cs], (TM, mc))
                carry_scr[:, cs] = h[TM - 1:TM, :]
            else:
                first = jnp.broadcast_to(sh_ref[:, :, cs], (S, L, mc)).reshape(TM, mc)
            for s in range(S):
                hl_ref[s, :, cs] = h[s * L + L - 1:s * L + L, :]
            xx = jnp.where(seg_start, first, pltpu.roll(h, 1, axis=0)) - h
            mixed = [(h + xx * mix_ref[m:m + 1, cs]).astype(BF16) for m in range(6)]
            mx_scr[0, :, cs] = mixed[0]
            mx_scr[1, :, cs] = mixed[2]
            mx_scr[2, :, cs] = mixed[3]
            acc('w', mixed[1], w1_ref, cs)
            acc('a', mixed[4], a1_ref, cs)
            acc('g', mixed[5], g1_ref, cs)
            if vres:
                acc('v', mixed[3], v1_ref, cs)
        tw_scr[...] = jnp.tanh(lora['w']).astype(BF16)
        ta_scr[...] = lora['a'].astype(BF16)
        tg_scr[...] = jax.nn.sigmoid(lora['g']).astype(BF16)
        if vres:
            tv_scr[...] = lora['v'].astype(BF16)

    r = jnp.dot(mx_scr[0], wr_ref[...], preferred_element_type=F32)
    k = jnp.dot(mx_scr[1], wk_ref[...], preferred_element_type=F32)
    v = jnp.dot(mx_scr[2], wv_ref[...], preferred_element_type=F32)
    wl = jnp.dot(tw_scr[...], w2_ref[...], preferred_element_type=F32)
    al = jnp.dot(ta_scr[...], a2_ref[...], preferred_element_type=F32)
    go_ref[...] = jnp.dot(tg_scr[...], g2_ref[...], preferred_element_type=F32).astype(go_ref.dtype)
    w = -_softplus(-(w0_ref[...] + wl)) - 0.5
    lw_ref[...] = -jnp.exp(w)
    a = jax.nn.sigmoid(a0_ref[...] + al)
    if vres:
        vl = jnp.dot(tv_scr[...], v2_ref[...], preferred_element_type=F32)
        v = v + (vf_ref[...].astype(F32) - v) * jax.nn.sigmoid(v0_ref[...] + vl)
    kk = k * kk_ref[...]
    norm = jnp.sqrt(_head_sum(kk * kk, _head_ones()))
    kkn_ref[...] = (kk / jnp.maximum(norm, 1e-12)).astype(kkn_ref.dtype)
    r_ref[...] = r.astype(r_ref.dtype)
    k_ref[...] = (k * (1.0 + (a - 1.0) * ka_ref[...])).astype(k_ref.dtype)
    v_ref[...] = v.astype(v_ref.dtype)
    a_ref[...] = a.astype(a_ref.dtype)


WKV_PAIRS = 16


def _mm(a, b):
    return jnp.dot(a.astype(BF16), b.astype(BF16), preferred_element_type=F32)


def _mm_nt(a, b):
    return lax.dot_general(a.astype(BF16), b.astype(BF16), (((1,), (1,)), ((), ())), preferred_element_type=F32)


def _mm_tn(a, b):
    return lax.dot_general(a.astype(BF16), b.astype(BF16), (((0,), (0,)), ((), ())), preferred_element_type=F32)


def _wkv_kernel(r_ref, k_ref, v_ref, kk_ref, a_ref, lw_ref, g_ref, rk_ref, lng_ref, lnb_ref, s_ref,
                z_ref, so_ref, st_scr, *, PP):
    c = pl.program_id(2)
    nc = pl.num_programs(2)
    C = r_ref.shape[0]
    assert 2 * C == LANES and C == HEAD_SIZE

    @pl.when(c == 0)
    def _():
        for pp in range(PP):
            st_scr[pp] = jnp.concatenate([s_ref[0, 2 * pp], s_ref[0, 2 * pp + 1]], axis=1)

    def ld(ref, sl):
        return ref[:, sl].astype(F32)

    row = lax.broadcasted_iota(jnp.int32, (C, LANES), 0)
    lane = lax.broadcasted_iota(jnp.int32, (C, LANES), 1)
    sidx = lane % C
    lo = lane < C
    strict = sidx < row
    incl = sidx <= row
    eye = (sidx == row).astype(F32)
    ones = _head_ones()

    def bd(x):
        z = jnp.zeros_like(x)
        return jnp.concatenate([jnp.where(lo, x, z), jnp.where(lo, z, x)], axis=0)

    lw = lw_ref[...]
    rows_full = lax.broadcasted_iota(jnp.int32, lw.shape, 0)
    lc = lw
    sh = 1
    while sh < C:
        lc = lc + jnp.where(rows_full >= sh, pltpu.roll(lc, sh, axis=0), 0.0)
        sh *= 2

    pairs = range(PP)
    sls = [slice(pp * LANES, (pp + 1) * LANES) for pp in pairs]
    lhs, bh_kh, v_bd, ltot, sp = [], [], [], [], []
    gm = []
    for sl in sls:
        lwp, lp = lw[:, sl], lc[:, sl]
        lt = lp[C - 1:C, :]
        kp, kkp = ld(k_ref, sl), ld(kk_ref, sl)
        bp = kkp * ld(a_ref, sl)
        e_neg = jnp.exp(-lp)
        e_rem = jnp.exp(lt - lp)
        at = -(kkp * jnp.exp(lp - lwp))
        rt = ld(r_ref, sl) * jnp.exp(lp)
        lhs.append(jnp.concatenate([at, rt], axis=0).astype(BF16))
        gm.append(_mm_nt(lhs[-1], jnp.concatenate([bd(bp * e_neg), bd(kp * e_neg)], axis=0)))
        bh_kh.append(jnp.concatenate([bp * e_rem, kp * e_rem], axis=0).astype(BF16))
        v_bd.append(bd(ld(v_ref, sl)).astype(BF16))
        ltot.append(lt)
    n_ab = [jnp.where(strict, g[:C, :LANES], 0.0) for g in gm]
    a_rb = [jnp.where(incl, g[C:, :LANES], 0.0) for g in gm]
    a_k = [jnp.concatenate([jnp.where(strict, g[:C, LANES:], 0.0), jnp.where(incl, g[C:, LANES:], 0.0)], axis=0)
           for g in gm]
    for pp in pairs:
        sp.append(st_scr[pp])
    s_prod = [_mm_nt(lhs[pp], bd(sp[pp])) for pp in pairs]
    av = [_mm(a_k[pp], v_bd[pp]) for pp in pairs]
    pw = n_ab
    tm = [eye + m for m in n_ab]
    pw_bd = [bd(m).astype(BF16) for m in pw]
    n = 2
    while n < C:
        pw = [_mm(pw[pp], pw_bd[pp]) for pp in pairs]
        pw_bd = [bd(m).astype(BF16) for m in pw]
        tm = [tm[pp] + _mm(tm[pp], pw_bd[pp]) for pp in pairs]
        n *= 2
    u = [_mm(tm[pp], bd(s_prod[pp][:C] + av[pp][:C])) for pp in pairs]
    y = [s_prod[pp][C:] + av[pp][C:] + _mm(a_rb[pp], bd(u[pp])) for pp in pairs]
    for pp in pairs:
        zz = _mm_tn(jnp.concatenate([u[pp], ld(v_ref, sls[pp])], axis=0), bh_kh[pp])
        st_scr[pp] = sp[pp] * jnp.exp(ltot[pp]) + jnp.where(lo, zz[:C], zz[C:])
    def head_sums(parts):
        stacked = jnp.concatenate(parts, axis=0) if PP > 1 else parts[0]
        m = _mm(stacked, ones)
        return [m[pp * C:(pp + 1) * C] for pp in pairs]

    mu = head_sums([t * (1.0 / HEAD_SIZE) for t in y])
    yc = [y[pp] - mu[pp] for pp in pairs]
    var = head_sums([t * t * (1.0 / HEAD_SIZE) for t in yc])
    rk_sum = head_sums([ld(r_ref, sl) * ld(k_ref, sl) * rk_ref[:, sl] for sl in sls])
    for pp, sl in enumerate(sls):
        yn = yc[pp] * lax.rsqrt(var[pp] + GN_EPS) * lng_ref[:, sl] + lnb_ref[:, sl]
        z_ref[:, sl] = ((yn + rk_sum[pp] * ld(v_ref, sl)) * ld(g_ref, sl)).astype(z_ref.dtype)

    @pl.when(c == nc - 1)
    def _():
        for pp in range(PP):
            so_ref[0, 2 * pp] = st_scr[pp, :, :C]
            so_ref[0, 2 * pp + 1] = st_scr[pp, :, C:]


def _wkv_call(rkvkal, gate, r_k, ln_g, ln_b, state, *, seq_len):
    T, D = gate.shape
    B = T // seq_len
    C = WKV_CHUNK
    NC = seq_len // C
    PP = min(WKV_PAIRS, D // LANES)
    Q = D // (PP * LANES)
    assert seq_len % C == 0 and D % (PP * LANES) == 0
    tile = pl.BlockSpec((C, PP * LANES), lambda b, q, c: (b * NC + c, q))
    row = pl.BlockSpec((1, PP * LANES), lambda b, q, c: (0, q))
    st = pl.BlockSpec((1, 2 * PP, HEAD_SIZE, HEAD_SIZE), lambda b, q, c: (b, q, 0, 0))
    return pl.pallas_call(
        functools.partial(_wkv_kernel, PP=PP),
        grid=(B, Q, NC),
        in_specs=[tile] * 7 + [row] * 3 + [st],
        out_specs=[tile, st],
        out_shape=[jax.ShapeDtypeStruct((T, D), BF16), jax.ShapeDtypeStruct(state.shape, F32)],
        scratch_shapes=[pltpu.VMEM((PP, HEAD_SIZE, LANES), F32)],
        compiler_params=_params("arbitrary", "arbitrary", "arbitrary"),
        name="rwkv_wkv",
    )(*rkvkal, gate, _row(r_k.reshape(-1)), _row(ln_g), _row(ln_b), state)


def _pad_rank(w, axis):
    pad = [(0, 0)] * w.ndim
    pad[axis] = (0, -w.shape[axis] % LORA_PAD)
    return jnp.pad(w, pad)


def _rwkv_in_call(x, g, p, shift, v_first, *, seq_len, layer):
    T, D = x.shape
    carry = shift is None
    vres = v_first is not None
    vl = layer - 1
    TM, L, S = _tiling(T, seq_len, carry, RWKV_IN_TILE)
    tn = min(RWKV_IN_COLS, D)
    nj = D // tn
    row_d = pl.BlockSpec((1, D), lambda i, j: (0, 0))
    row_n = pl.BlockSpec((1, tn), lambda i, j: (0, j))
    tile_n = pl.BlockSpec((TM, tn), lambda i, j: (i, j))

    def col_w(lyr):
        return pl.BlockSpec((None, D, tn), lambda i, j: (lyr, 0, j))

    def lora1(a, lyr):
        return pl.BlockSpec((None,) + a.shape[1:], lambda i, j: (lyr, 0, 0))

    def lora2(a, lyr):
        return pl.BlockSpec((None, a.shape[1], tn), lambda i, j: (lyr, 0, j))

    in_specs = [pl.BlockSpec((TM, D), lambda i, j: (i, 0)), row_d, pl.BlockSpec((6, D), lambda i, j: (0, 0)),
                col_w(layer), col_w(layer), col_w(layer),
                lora1(p['w1'], layer), lora1(p['a1'], layer), lora1(p['g1'], layer),
                lora2(p['w2'], layer), lora2(p['a2'], layer), lora2(p['g2'], layer),
                row_n, row_n, row_n, row_n]
    args = [x, _row(g), p['mix'][layer], p['w_r'], p['w_k'], p['w_v'], p['w1'], p['a1'], p['g1'],
            p['w2'], p['a2'], p['g2'],
            _row(p['w0'][layer]), _row(p['a0'][layer]), _row(p['k_k'][layer]), _row(p['k_a'][layer])]
    if vres:
        in_specs += [lora1(p['v1'], vl), lora2(p['v2'], vl), row_n, tile_n]
        args += [p['v1'], p['v2'], _row(p['v0'][vl]), v_first]
    if not carry:
        in_specs.append(pl.BlockSpec((S, 1, D), lambda i, j: (i, 0, 0)))
        args.append(shift.reshape(-1, 1, D))
    nseg = (T // TM) * S
    out_specs = [tile_n] * 7 + [pl.BlockSpec((S, 1, D), lambda i, j: (i, 0, 0))]
    out_shape = ([jax.ShapeDtypeStruct((T, D), BF16)] * 5 + [jax.ShapeDtypeStruct((T, D), F32)]
                 + [jax.ShapeDtypeStruct((T, D), BF16), jax.ShapeDtypeStruct((nseg, 1, D), F32)])
    scratch = [pltpu.VMEM((3, TM, D), BF16), pltpu.VMEM((TM, p['w1'].shape[2]), BF16),
               pltpu.VMEM((TM, p['a1'].shape[2]), BF16), pltpu.VMEM((TM, p['g1'].shape[2]), BF16)]
    if vres:
        scratch.append(pltpu.VMEM((TM, p['v1'].shape[2]), BF16))
    if carry:
        scratch.append(pltpu.VMEM((1, D), F32))
    outs = pl.pallas_call(
        functools.partial(_rwkv_in_kernel, S=S, L=L, carry=carry, vres=vres),
        grid=(T // TM, nj),
        in_specs=in_specs,
        out_specs=out_specs,
        out_shape=out_shape,
        scratch_shapes=scratch,
        compiler_params=_params("arbitrary", "arbitrary"),
        name="rwkv_in_carry" if carry else "rwkv_in_state",
    )(*args)
    r, k, v, kk, a, lw, gate, hl = outs
    h_last = hl[-1:, 0, :] if carry else hl[:, 0, :]
    return (r, k, v, kk, a, lw), gate, h_last


def _proj_call(z, x, w, g, *, layer, ln_g=None, ln_b=None, b=None):
    T, D = x.shape
    TM = min(PROJ_TILE, T)
    assert T % TM == 0
    row = pl.BlockSpec((1, D), lambda i: (0, 0))
    tile = pl.BlockSpec((TM, D), lambda i: (i, 0))
    in_specs = [tile, tile, pl.BlockSpec((None, D, D), lambda i: (layer, 0, 0)), row]
    args = [z, x, w, _row(g)]
    if ln_g is not None:
        in_specs += [row, row]
        args += [_row(ln_g), _row(ln_b)]
    if b is not None:
        in_specs.append(row)
        args.append(_row(b))
    return pl.pallas_call(
        functools.partial(_proj_kernel, ln=ln_g is not None, bias=b is not None),
        grid=(T // TM,),
        in_specs=in_specs,
        out_specs=tile,
        out_shape=jax.ShapeDtypeStruct((T, D), F32),
        compiler_params=_params("arbitrary"),
        name="proj_ln" if ln_g is not None else "proj",
    )(*args)


def _trunk(x3, conv_st, shift_st, wkv_st, ffn_st, p):
    B, T, D = x3.shape
    x = x3.reshape(B * T, D)
    depth = p['norm_g'].shape[0]
    new_conv, new_shift, new_wkv, new_ffn = [], [], [], []
    v_first = None
    for i in range(depth):
        g = p['norm_g'][i]
        j = i // 2
        if i % 2 == 0:
            u = _glu_call(x, g[0], p['conv_pw1_w'], p['conv_pw1_b'][j], layer=j)
            c = _dwconv_call(u, p['conv_dw_w'][j], p['conv_dw_b'][j],
                             None if conv_st is None else conv_st[j], seq_len=T)
            new_conv.append(u.reshape(B, T, D)[:, T - (CONV_WIDTH - 1):])
            x = _proj_call(c, x, p['conv_pw2_w'], g[1], layer=j, ln_g=p['conv_ln_g'][j], ln_b=p['conv_ln_b'][j],
                           b=p['conv_pw2_b'][j])
        else:
            q = {k[5:]: v for k, v in p.items() if k.startswith('rwkv_')}
            rkvkal, gate, h_last = _rwkv_in_call(x, g[0], q, None if shift_st is None else shift_st[j],
                                                 v_first if j > 0 else None, seq_len=T, layer=j)
            if j == 0:
                v_first = rkvkal[2]
            s0 = jnp.zeros((B, D // HEAD_SIZE, HEAD_SIZE, HEAD_SIZE), F32) if wkv_st is None else wkv_st[j]
            z, s_new = _wkv_call(rkvkal, gate, q['r_k'][j], q['ln_g'][j], q['ln_b'][j], s0, seq_len=T)
            new_shift.append(h_last)
            new_wkv.append(s_new)
            x = _proj_call(z, x, q['w_o'], g[1], layer=j)
        x, nb = _ffn_call(x, g[2], g[3], p['ffn_w_in'], p['ffn_dw_w'][i], p['ffn_dw_b'][i], p['ffn_w_out'],
                          None if ffn_st is None else ffn_st[i], seq_len=T, layer=i)
        new_ffn.append(nb)
    return (x.reshape(B, T, D), jnp.stack(new_conv), jnp.stack(new_shift), jnp.stack(new_wkv), jnp.stack(new_ffn))


def kernel(x_prompt, x_sample, state_conv_mix, state_rwkv_shift, state_rwkv_wkv, state_ffn_conv, norm_g,
           conv_pw1_w, conv_pw1_b, conv_dw_w, conv_dw_b, conv_ln_g, conv_ln_b, conv_pw2_w, conv_pw2_b,
           rwkv_mix, rwkv_w_r, rwkv_w_k, rwkv_w_v, rwkv_w_o, rwkv_w0, rwkv_w1, rwkv_w2, rwkv_a0, rwkv_a1,
           rwkv_a2, rwkv_v0, rwkv_v1, rwkv_v2, rwkv_g1, rwkv_g2, rwkv_k_k, rwkv_k_a, rwkv_r_k, rwkv_ln_g,
           rwkv_ln_b, ffn_w_in, ffn_dw_w, ffn_dw_b, ffn_w_out):
    bf = lambda w: w.astype(BF16)
    pad_c = lambda w: _pad_rank(w, 2).astype(BF16)
    pad_r = lambda w: _pad_rank(w, 1).astype(BF16)
    p = dict(norm_g=norm_g, conv_pw1_w=bf(conv_pw1_w), conv_pw1_b=conv_pw1_b, conv_dw_w=conv_dw_w,
             conv_dw_b=conv_dw_b, conv_ln_g=conv_ln_g, conv_ln_b=conv_ln_b, conv_pw2_w=bf(conv_pw2_w),
             conv_pw2_b=conv_pw2_b, rwkv_mix=rwkv_mix, rwkv_w_r=bf(rwkv_w_r), rwkv_w_k=bf(rwkv_w_k),
             rwkv_w_v=bf(rwkv_w_v), rwkv_w_o=bf(rwkv_w_o), rwkv_w0=rwkv_w0, rwkv_w1=pad_c(rwkv_w1),
             rwkv_w2=pad_r(rwkv_w2), rwkv_a0=rwkv_a0, rwkv_a1=pad_c(rwkv_a1), rwkv_a2=pad_r(rwkv_a2),
             rwkv_v0=rwkv_v0, rwkv_v1=pad_c(rwkv_v1), rwkv_v2=pad_r(rwkv_v2), rwkv_g1=pad_c(rwkv_g1),
             rwkv_g2=pad_r(rwkv_g2), rwkv_k_k=rwkv_k_k, rwkv_k_a=rwkv_k_a, rwkv_r_k=rwkv_r_k,
             rwkv_ln_g=rwkv_ln_g, rwkv_ln_b=rwkv_ln_b, ffn_w_in=bf(ffn_w_in), ffn_dw_w=ffn_dw_w,
             ffn_dw_b=ffn_dw_b, ffn_w_out=bf(ffn_w_out))
    y_p, conv_p, shift_p, wkv_p, ffn_p = _trunk(x_prompt, None, None, None, None, p)
    y_s, conv_s, shift_s, wkv_s, ffn_s = _trunk(x_sample, state_conv_mix, state_rwkv_shift, state_rwkv_wkv,
                                                state_ffn_conv, p)
    return (y_p, y_s, conv_p, conv_s, shift_p, shift_s, wkv_p, wkv_s, ffn_p, ffn_s)
```

```python
import functools

import jax
import jax.numpy as jnp
from jax import lax
from jax.experimental import pallas as pl
from jax.experimental.pallas import tpu as pltpu

F32 = jnp.float32
BF16 = jnp.bfloat16

RMS_EPS = 1e-6
LN_EPS = 1e-5
GN_EPS = 64e-5
HEAD_SIZE = 64
CONV_WIDTH = 31
FFN_CONV_WIDTH = 3

LANES = 128
SUBLANES = 8
VMEM_LIMIT_BYTES = 56 * 1024 * 1024

TOKEN_TILE = 512
WKV_CHUNK = 64


def _params(*semantics):
    return pltpu.CompilerParams(dimension_semantics=semantics, vmem_limit_bytes=VMEM_LIMIT_BYTES)


def _rmsnorm(x, g):
    return x * lax.rsqrt(jnp.mean(x * x, axis=-1, keepdims=True) + RMS_EPS) * g


def _row(v):
    return v.reshape(1, -1)


def _ffn_kernel(*refs, S, L, carry):
    if carry:
        (x_ref, g2_ref, g3_ref, wu_ref, wg_ref, dww_ref, dwb_ref, wo_ref,
         o_ref, nb_ref, h_scr, gpx_scr, carry_scr) = refs
    else:
        (x_ref, g2_ref, g3_ref, wu_ref, wg_ref, dww_ref, dwb_ref, wo_ref, halo_ref,
         o_ref, nb_ref, h_scr, gpx_scr) = refs
    i = pl.program_id(0)
    j = pl.program_id(1)
    nj = pl.num_programs(1)
    K = FFN_CONV_WIDTH - 1
    tf = wu_ref.shape[-1]

    @pl.when(j == 0)
    def _():
        h_scr[...] = _rmsnorm(x_ref[...], g2_ref[...]).astype(BF16)
        o_ref[...] = jnp.zeros(o_ref.shape, F32)

    if carry:
        @pl.when(i == 0)
        def _():
            carry_scr[j] = jnp.zeros((SUBLANES, tf), F32)

    h = h_scr[...]
    w = dww_ref[...]
    nsub = 2 if tf % (2 * LANES) == 0 else 1
    ts = tf // nsub
    cols = [slice(s * ts, (s + 1) * ts) for s in range(nsub)]
    up = [(jnp.dot(h, wu_ref[:, cs], preferred_element_type=F32),
           jnp.dot(h, wg_ref[:, cs], preferred_element_type=F32)) for cs in cols]
    contrib = None
    for cs, (u, gp) in zip(cols, up):
        gp3 = gp.reshape(S, L, ts)
        gpx_scr[:, SUBLANES:SUBLANES + L, cs] = gp3
        if carry:
            gpx_scr[0, 0:SUBLANES, cs] = carry_scr[j, :, cs]
        else:
            gpx_scr[:, SUBLANES - K:SUBLANES, cs] = halo_ref[:, :, cs]
        gc = (gpx_scr[:, SUBLANES - 2:SUBLANES - 2 + L, cs] * w[0:1, cs]
              + gpx_scr[:, SUBLANES - 1:SUBLANES - 1 + L, cs] * w[1:2, cs]
              + gp3 * w[2:3, cs] + dwb_ref[:, cs])
        act = (jax.nn.gelu(gc) * u.reshape(S, L, ts)).reshape(S * L, ts).astype(BF16)
        part = jnp.dot(act, wo_ref[cs, :], preferred_element_type=F32)
        contrib = part if contrib is None else contrib + part
    o_ref[...] += contrib

    nb_ref[...] = gpx_scr[:, SUBLANES + L - K:SUBLANES + L, :]
    if carry:
        carry_scr[j] = gpx_scr[0, L:L + SUBLANES, :]

    @pl.when(j == nj - 1)
    def _():
        o_ref[...] = x_ref[...] + _rmsnorm(o_ref[...], g3_ref[...])


def _ffn_chunk(d_ff):
    for tf in (512, 256, 128):
        if d_ff % tf == 0:
            return tf
    raise ValueError(f"d_ff={d_ff} is not a multiple of {LANES}")


def _ffn_call(x, g2, g3, w_in, dw_w, dw_b, w_out, halo, *, seq_len, layer):
    T, D = x.shape
    d_ff = w_out.shape[1]
    tf = _ffn_chunk(d_ff)
    nj = d_ff // tf
    carry = halo is None
    TM = min(TOKEN_TILE, T)
    L = TM if carry else seq_len
    S = TM // L
    assert T % TM == 0 and TM % L == 0 and (carry or L == seq_len)
    nt = T // TM
    in_specs = [
        pl.BlockSpec((TM, D), lambda i, j: (i, 0)),
        pl.BlockSpec((1, D), lambda i, j: (0, 0)),
        pl.BlockSpec((1, D), lambda i, j: (0, 0)),
        pl.BlockSpec((None, D, tf), lambda i, j: (layer, 0, j)),
        pl.BlockSpec((None, D, tf), lambda i, j: (layer, 0, nj + j)),
        pl.BlockSpec((FFN_CONV_WIDTH, tf), lambda i, j: (0, j)),
        pl.BlockSpec((1, tf), lambda i, j: (0, j)),
        pl.BlockSpec((None, tf, D), lambda i, j: (layer, j, 0)),
    ]
    args = [x, _row(g2), _row(g3), w_in, w_in, dw_w, _row(dw_b), w_out]
    scratch = [pltpu.VMEM((TM, D), BF16), pltpu.VMEM((S, SUBLANES + L, tf), F32)]
    nb_spec = pl.BlockSpec((S, FFN_CONV_WIDTH - 1, tf), lambda i, j: (i, 0, j))
    if carry:
        scratch.append(pltpu.VMEM((nj, SUBLANES, tf), F32))
    else:
        in_specs.append(nb_spec)
        args.append(halo)
    out, nb = pl.pallas_call(
        functools.partial(_ffn_kernel, S=S, L=L, carry=carry),
        grid=(nt, nj),
        in_specs=in_specs,
        out_specs=[pl.BlockSpec((TM, D), lambda i, j: (i, 0)), nb_spec],
        out_shape=[jax.ShapeDtypeStruct((T, D), F32),
                   jax.ShapeDtypeStruct((nt * S, FFN_CONV_WIDTH - 1, d_ff), F32)],
        scratch_shapes=scratch,
        compiler_params=_params("arbitrary", "arbitrary"),
        name="ffn_carry" if carry else "ffn_state",
    )(*args)
    return out, (nb[-1:] if carry else nb)


def _tiling(T, seq_len, carry, tile=TOKEN_TILE):
    TM = min(tile, T)
    L = TM if carry else seq_len
    assert T % TM == 0 and TM % L == 0 and L % SUBLANES == 0
    return TM, L, TM // L


def _glu_kernel(x_ref, g_ref, wa_ref, wb_ref, ba_ref, bb_ref, u_ref, h_scr):
    @pl.when(pl.program_id(1) == 0)
    def _():
        h_scr[...] = _rmsnorm(x_ref[...], g_ref[...]).astype(BF16)

    h = h_scr[...]
    a = jnp.dot(h, wa_ref[...], preferred_element_type=F32) + ba_ref[...]
    b = jnp.dot(h, wb_ref[...], preferred_element_type=F32) + bb_ref[...]
    u_ref[...] = a * jax.nn.sigmoid(b)


def _glu_call(x, g, w1, b1, *, layer):
    T, D = x.shape
    TM = min(TOKEN_TILE, T)
    tn = min(512, D)
    nj = D // tn
    assert T % TM == 0 and D % tn == 0
    return pl.pallas_call(
        _glu_kernel,
        grid=(T // TM, nj),
        in_specs=[
            pl.BlockSpec((TM, D), lambda i, j: (i, 0)),
            pl.BlockSpec((1, D), lambda i, j: (0, 0)),
            pl.BlockSpec((None, D, tn), lambda i, j: (layer, 0, j)),
            pl.BlockSpec((None, D, tn), lambda i, j: (layer, 0, nj + j)),
            pl.BlockSpec((1, tn), lambda i, j: (0, j)),
            pl.BlockSpec((1, tn), lambda i, j: (0, nj + j)),
        ],
        out_specs=pl.BlockSpec((TM, tn), lambda i, j: (i, j)),
        out_shape=jax.ShapeDtypeStruct((T, D), F32),
        scratch_shapes=[pltpu.VMEM((TM, D), BF16)],
        compiler_params=_params("arbitrary", "arbitrary"),
        name="conv_glu",
    )(x, _row(g), w1, w1, _row(b1), _row(b1))


HIST_PAD = 32
DW_ROWS = 64


def _dwconv_kernel(*refs, S, L, carry):
    if carry:
        u_ref, w_ref, b_ref, c_ref, ux_scr, sh_scr, carry_scr = refs
    else:
        u_ref, w_ref, b_ref, st_ref, c_ref, ux_scr, sh_scr = refs
    i = pl.program_id(0)
    j = pl.program_id(1)
    tc = u_ref.shape[-1]
    K = CONV_WIDTH - 1
    ux_scr[:, HIST_PAD:HIST_PAD + L, :] = u_ref[...].reshape(S, L, tc)
    if carry:
        @pl.when(i == 0)
        def _():
            carry_scr[j] = jnp.zeros((HIST_PAD, tc), F32)
        ux_scr[0, 0:HIST_PAD, :] = carry_scr[j]
    else:
        ux_scr[:, HIST_PAD - K:HIST_PAD, :] = st_ref[...]
    nr = HIST_PAD + L - SUBLANES
    for r in range(1, SUBLANES):
        sh_scr[r - 1] = ux_scr[:, r:r + nr, :]
    w = w_ref[...]
    rb = min(DW_ROWS, L)
    for s in range(S):
        for r0 in range(0, L, rb):
            acc = jnp.broadcast_to(b_ref[...], (rb, tc))
            for k in range(CONV_WIDTH):
                m = HIST_PAD - K + k
                r, lo = m % SUBLANES, r0 + m - m % SUBLANES
                tap = ux_scr[s, lo:lo + rb, :] if r == 0 else sh_scr[r - 1, s, lo:lo + rb, :]
                acc = acc + tap * w[k:k + 1, :]
            c_ref[s * L + r0:s * L + r0 + rb, :] = acc
    if carry:
        carry_scr[j] = ux_scr[0, L:L + HIST_PAD, :]


def _dwconv_call(u, w, b, state, *, seq_len):
    T, D = u.shape
    carry = state is None
    TM, L, S = _tiling(T, seq_len, carry)
    tc = min(256, D)
    nc = D // tc
    in_specs = [
        pl.BlockSpec((TM, tc), lambda i, j: (i, j)),
        pl.BlockSpec((CONV_WIDTH, tc), lambda i, j: (0, j)),
        pl.BlockSpec((1, tc), lambda i, j: (0, j)),
    ]
    args = [u, w, _row(b)]
    scratch = [pltpu.VMEM((S, HIST_PAD + L, tc), F32),
               pltpu.VMEM((SUBLANES - 1, S, HIST_PAD + L - SUBLANES, tc), F32)]
    if carry:
        scratch.append(pltpu.VMEM((nc, HIST_PAD, tc), F32))
    else:
        in_specs.append(pl.BlockSpec((S, CONV_WIDTH - 1, tc), lambda i, j: (i, 0, j)))
        args.append(state)
    return pl.pallas_call(
        functools.partial(_dwconv_kernel, S=S, L=L, carry=carry),
        grid=(T // TM, nc),
        in_specs=in_specs,
        out_specs=pl.BlockSpec((TM, tc), lambda i, j: (i, j)),
        out_shape=jax.ShapeDtypeStruct((T, D), F32),
        scratch_shapes=scratch,
        compiler_params=_params("arbitrary", "arbitrary"),
        name="conv_dw_carry" if carry else "conv_dw_state",
    )(*args)


def _proj_kernel(*refs, ln, bias):
    refs = list(refs)
    z_ref, x_ref, w_ref, g_ref = refs[:4]
    rest = refs[4:]
    z = z_ref[...].astype(F32)
    if ln:
        lg_ref, lb_ref = rest[:2]
        rest = rest[2:]
        mu = jnp.mean(z, axis=-1, keepdims=True)
        zc = z - mu
        var = jnp.mean(zc * zc, axis=-1, keepdims=True)
        z = zc * lax.rsqrt(var + LN_EPS) * lg_ref[...] + lb_ref[...]
        z = z * jax.nn.sigmoid(z)
    out = jnp.dot(z.astype(BF16), w_ref[...], preferred_element_type=F32)
    if bias:
        out = out + rest[0][...]
        rest = rest[1:]
    o_ref, = rest
    o_ref[...] = x_ref[...] + _rmsnorm(out, g_ref[...])


PROJ_TILE = 256


def _head_ones(width=LANES):
    r = lax.broadcasted_iota(jnp.int32, (width, width), 0) // HEAD_SIZE
    c = lax.broadcasted_iota(jnp.int32, (width, width), 1) // HEAD_SIZE
    return (r == c).astype(BF16)


def _head_sum(x, ones):
    outs = []
    for p in range(x.shape[1] // LANES):
        xp = x[:, p * LANES:(p + 1) * LANES]
        hi = xp.astype(BF16)
        lo = (xp - hi.astype(F32)).astype(BF16)
        outs.append(jnp.dot(hi, ones, preferred_element_type=F32) + jnp.dot(lo, ones, preferred_element_type=F32))
    return outs[0] if len(outs) == 1 else jnp.concatenate(outs, axis=1)


LORA_PAD = 128
RWKV_IN_TILE = 512
RWKV_IN_COLS = 256


def _softplus(z):
    return jnp.maximum(z, 0.0) + jnp.log(1.0 + jnp.exp(-jnp.abs(z)))


MIX_CHUNK = 512


def _rwkv_in_kernel(*refs, S, L, carry, vres):
    refs = list(refs)
    x_ref, g_ref, mix_ref, wr_ref, wk_ref, wv_ref, w1_ref, a1_ref, g1_ref = refs[:9]
    w2_ref, a2_ref, g2_ref, w0_ref, a0_ref, kk_ref, ka_ref = refs[9:16]
    rest = refs[16:]
    if vres:
        v1_ref, v2_ref, v0_ref, vf_ref = rest[:4]
        rest = rest[4:]
    if not carry:
        sh_ref = rest[0]
        rest = rest[1:]
    r_ref, k_ref, v_ref, kkn_ref, a_ref, lw_ref, go_ref, hl_ref = rest[:8]
    rest = rest[8:]
    mx_scr, tw_scr, ta_scr, tg_scr = rest[:4]
    rest = rest[4:]
    if vres:
        tv_scr = rest[0]
        rest = rest[1:]
    if carry:
        carry_scr, = rest
    i = pl.program_id(0)
    j = pl.program_id(1)
    TM, D = x_ref.shape

    @pl.when(j == 0)
    def _():
        x = x_ref[...]
        rs = lax.rsqrt(jnp.mean(x * x, axis=-1, keepdims=True) + RMS_EPS)
        if carry:
            @pl.when(i == 0)
            def _():
                carry_scr[...] = jnp.zeros((1, D), F32)
        mc = min(MIX_CHUNK, D)
        seg_start = lax.broadcasted_iota(jnp.int32, (TM, mc), 0) % L == 0
        lora = {}

        def acc(name, xm, w_ref, cs):
            part = jnp.dot(xm, w_ref[cs, :], preferred_element_type=F32)
            lora[name] = part if name not in lora else lora[name] + part

        for c0 in range(0, D, mc):
            cs = slice(c0, c0 + mc)
            h = x_ref[:, cs] * rs * g_ref[:, cs]
            if carry:
                first = jnp.broadcast_to(carry_scr[:, cs], (TM, mc))
                carry_scr[:, cs] = h[TM - 1:TM, :]
            else:
                first = jnp.broadcast_to(sh_ref[:, :, cs], (S, L, mc)).reshape(TM, mc)
            for s in range(S):
                hl_ref[s, :, cs] = h[s * L + L - 1:s * L + L, :]
            xx = jnp.where(seg_start, first, pltpu.roll(h, 1, axis=0)) - h
            mixed = [(h + xx * mix_ref[m:m + 1, cs]).astype(BF16) for m in range(6)]
            mx_scr[0, :, cs] = mixed[0]
            mx_scr[1, :, cs] = mixed[2]
            mx_scr[2, :, cs] = mixed[3]
            acc('w', mixed[1], w1_ref, cs)
            acc('a', mixed[4], a1_ref, cs)
            acc('g', mixed[5], g1_ref, cs)
            if vres:
                acc('v', mixed[3], v1_ref, cs)
        tw_scr[...] = jnp.tanh(lora['w']).astype(BF16)
        ta_scr[...] = lora['a'].astype(BF16)
        tg_scr[...] = jax.nn.sigmoid(lora['g']).astype(BF16)
        if vres:
            tv_scr[...] = lora['v'].astype(BF16)

    wl = jnp.dot(tw_scr[...], w2_ref[...], preferred_element_type=F32)
    al = jnp.dot(ta_scr[...], a2_ref[...], preferred_element_type=F32)
    go_ref[...] = jnp.dot(tg_scr[...], g2_ref[...], preferred_element_type=F32).astype(go_ref.dtype)
    if vres:
        vl = jnp.dot(tv_scr[...], v2_ref[...], preferred_element_type=F32)
    k = jnp.dot(mx_scr[1], wk_ref[...], preferred_element_type=F32)
    r = jnp.dot(mx_scr[0], wr_ref[...], preferred_element_type=F32)
    w = -_softplus(-(w0_ref[...] + wl)) - 0.5
    lw_ref[...] = -jnp.exp(w)
    a = jax.nn.sigmoid(a0_ref[...] + al)
    a_ref[...] = a.astype(a_ref.dtype)
    kk = k * kk_ref[...]
    norm = jnp.sqrt(_head_sum(kk * kk, _head_ones()))
    v = jnp.dot(mx_scr[2], wv_ref[...], preferred_element_type=F32)
    kkn_ref[...] = (kk / jnp.maximum(norm, 1e-12)).astype(kkn_ref.dtype)
    k_ref[...] = (k * (1.0 + (a - 1.0) * ka_ref[...])).astype(k_ref.dtype)
    r_ref[...] = r.astype(r_ref.dtype)
    if vres:
        v = v + (vf_ref[...].astype(F32) - v) * jax.nn.sigmoid(v0_ref[...] + vl)
    v_ref[...] = v.astype(v_ref.dtype)


WKV_GROUP_HEADS = LANES // HEAD_SIZE
WKV_GROUPS = 16
INV_BLOCK = 16


def _mm(a, b):
    return jnp.dot(a.astype(BF16), b.astype(BF16), preferred_element_type=F32)


def _mm_nt(a, b):
    return lax.dot_general(a.astype(BF16), b.astype(BF16), (((1,), (1,)), ((), ())), preferred_element_type=F32)


def _mm_tn(a, b):
    return lax.dot_general(a.astype(BF16), b.astype(BF16), (((0,), (0,)), ((), ())), preferred_element_type=F32)


def _wkv_kernel(r_ref, k_ref, v_ref, kk_ref, a_ref, lw_ref, g_ref, rk_ref, lng_ref, lnb_ref, s_ref,
                z_ref, so_ref, st_scr, *, NG, GH):
    c = pl.program_id(2)
    nc = pl.num_programs(2)
    C = r_ref.shape[0]
    GW = GH * C
    assert C == HEAD_SIZE

    @pl.when(c == 0)
    def _():
        for gi in range(NG):
            st_scr[gi] = jnp.concatenate([s_ref[0, GH * gi + h] for h in range(GH)], axis=1)

    def ld(ref, sl):
        return ref[:, sl].astype(F32)

    row = lax.broadcasted_iota(jnp.int32, (C, GW), 0)
    lane = lax.broadcasted_iota(jnp.int32, (C, GW), 1)
    sidx = lane % C
    head = lane // C
    strict = sidx < row
    incl = sidx <= row
    eye = (sidx == row).astype(F32)
    ones = _head_ones(GW)

    def bd(x):
        z = jnp.zeros_like(x)
        return jnp.concatenate([jnp.where(head == h, x, z) for h in range(GH)], axis=0)

    def diag_blocks(zz):
        out = zz[:C]
        for h in range(1, GH):
            out = jnp.where(head == h, zz[h * C:(h + 1) * C], out)
        return out

    lw = lw_ref[...]
    rows_full = lax.broadcasted_iota(jnp.int32, lw.shape, 0)
    lc = lw
    sh = 1
    while sh < C:
        lc = lc + jnp.where(rows_full >= sh, pltpu.roll(lc, sh, axis=0), 0.0)
        sh *= 2

    pairs = range(NG)
    sls = [slice(gi * GW, (gi + 1) * GW) for gi in pairs]
    lhs, bh_kh, v_bd, ltot, sp = [], [], [], [], []
    gm = []
    for sl in sls:
        lwp, lp = lw[:, sl], lc[:, sl]
        lt = lp[C - 1:C, :]
        kp, kkp = ld(k_ref, sl), ld(kk_ref, sl)
        bp = kkp * ld(a_ref, sl)
        e_neg = jnp.exp(-lp)
        e_rem = jnp.exp(lt - lp)
        at = -(kkp * jnp.exp(lp - lwp))
        rt = ld(r_ref, sl) * jnp.exp(lp)
        lhs.append(jnp.concatenate([at, rt], axis=0).astype(BF16))
        gm.append(_mm_nt(lhs[-1], jnp.concatenate([bd(bp * e_neg), bd(kp * e_neg)], axis=0)))
        bh_kh.append(jnp.concatenate([bp * e_rem, kp * e_rem], axis=0).astype(BF16))
        v_bd.append(bd(ld(v_ref, sl)).astype(BF16))
        ltot.append(lt)
    n_ab = [jnp.where(strict, g[:C, :GW], 0.0) for g in gm]
    a_rb = [jnp.where(incl, g[C:, :GW], 0.0) for g in gm]
    a_k = [jnp.concatenate([jnp.where(strict, g[:C, GW:], 0.0), jnp.where(incl, g[C:, GW:], 0.0)], axis=0)
           for g in gm]
    for pp in pairs:
        sp.append(st_scr[pp])
    s_prod = [_mm_nt(lhs[pp], bd(sp[pp])) for pp in pairs]
    av = [_mm(a_k[pp], v_bd[pp]) for pp in pairs]
    blk = [(row // (INV_BLOCK << lvl)) == (sidx // (INV_BLOCK << lvl)) for lvl in range(3)]
    assert INV_BLOCK << 2 == C
    pw = [jnp.where(blk[0], m, 0.0) for m in n_ab]
    tm = [eye + m for m in pw]
    pw_bd = [bd(m).astype(BF16) for m in pw]
    n = 2
    while n < INV_BLOCK:
        pw = [_mm(pw[pp], pw_bd[pp]) for pp in pairs]
        pw_bd = [bd(m).astype(BF16) for m in pw]
        tm = [tm[pp] + _mm(tm[pp], pw_bd[pp]) for pp in pairs]
        n *= 2
    for lvl in (1, 2):
        off = jnp.logical_and(blk[lvl], jnp.logical_not(blk[lvl - 1]))
        t_bd = [bd(m).astype(BF16) for m in tm]
        tl = [_mm(tm[pp], bd(jnp.where(off, n_ab[pp], 0.0))) for pp in pairs]
        tm = [tm[pp] + _mm(tl[pp], t_bd[pp]) for pp in pairs]
    u = [_mm(tm[pp], bd(s_prod[pp][:C] + av[pp][:C])) for pp in pairs]
    y = [s_prod[pp][C:] + av[pp][C:] + _mm(a_rb[pp], bd(u[pp])) for pp in pairs]
    for pp in pairs:
        zz = _mm_tn(jnp.concatenate([u[pp], ld(v_ref, sls[pp])], axis=0), bh_kh[pp])
        st_scr[pp] = sp[pp] * jnp.exp(ltot[pp]) + diag_blocks(zz)
    def head_sums(parts):
        stacked = jnp.concatenate(parts, axis=0) if NG > 1 else parts[0]
        m = _mm(stacked, ones)
        return [m[pp * C:(pp + 1) * C] for pp in pairs]

    mu = head_sums([t * (1.0 / HEAD_SIZE) for t in y])
    yc = [y[pp] - mu[pp] for pp in pairs]
    var = head_sums([t * t * (1.0 / HEAD_SIZE) for t in yc])
    rk_sum = head_sums([ld(r_ref, sl) * ld(k_ref, sl) * rk_ref[:, sl] for sl in sls])
    for pp, sl in enumerate(sls):
        yn = yc[pp] * lax.rsqrt(var[pp] + GN_EPS) * lng_ref[:, sl] + lnb_ref[:, sl]
        z_ref[:, sl] = ((yn + rk_sum[pp] * ld(v_ref, sl)) * ld(g_ref, sl)).astype(z_ref.dtype)

    @pl.when(c == nc - 1)
    def _():
        for gi in range(NG):
            for h in range(GH):
                so_ref[0, GH * gi + h] = st_scr[gi, :, h * C:(h + 1) * C]


def _wkv_call(rkvkal, gate, r_k, ln_g, ln_b, state, *, seq_len):
    T, D = gate.shape
    B = T // seq_len
    C = WKV_CHUNK
    NC = seq_len // C
    GH = min(WKV_GROUP_HEADS, D // HEAD_SIZE)
    GW = GH * HEAD_SIZE
    NG = min(WKV_GROUPS, D // GW)
    Q = D // (NG * GW)
    assert seq_len % C == 0 and D % (NG * GW) == 0 and GW % LANES == 0
    tile = pl.BlockSpec((C, NG * GW), lambda b, q, c: (b * NC + c, q))
    row = pl.BlockSpec((1, NG * GW), lambda b, q, c: (0, q))
    st = pl.BlockSpec((1, NG * GH, HEAD_SIZE, HEAD_SIZE), lambda b, q, c: (b, q, 0, 0))
    return pl.pallas_call(
        functools.partial(_wkv_kernel, NG=NG, GH=GH),
        grid=(B, Q, NC),
        in_specs=[tile] * 7 + [row] * 3 + [st],
        out_specs=[tile, st],
        out_shape=[jax.ShapeDtypeStruct((T, D), BF16), jax.ShapeDtypeStruct(state.shape, F32)],
        scratch_shapes=[pltpu.VMEM((NG, HEAD_SIZE, GW), F32)],
        compiler_params=_params("arbitrary", "arbitrary", "arbitrary"),
        name="rwkv_wkv",
    )(*rkvkal, gate, _row(r_k.reshape(-1)), _row(ln_g), _row(ln_b), state)


def _pad_rank(w, axis):
    pad = [(0, 0)] * w.ndim
    pad[axis] = (0, -w.shape[axis] % LORA_PAD)
    return jnp.pad(w, pad)


def _rwkv_in_call(x, g, p, shift, v_first, *, seq_len, layer):
    T, D = x.shape
    carry = shift is None
    vres = v_first is not None
    vl = layer - 1
    TM, L, S = _tiling(T, seq_len, carry, RWKV_IN_TILE)
    tn = min(RWKV_IN_COLS, D)
    nj = D // tn
    row_d = pl.BlockSpec((1, D), lambda i, j: (0, 0))
    row_n = pl.BlockSpec((1, tn), lambda i, j: (0, j))
    tile_n = pl.BlockSpec((TM, tn), lambda i, j: (i, j))

    def col_w(lyr):
        return pl.BlockSpec((None, D, tn), lambda i, j: (lyr, 0, j))

    def lora1(a, lyr):
        return pl.BlockSpec((None,) + a.shape[1:], lambda i, j: (lyr, 0, 0))

    def lora2(a, lyr):
        return pl.BlockSpec((None, a.shape[1], tn), lambda i, j: (lyr, 0, j))

    in_specs = [pl.BlockSpec((TM, D), lambda i, j: (i, 0)), row_d, pl.BlockSpec((6, D), lambda i, j: (0, 0)),
                col_w(layer), col_w(layer), col_w(layer),
                lora1(p['w1'], layer), lora1(p['a1'], layer), lora1(p['g1'], layer),
                lora2(p['w2'], layer), lora2(p['a2'], layer), lora2(p['g2'], layer),
                row_n, row_n, row_n, row_n]
    args = [x, _row(g), p['mix'][layer], p['w_r'], p['w_k'], p['w_v'], p['w1'], p['a1'], p['g1'],
            p['w2'], p['a2'], p['g2'],
            _row(p['w0'][layer]), _row(p['a0'][layer]), _row(p['k_k'][layer]), _row(p['k_a'][layer])]
    if vres:
        in_specs += [lora1(p['v1'], vl), lora2(p['v2'], vl), row_n, tile_n]
        args += [p['v1'], p['v2'], _row(p['v0'][vl]), v_first]
    if not carry:
        in_specs.append(pl.BlockSpec((S, 1, D), lambda i, j: (i, 0, 0)))
        args.append(shift.reshape(-1, 1, D))
    nseg = (T // TM) * S
    out_specs = [tile_n] * 7 + [pl.BlockSpec((S, 1, D), lambda i, j: (i, 0, 0))]
    out_shape = ([jax.ShapeDtypeStruct((T, D), BF16)] * 5 + [jax.ShapeDtypeStruct((T, D), F32)]
                 + [jax.ShapeDtypeStruct((T, D), BF16), jax.ShapeDtypeStruct((nseg, 1, D), F32)])
    scratch = [pltpu.VMEM((3, TM, D), BF16), pltpu.VMEM((TM, p['w1'].shape[2]), BF16),
               pltpu.VMEM((TM, p['a1'].shape[2]), BF16), pltpu.VMEM((TM, p['g1'].shape[2]), BF16)]
    if vres:
        scratch.append(pltpu.VMEM((TM, p['v1'].shape[2]), BF16))
    if carry:
        scratch.append(pltpu.VMEM((1, D), F32))
    outs = pl.pallas_call(
        functools.partial(_rwkv_in_kernel, S=S, L=L, carry=carry, vres=vres),
        grid=(T // TM, nj),
        in_specs=in_specs,
        out_specs=out_specs,
        out_shape=out_shape,
        scratch_shapes=scratch,
        compiler_params=_params("arbitrary", "arbitrary"),
        name="rwkv_in_carry" if carry else "rwkv_in_state",
    )(*args)
    r, k, v, kk, a, lw, gate, hl = outs
    h_last = hl[-1:, 0, :] if carry else hl[:, 0, :]
    return (r, k, v, kk, a, lw), gate, h_last


def _proj_call(z, x, w, g, *, layer, ln_g=None, ln_b=None, b=None):
    T, D = x.shape
    TM = min(PROJ_TILE, T)
    assert T % TM == 0
    row = pl.BlockSpec((1, D), lambda i: (0, 0))
    tile = pl.BlockSpec((TM, D), lambda i: (i, 0))
    in_specs = [tile, tile, pl.BlockSpec((None, D, D), lambda i: (layer, 0, 0)), row]
    args = [z, x, w, _row(g)]
    if ln_g is not None:
        in_specs += [row, row]
        args += [_row(ln_g), _row(ln_b)]
    if b is not None:
        in_specs.append(row)
        args.append(_row(b))
    return pl.pallas_call(
        functools.partial(_proj_kernel, ln=ln_g is not None, bias=b is not None),
        grid=(T // TM,),
        in_specs=in_specs,
        out_specs=tile,
        out_shape=jax.ShapeDtypeStruct((T, D), F32),
        compiler_params=_params("arbitrary"),
        name="proj_ln" if ln_g is not None else "proj",
    )(*args)


def _trunk(x3, conv_st, shift_st, wkv_st, ffn_st, p):
    B, T, D = x3.shape
    x = x3.reshape(B * T, D)
    depth = p['norm_g'].shape[0]
    new_conv, new_shift, new_wkv, new_ffn = [], [], [], []
    v_first = None
    for i in range(depth):
        g = p['norm_g'][i]
        j = i // 2
        if i % 2 == 0:
            u = _glu_call(x, g[0], p['conv_pw1_w'], p['conv_pw1_b'][j], layer=j)
            c = _dwconv_call(u, p['conv_dw_w'][j], p['conv_dw_b'][j],
                             None if conv_st is None else conv_st[j], seq_len=T)
            new_conv.append(u.reshape(B, T, D)[:, T - (CONV_WIDTH - 1):])
            x = _proj_call(c, x, p['conv_pw2_w'], g[1], layer=j, ln_g=p['conv_ln_g'][j], ln_b=p['conv_ln_b'][j],
                           b=p['conv_pw2_b'][j])
        else:
            q = {k[5:]: v for k, v in p.items() if k.startswith('rwkv_')}
            rkvkal, gate, h_last = _rwkv_in_call(x, g[0], q, None if shift_st is None else shift_st[j],
                                                 v_first if j > 0 else None, seq_len=T, layer=j)
            if j == 0:
                v_first = rkvkal[2]
            s0 = jnp.zeros((B, D // HEAD_SIZE, HEAD_SIZE, HEAD_SIZE), F32) if wkv_st is None else wkv_st[j]
            z, s_new = _wkv_call(rkvkal, gate, q['r_k'][j], q['ln_g'][j], q['ln_b'][j], s0, seq_len=T)
            new_shift.append(h_last)
            new_wkv.append(s_new)
            x = _proj_call(z, x, q['w_o'], g[1], layer=j)
        x, nb = _ffn_call(x, g[2], g[3], p['ffn_w_in'], p['ffn_dw_w'][i], p['ffn_dw_b'][i], p['ffn_w_out'],
                          None if ffn_st is None else ffn_st[i], seq_len=T, layer=i)
        new_ffn.append(nb)
    return (x.reshape(B, T, D), jnp.stack(new_conv), jnp.stack(new_shift), jnp.stack(new_wkv), jnp.stack(new_ffn))


def kernel(x_prompt, x_sample, state_conv_mix, state_rwkv_shift, state_rwkv_wkv, state_ffn_conv, norm_g,
           conv_pw1_w, conv_pw1_b, conv_dw_w, conv_dw_b, conv_ln_g, conv_ln_b, conv_pw2_w, conv_pw2_b,
           rwkv_mix, rwkv_w_r, rwkv_w_k, rwkv_w_v, rwkv_w_o, rwkv_w0, rwkv_w1, rwkv_w2, rwkv_a0, rwkv_a1,
           rwkv_a2, rwkv_v0, rwkv_v1, rwkv_v2, rwkv_g1, rwkv_g2, rwkv_k_k, rwkv_k_a, rwkv_r_k, rwkv_ln_g,
           rwkv_ln_b, ffn_w_in, ffn_dw_w, ffn_dw_b, ffn_w_out):
    bf = lambda w: w.astype(BF16)
    pad_c = lambda w: _pad_rank(w, 2).astype(BF16)
    pad_r = lambda w: _pad_rank(w, 1).astype(BF16)
    p = dict(norm_g=norm_g, conv_pw1_w=bf(conv_pw1_w), conv_pw1_b=conv_pw1_b, conv_dw_w=conv_dw_w,
             conv_dw_b=conv_dw_b, conv_ln_g=conv_ln_g, conv_ln_b=conv_ln_b, conv_pw2_w=bf(conv_pw2_w),
             conv_pw2_b=conv_pw2_b, rwkv_mix=rwkv_mix, rwkv_w_r=bf(rwkv_w_r), rwkv_w_k=bf(rwkv_w_k),
             rwkv_w_v=bf(rwkv_w_v), rwkv_w_o=bf(rwkv_w_o), rwkv_w0=rwkv_w0, rwkv_w1=pad_c(rwkv_w1),
             rwkv_w2=pad_r(rwkv_w2), rwkv_a0=rwkv_a0, rwkv_a1=pad_c(rwkv_a1), rwkv_a2=pad_r(rwkv_a2),
             rwkv_v0=rwkv_v0, rwkv_v1=pad_c(rwkv_v1), rwkv_v2=pad_r(rwkv_v2), rwkv_g1=pad_c(rwkv_g1),
             rwkv_g2=pad_r(rwkv_g2), rwkv_k_k=rwkv_k_k, rwkv_k_a=rwkv_k_a, rwkv_r_k=rwkv_r_k,
             rwkv_ln_g=rwkv_ln_g, rwkv_ln_b=rwkv_ln_b, ffn_w_in=bf(ffn_w_in), ffn_dw_w=ffn_dw_w,
             ffn_dw_b=ffn_dw_b, ffn_w_out=bf(ffn_w_out))
    y_p, conv_p, shift_p, wkv_p, ffn_p = _trunk(x_prompt, None, None, None, None, p)
    y_s, conv_s, shift_s, wkv_s, ffn_s = _trunk(x_sample, state_conv_mix, state_rwkv_shift, state_rwkv_wkv,
                                                state_ffn_conv, p)
    return (y_p, y_s, conv_p, conv_s, shift_p, shift_s, wkv_p, wkv_s, ffn_p, ffn_s)
```

```python
import functools

import jax
import jax.numpy as jnp
from jax import lax
from jax.experimental import pallas as pl
from jax.experimental.pallas import tpu as pltpu

F32 = jnp.float32
BF16 = jnp.bfloat16

RMS_EPS = 1e-6
LN_EPS = 1e-5
GN_EPS = 64e-5
HEAD_SIZE = 64
CONV_WIDTH = 31
FFN_CONV_WIDTH = 3

LANES = 128
SUBLANES = 8
VMEM_LIMIT_BYTES = 56 * 1024 * 1024

TOKEN_TILE = 512
WKV_CHUNK = 64


def _params(*semantics):
    return pltpu.CompilerParams(dimension_semantics=semantics, vmem_limit_bytes=VMEM_LIMIT_BYTES)


def _rmsnorm(x, g):
    return x * lax.rsqrt(jnp.mean(x * x, axis=-1, keepdims=True) + RMS_EPS) * g


def _row(v):
    return v.reshape(1, -1)


def _ffn_kernel(*refs, S, L, carry):
    if carry:
        (x_ref, g2_ref, g3_ref, wu_ref, wg_ref, dww_ref, dwb_ref, wo_ref,
         o_ref, nb_ref, h_scr, gpx_scr, carry_scr) = refs
    else:
        (x_ref, g2_ref, g3_ref, wu_ref, wg_ref, dww_ref, dwb_ref, wo_ref, halo_ref,
         o_ref, nb_ref, h_scr, gpx_scr) = refs
    i = pl.program_id(0)
    j = pl.program_id(1)
    nj = pl.num_programs(1)
    K = FFN_CONV_WIDTH - 1
    tf = wu_ref.shape[-1]

    @pl.when(j == 0)
    def _():
        h_scr[...] = _rmsnorm(x_ref[...], g2_ref[...]).astype(BF16)
        o_ref[...] = jnp.zeros(o_ref.shape, F32)

    if carry:
        @pl.when(i == 0)
        def _():
            carry_scr[j] = jnp.zeros((SUBLANES, tf), F32)

    h = h_scr[...]
    w = dww_ref[...]
    nsub = 2 if tf % (2 * LANES) == 0 else 1
    ts = tf // nsub
    cols = [slice(s * ts, (s + 1) * ts) for s in range(nsub)]
    up = [(jnp.dot(h, wu_ref[:, cs], preferred_element_type=F32),
           jnp.dot(h, wg_ref[:, cs], preferred_element_type=F32)) for cs in cols]
    for cs, (u, gp) in zip(cols, up):
        gp3 = gp.reshape(S, L, ts)
        gpx_scr[:, SUBLANES:SUBLANES + L, cs] = gp3
        if carry:
            gpx_scr[0, 0:SUBLANES, cs] = carry_scr[j, :, cs]
        else:
            gpx_scr[:, SUBLANES - K:SUBLANES, cs] = halo_ref[:, :, cs]
        gc = (gpx_scr[:, SUBLANES - 2:SUBLANES - 2 + L, cs] * w[0:1, cs]
              + gpx_scr[:, SUBLANES - 1:SUBLANES - 1 + L, cs] * w[1:2, cs]
              + gp3 * w[2:3, cs] + dwb_ref[:, cs])
        act = (jax.nn.gelu(gc) * u.reshape(S, L, ts)).reshape(S * L, ts).astype(BF16)
        o_ref[...] += jnp.dot(act, wo_ref[cs, :], preferred_element_type=F32)

    nb_ref[...] = gpx_scr[:, SUBLANES + L - K:SUBLANES + L, :]
    if carry:
        carry_scr[j] = gpx_scr[0, L:L + SUBLANES, :]

    @pl.when(j == nj - 1)
    def _():
        o_ref[...] = x_ref[...] + _rmsnorm(o_ref[...], g3_ref[...])


def _ffn_chunk(d_ff):
    for tf in (512, 256, 128):
        if d_ff % tf == 0:
            return tf
    raise ValueError(f"d_ff={d_ff} is not a multiple of {LANES}")


def _ffn_call(x, g2, g3, w_in, dw_w, dw_b, w_out, halo, *, seq_len, layer):
    T, D = x.shape
    d_ff = w_out.shape[1]
    tf = _ffn_chunk(d_ff)
    nj = d_ff // tf
    carry = halo is None
    TM = min(TOKEN_TILE, T)
    L = TM if carry else seq_len
    S = TM // L
    assert T % TM == 0 and TM % L == 0 and (carry or L == seq_len)
    nt = T // TM
    in_specs = [
        pl.BlockSpec((TM, D), lambda i, j: (i, 0)),
        pl.BlockSpec((1, D), lambda i, j: (0, 0)),
        pl.BlockSpec((1, D), lambda i, j: (0, 0)),
        pl.BlockSpec((None, D, tf), lambda i, j: (layer, 0, j)),
        pl.BlockSpec((None, D, tf), lambda i, j: (layer, 0, nj + j)),
        pl.BlockSpec((FFN_CONV_WIDTH, tf), lambda i, j: (0, j)),
        pl.BlockSpec((1, tf), lambda i, j: (0, j)),
        pl.BlockSpec((None, tf, D), lambda i, j: (layer, j, 0)),
    ]
    args = [x, _row(g2), _row(g3), w_in, w_in, dw_w, _row(dw_b), w_out]
    scratch = [pltpu.VMEM((TM, D), BF16), pltpu.VMEM((S, SUBLANES + L, tf), F32)]
    nb_spec = pl.BlockSpec((S, FFN_CONV_WIDTH - 1, tf), lambda i, j: (i, 0, j))
    if carry:
        scratch.append(pltpu.VMEM((nj, SUBLANES, tf), F32))
    else:
        in_specs.append(nb_spec)
        args.append(halo)
    out, nb = pl.pallas_call(
        functools.partial(_ffn_kernel, S=S, L=L, carry=carry),
        grid=(nt, nj),
        in_specs=in_specs,
        out_specs=[pl.BlockSpec((TM, D), lambda i, j: (i, 0)), nb_spec],
        out_shape=[jax.ShapeDtypeStruct((T, D), F32),
                   jax.ShapeDtypeStruct((nt * S, FFN_CONV_WIDTH - 1, d_ff), F32)],
        scratch_shapes=scratch,
        compiler_params=_params("arbitrary", "arbitrary"),
        name="ffn_carry" if carry else "ffn_state",
    )(*args)
    return out, (nb[-1:] if carry else nb)


def _tiling(T, seq_len, carry, tile=TOKEN_TILE):
    TM = min(tile, T)
    L = TM if carry else seq_len
    assert T % TM == 0 and TM % L == 0 and L % SUBLANES == 0
    return TM, L, TM // L


def _glu_kernel(x_ref, g_ref, wa_ref, wb_ref, ba_ref, bb_ref, u_ref, h_scr):
    @pl.when(pl.program_id(1) == 0)
    def _():
        h_scr[...] = _rmsnorm(x_ref[...], g_ref[...]).astype(BF16)

    h = h_scr[...]
    a = jnp.dot(h, wa_ref[...], preferred_element_type=F32) + ba_ref[...]
    b = jnp.dot(h, wb_ref[...], preferred_element_type=F32) + bb_ref[...]
    u_ref[...] = a * jax.nn.sigmoid(b)


def _glu_call(x, g, w1, b1, *, layer):
    T, D = x.shape
    TM = min(TOKEN_TILE, T)
    tn = min(512, D)
    nj = D // tn
    assert T % TM == 0 and D % tn == 0
    return pl.pallas_call(
        _glu_kernel,
        grid=(T // TM, nj),
        in_specs=[
            pl.BlockSpec((TM, D), lambda i, j: (i, 0)),
            pl.BlockSpec((1, D), lambda i, j: (0, 0)),
            pl.BlockSpec((None, D, tn), lambda i, j: (layer, 0, j)),
            pl.BlockSpec((None, D, tn), lambda i, j: (layer, 0, nj + j)),
            pl.BlockSpec((1, tn), lambda i, j: (0, j)),
            pl.BlockSpec((1, tn), lambda i, j: (0, nj + j)),
        ],
        out_specs=pl.BlockSpec((TM, tn), lambda i, j: (i, j)),
        out_shape=jax.ShapeDtypeStruct((T, D), F32),
        scratch_shapes=[pltpu.VMEM((TM, D), BF16)],
        compiler_params=_params("arbitrary", "arbitrary"),
        name="conv_glu",
    )(x, _row(g), w1, w1, _row(b1), _row(b1))


HIST_PAD = 32
DW_ROWS = 64


def _dwconv_kernel(*refs, S, L, carry):
    if carry:
        u_ref, w_ref, b_ref, c_ref, ux_scr, sh_scr, carry_scr = refs
    else:
        u_ref, w_ref, b_ref, st_ref, c_ref, ux_scr, sh_scr = refs
    i = pl.program_id(0)
    j = pl.program_id(1)
    tc = u_ref.shape[-1]
    K = CONV_WIDTH - 1
    ux_scr[:, HIST_PAD:HIST_PAD + L, :] = u_ref[...].reshape(S, L, tc)
    if carry:
        @pl.when(i == 0)
        def _():
            carry_scr[j] = jnp.zeros((HIST_PAD, tc), F32)
        ux_scr[0, 0:HIST_PAD, :] = carry_scr[j]
    else:
        ux_scr[:, HIST_PAD - K:HIST_PAD, :] = st_ref[...]
    nr = HIST_PAD + L - SUBLANES
    for r in range(1, SUBLANES):
        sh_scr[r - 1] = ux_scr[:, r:r + nr, :]
    w = w_ref[...]
    rb = min(DW_ROWS, L)
    for s in range(S):
        for r0 in range(0, L, rb):
            acc = jnp.broadcast_to(b_ref[...], (rb, tc))
            for k in range(CONV_WIDTH):
                m = HIST_PAD - K + k
                r, lo = m % SUBLANES, r0 + m - m % SUBLANES
                tap = ux_scr[s, lo:lo + rb, :] if r == 0 else sh_scr[r - 1, s, lo:lo + rb, :]
                acc = acc + tap * w[k:k + 1, :]
            c_ref[s * L + r0:s * L + r0 + rb, :] = acc
    if carry:
        carry_scr[j] = ux_scr[0, L:L + HIST_PAD, :]


def _dwconv_call(u, w, b, state, *, seq_len):
    T, D = u.shape
    carry = state is None
    TM, L, S = _tiling(T, seq_len, carry)
    tc = min(256, D)
    nc = D // tc
    in_specs = [
        pl.BlockSpec((TM, tc), lambda i, j: (i, j)),
        pl.BlockSpec((CONV_WIDTH, tc), lambda i, j: (0, j)),
        pl.BlockSpec((1, tc), lambda i, j: (0, j)),
    ]
    args = [u, w, _row(b)]
    scratch = [pltpu.VMEM((S, HIST_PAD + L, tc), F32),
               pltpu.VMEM((SUBLANES - 1, S, HIST_PAD + L - SUBLANES, tc), F32)]
    if carry:
        scratch.append(pltpu.VMEM((nc, HIST_PAD, tc), F32))
    else:
        in_specs.append(pl.BlockSpec((S, CONV_WIDTH - 1, tc), lambda i, j: (i, 0, j)))
        args.append(state)
    return pl.pallas_call(
        functools.partial(_dwconv_kernel, S=S, L=L, carry=carry),
        grid=(T // TM, nc),
        in_specs=in_specs,
        out_specs=pl.BlockSpec((TM, tc), lambda i, j: (i, j)),
        out_shape=jax.ShapeDtypeStruct((T, D), F32),
        scratch_shapes=scratch,
        compiler_params=_params("arbitrary", "arbitrary"),
        name="conv_dw_carry" if carry else "conv_dw_state",
    )(*args)


def _proj_kernel(*refs, ln, bias):
    refs = list(refs)
    z_ref, x_ref, w_ref, g_ref = refs[:4]
    rest = refs[4:]
    z = z_ref[...].astype(F32)
    if ln:
        lg_ref, lb_ref = rest[:2]
        rest = rest[2:]
        mu = jnp.mean(z, axis=-1, keepdims=True)
        zc = z - mu
        var = jnp.mean(zc * zc, axis=-1, keepdims=True)
        z = zc * lax.rsqrt(var + LN_EPS) * lg_ref[...] + lb_ref[...]
        z = z * jax.nn.sigmoid(z)
    out = jnp.dot(z.astype(BF16), w_ref[...], preferred_element_type=F32)
    if bias:
        out = out + rest[0][...]
        rest = rest[1:]
    o_ref, = rest
    o_ref[...] = x_ref[...] + _rmsnorm(out, g_ref[...])


PROJ_TILE = 256


def _head_ones(width=LANES):
    r = lax.broadcasted_iota(jnp.int32, (width, width), 0) // HEAD_SIZE
    c = lax.broadcasted_iota(jnp.int32, (width, width), 1) // HEAD_SIZE
    return (r == c).astype(BF16)


def _head_sum(x, ones):
    outs = []
    for p in range(x.shape[1] // LANES):
        xp = x[:, p * LANES:(p + 1) * LANES]
        hi = xp.astype(BF16)
        lo = (xp - hi.astype(F32)).astype(BF16)
        outs.append(jnp.dot(hi, ones, preferred_element_type=F32) + jnp.dot(lo, ones, preferred_element_type=F32))
    return outs[0] if len(outs) == 1 else jnp.concatenate(outs, axis=1)


LORA_PAD = 128
RWKV_IN_TILE = 512
RWKV_IN_COLS = 256


def _softplus(z):
    return jnp.maximum(z, 0.0) + jnp.log(1.0 + jnp.exp(-jnp.abs(z)))


MIX_CHUNK = 512


def _rwkv_in_kernel(*refs, S, L, carry, vres):
    refs = list(refs)
    x_ref, g_ref, mix_ref, wr_ref, wk_ref, wv_ref, w1_ref, a1_ref, g1_ref = refs[:9]
    w2_ref, a2_ref, g2_ref, w0_ref, a0_ref, kk_ref, ka_ref = refs[9:16]
    rest = refs[16:]
    if vres:
        v1_ref, v2_ref, v0_ref, vf_ref = rest[:4]
        rest = rest[4:]
    if not carry:
        sh_ref = rest[0]
        rest = rest[1:]
    r_ref, k_ref, v_ref, kkn_ref, a_ref, lw_ref, go_ref, hl_ref = rest[:8]
    rest = rest[8:]
    mx_scr, tw_scr, ta_scr, tg_scr = rest[:4]
    rest = rest[4:]
    if vres:
        tv_scr = rest[0]
        rest = rest[1:]
    if carry:
        carry_scr, = rest
    i = pl.program_id(0)
    j = pl.program_id(1)
    TM, D = x_ref.shape

    @pl.when(j == 0)
    def _():
        x = x_ref[...]
        rs = lax.rsqrt(jnp.mean(x * x, axis=-1, keepdims=True) + RMS_EPS)
        if carry:
            @pl.when(i == 0)
            def _():
                carry_scr[...] = jnp.zeros((1, D), F32)
        mc = min(MIX_CHUNK, D)
        seg_start = lax.broadcasted_iota(jnp.int32, (TM, mc), 0) % L == 0
        lora = {}

        def acc(name, xm, w_ref, cs):
            part = jnp.dot(xm, w_ref[cs, :], preferred_element_type=F32)
            lora[name] = part if name not in lora else lora[name] + part

        for c0 in range(0, D, mc):
            cs = slice(c0, c0 + mc)
            h = x_ref[:, cs] * rs * g_ref[:, cs]
            if carry:
                first = jnp.broadcast_to(carry_scr[:, cs], (TM, mc))
                carry_scr[:, cs] = h[TM - 1:TM, :]
            else:
                first = jnp.broadcast_to(sh_ref[:, :, cs], (S, L, mc)).reshape(TM, mc)
            for s in range(S):
                hl_ref[s, :, cs] = h[s * L + L - 1:s * L + L, :]
            xx = jnp.where(seg_start, first, pltpu.roll(h, 1, axis=0)) - h
            mixed = [(h + xx * mix_ref[m:m + 1, cs]).astype(BF16) for m in range(6)]
            mx_scr[0, :, cs] = mixed[0]
            mx_scr[1, :, cs] = mixed[2]
            mx_scr[2, :, cs] = mixed[3]
            acc('w', mixed[1], w1_ref, cs)
            acc('a', mixed[4], a1_ref, cs)
            acc('g', mixed[5], g1_ref, cs)
            if vres:
                acc('v', mixed[3], v1_ref, cs)
        tw_scr[...] = jnp.tanh(lora['w']).astype(BF16)
        ta_scr[...] = lora['a'].astype(BF16)
        tg_scr[...] = jax.nn.sigmoid(lora['g']).astype(BF16)
        if vres:
            tv_scr[...] = lora['v'].astype(BF16)

    r = jnp.dot(mx_scr[0], wr_ref[...], preferred_element_type=F32)
    k = jnp.dot(mx_scr[1], wk_ref[...], preferred_element_type=F32)
    v = jnp.dot(mx_scr[2], wv_ref[...], preferred_element_type=F32)
    wl = jnp.dot(tw_scr[...], w2_ref[...], preferred_element_type=F32)
    al = jnp.dot(ta_scr[...], a2_ref[...], preferred_element_type=F32)
    go_ref[...] = jnp.dot(tg_scr[...], g2_ref[...], preferred_element_type=F32).astype(go_ref.dtype)
    w = -_softplus(-(w0_ref[...] + wl)) - 0.5
    lw_ref[...] = -jnp.exp(w)
    a = jax.nn.sigmoid(a0_ref[...] + al)
    if vres:
        vl = jnp.dot(tv_scr[...], v2_ref[...], preferred_element_type=F32)
        v = v + (vf_ref[...].astype(F32) - v) * jax.nn.sigmoid(v0_ref[...] + vl)
    kk = k * kk_ref[...]
    norm = jnp.sqrt(_head_sum(kk * kk, _head_ones()))
    kkn_ref[...] = (kk / jnp.maximum(norm, 1e-12)).astype(kkn_ref.dtype)
    r_ref[...] = r.astype(r_ref.dtype)
    k_ref[...] = (k * (1.0 + (a - 1.0) * ka_ref[...])).astype(k_ref.dtype)
    v_ref[...] = v.astype(v_ref.dtype)
    a_ref[...] = a.astype(a_ref.dtype)


WKV_GROUP_HEADS = LANES // HEAD_SIZE
WKV_GROUPS = 16
INV_BLOCK = 16


def _mm(a, b):
    return jnp.dot(a.astype(BF16), b.astype(BF16), preferred_element_type=F32)


def _mm_nt(a, b):
    return lax.dot_general(a.astype(BF16), b.astype(BF16), (((1,), (1,)), ((), ())), preferred_element_type=F32)


def _mm_tn(a, b):
    return lax.dot_general(a.astype(BF16), b.astype(BF16), (((0,), (0,)), ((), ())), preferred_element_type=F32)


def _wkv_kernel(r_ref, k_ref, v_ref, kk_ref, a_ref, lw_ref, g_ref, rk_ref, lng_ref, lnb_ref, s_ref,
                z_ref, so_ref, st_scr, *, NG, GH):
    c = pl.program_id(2)
    nc = pl.num_programs(2)
    C = r_ref.shape[0]
    GW = GH * C
    assert C == HEAD_SIZE

    @pl.when(c == 0)
    def _():
        for gi in range(NG):
            st_scr[gi] = jnp.concatenate([s_ref[0, GH * gi + h] for h in range(GH)], axis=1)

    def ld(ref, sl):
        return ref[:, sl].astype(F32)

    row = lax.broadcasted_iota(jnp.int32, (C, GW), 0)
    lane = lax.broadcasted_iota(jnp.int32, (C, GW), 1)
    sidx = lane % C
    head = lane // C
    strict = sidx < row
    incl = sidx <= row
    eye = (sidx == row).astype(F32)
    ones = _head_ones(GW)

    def bd(x):
        z = jnp.zeros_like(x)
        return jnp.concatenate([jnp.where(head == h, x, z) for h in range(GH)], axis=0)

    def diag_blocks(zz):
        out = zz[:C]
        for h in range(1, GH):
            out = jnp.where(head == h, zz[h * C:(h + 1) * C], out)
        return out

    lw = lw_ref[...]
    rows_full = lax.broadcasted_iota(jnp.int32, lw.shape, 0)
    lc = lw
    sh = 1
    while sh < C:
        lc = lc + jnp.where(rows_full >= sh, pltpu.roll(lc, sh, axis=0), 0.0)
        sh *= 2

    pairs = range(NG)
    sls = [slice(gi * GW, (gi + 1) * GW) for gi in pairs]
    lhs, bh_kh, v_bd, ltot, sp = [], [], [], [], []
    gm = []
    for sl in sls:
        lwp, lp = lw[:, sl], lc[:, sl]
        lt = lp[C - 1:C, :]
        kp, kkp = ld(k_ref, sl), ld(kk_ref, sl)
        bp = kkp * ld(a_ref, sl)
        e_neg = jnp.exp(-lp)
        e_rem = jnp.exp(lt - lp)
        at = -(kkp * jnp.exp(lp - lwp))
        rt = ld(r_ref, sl) * jnp.exp(lp)
        lhs.append(jnp.concatenate([at, rt], axis=0).astype(BF16))
        gm.append(_mm_nt(lhs[-1], jnp.concatenate([bd(bp * e_neg), bd(kp * e_neg)], axis=0)))
        bh_kh.append(jnp.concatenate([bp * e_rem, kp * e_rem], axis=0).astype(BF16))
        v_bd.append(bd(ld(v_ref, sl)).astype(BF16))
        ltot.append(lt)
    n_ab = [jnp.where(strict, g[:C, :GW], 0.0) for g in gm]
    a_rb = [jnp.where(incl, g[C:, :GW], 0.0) for g in gm]
    a_k = [jnp.concatenate([jnp.where(strict, g[:C, GW:], 0.0), jnp.where(incl, g[C:, GW:], 0.0)], axis=0)
           for g in gm]
    for pp in pairs:
        sp.append(st_scr[pp])
    s_prod = [_mm_nt(lhs[pp], bd(sp[pp])) for pp in pairs]
    av = [_mm(a_k[pp], v_bd[pp]) for pp in pairs]
    blk = [(row // (INV_BLOCK << lvl)) == (sidx // (INV_BLOCK << lvl)) for lvl in range(3)]
    assert INV_BLOCK << 2 == C
    pw = [jnp.where(blk[0], m, 0.0) for m in n_ab]
    tm = [eye + m for m in pw]
    pw_bd = [bd(m).astype(BF16) for m in pw]
    n = 2
    while n < INV_BLOCK:
        pw = [_mm(pw[pp], pw_bd[pp]) for pp in pairs]
        pw_bd = [bd(m).astype(BF16) for m in pw]
        tm = [tm[pp] + _mm(tm[pp], pw_bd[pp]) for pp in pairs]
        n *= 2
    for lvl in (1, 2):
        off = jnp.logical_and(blk[lvl], jnp.logical_not(blk[lvl - 1]))
        t_bd = [bd(m).astype(BF16) for m in tm]
        tl = [_mm(tm[pp], bd(jnp.where(off, n_ab[pp], 0.0))) for pp in pairs]
        tm = [tm[pp] + _mm(tl[pp], t_bd[pp]) for pp in pairs]
    u = [_mm(tm[pp], bd(s_prod[pp][:C] + av[pp][:C])) for pp in pairs]
    y = [s_prod[pp][C:] + av[pp][C:] + _mm(a_rb[pp], bd(u[pp])) for pp in pairs]
    for pp in pairs:
        zz = _mm_tn(jnp.concatenate([u[pp], ld(v_ref, sls[pp])], axis=0), bh_kh[pp])
        st_scr[pp] = sp[pp] * jnp.exp(ltot[pp]) + diag_blocks(zz)
    def head_sums(parts):
        stacked = jnp.concatenate(parts, axis=0) if NG > 1 else parts[0]
        m = _mm(stacked, ones)
        return [m[pp * C:(pp + 1) * C] for pp in pairs]

    mu = head_sums([t * (1.0 / HEAD_SIZE) for t in y])
    yc = [y[pp] - mu[pp] for pp in pairs]
    var = head_sums([t * t * (1.0 / HEAD_SIZE) for t in yc])
    rk_sum = head_sums([ld(r_ref, sl) * ld(k_ref, sl) * rk_ref[:, sl] for sl in sls])
    for pp, sl in enumerate(sls):
        yn = yc[pp] * lax.rsqrt(var[pp] + GN_EPS) * lng_ref[:, sl] + lnb_ref[:, sl]
        z_ref[:, sl] = ((yn + rk_sum[pp] * ld(v_ref, sl)) * ld(g_ref, sl)).astype(z_ref.dtype)

    @pl.when(c == nc - 1)
    def _():
        for gi in range(NG):
            for h in range(GH):
                so_ref[0, GH * gi + h] = st_scr[gi, :, h * C:(h + 1) * C]


def _wkv_call(rkvkal, gate, r_k, ln_g, ln_b, state, *, seq_len):
    T, D = gate.shape
    B = T // seq_len
    C = WKV_CHUNK
    NC = seq_len // C
    GH = min(WKV_GROUP_HEADS, D // HEAD_SIZE)
    GW = GH * HEAD_SIZE
    NG = min(WKV_GROUPS, D // GW)
    Q = D // (NG * GW)
    assert seq_len % C == 0 and D % (NG * GW) == 0 and GW % LANES == 0
    tile = pl.BlockSpec((C, NG * GW), lambda b, q, c: (b * NC + c, q))
    row = pl.BlockSpec((1, NG * GW), lambda b, q, c: (0, q))
    st = pl.BlockSpec((1, NG * GH, HEAD_SIZE, HEAD_SIZE), lambda b, q, c: (b, q, 0, 0))
    return pl.pallas_call(
        functools.partial(_wkv_kernel, NG=NG, GH=GH),
        grid=(B, Q, NC),
        in_specs=[tile] * 7 + [row] * 3 + [st],
        out_specs=[tile, st],
        out_shape=[jax.ShapeDtypeStruct((T, D), BF16), jax.ShapeDtypeStruct(state.shape, F32)],
        scratch_shapes=[pltpu.VMEM((NG, HEAD_SIZE, GW), F32)],
        compiler_params=_params("arbitrary", "arbitrary", "arbitrary"),
        name="rwkv_wkv",
    )(*rkvkal, gate, _row(r_k.reshape(-1)), _row(ln_g), _row(ln_b), state)


def _pad_rank(w, axis):
    pad = [(0, 0)] * w.ndim
    pad[axis] = (0, -w.shape[axis] % LORA_PAD)
    return jnp.pad(w, pad)


def _rwkv_in_call(x, g, p, shift, v_first, *, seq_len, layer):
    T, D = x.shape
    carry = shift is None
    vres = v_first is not None
    vl = layer - 1
    TM, L, S = _tiling(T, seq_len, carry, RWKV_IN_TILE)
    tn = min(RWKV_IN_COLS, D)
    nj = D // tn
    row_d = pl.BlockSpec((1, D), lambda i, j: (0, 0))
    row_n = pl.BlockSpec((1, tn), lambda i, j: (0, j))
    tile_n = pl.BlockSpec((TM, tn), lambda i, j: (i, j))

    def col_w(lyr):
        return pl.BlockSpec((None, D, tn), lambda i, j: (lyr, 0, j))

    def lora1(a, lyr):
        return pl.BlockSpec((None,) + a.shape[1:], lambda i, j: (lyr, 0, 0))

    def lora2(a, lyr):
        return pl.BlockSpec((None, a.shape[1], tn), lambda i, j: (lyr, 0, j))

    in_specs = [pl.BlockSpec((TM, D), lambda i, j: (i, 0)), row_d, pl.BlockSpec((6, D), lambda i, j: (0, 0)),
                col_w(layer), col_w(layer), col_w(layer),
                lora1(p['w1'], layer), lora1(p['a1'], layer), lora1(p['g1'], layer),
                lora2(p['w2'], layer), lora2(p['a2'], layer), lora2(p['g2'], layer),
                row_n, row_n, row_n, row_n]
    args = [x, _row(g), p['mix'][layer], p['w_r'], p['w_k'], p['w_v'], p['w1'], p['a1'], p['g1'],
            p['w2'], p['a2'], p['g2'],
            _row(p['w0'][layer]), _row(p['a0'][layer]), _row(p['k_k'][layer]), _row(p['k_a'][layer])]
    if vres:
        in_specs += [lora1(p['v1'], vl), lora2(p['v2'], vl), row_n, tile_n]
        args += [p['v1'], p['v2'], _row(p['v0'][vl]), v_first]
    if not carry:
        in_specs.append(pl.BlockSpec((S, 1, D), lambda i, j: (i, 0, 0)))
        args.append(shift.reshape(-1, 1, D))
    nseg = (T // TM) * S
    out_specs = [tile_n] * 7 + [pl.BlockSpec((S, 1, D), lambda i, j: (i, 0, 0))]
    out_shape = ([jax.ShapeDtypeStruct((T, D), BF16)] * 5 + [jax.ShapeDtypeStruct((T, D), F32)]
                 + [jax.ShapeDtypeStruct((T, D), BF16), jax.ShapeDtypeStruct((nseg, 1, D), F32)])
    scratch = [pltpu.VMEM((3, TM, D), BF16), pltpu.VMEM((TM, p['w1'].shape[2]), BF16),
               pltpu.VMEM((TM, p['a1'].shape[2]), BF16), pltpu.VMEM((TM, p['g1'].shape[2]), BF16)]
    if vres:
        scratch.append(pltpu.VMEM((TM, p['v1'].shape[2]), BF16))
    if carry:
        scratch.append(pltpu.VMEM((1, D), F32))
    outs = pl.pallas_call(
        functools.partial(_rwkv_in_kernel, S=S, L=L, carry=carry, vres=vres),
        grid=(T // TM, nj),
        in_specs=in_specs,
        out_specs=out_specs,
        out_shape=out_shape,
        scratch_shapes=scratch,
        compiler_params=_params("arbitrary", "arbitrary"),
        name="rwkv_in_carry" if carry else "rwkv_in_state",
    )(*args)
    r, k, v, kk, a, lw, gate, hl = outs
    h_last = hl[-1:, 0, :] if carry else hl[:, 0, :]
    return (r, k, v, kk, a, lw), gate, h_last


def _proj_call(z, x, w, g, *, layer, ln_g=None, ln_b=None, b=None):
    T, D = x.shape
    TM = min(PROJ_TILE, T)
    assert T % TM == 0
    row = pl.BlockSpec((1, D), lambda i: (0, 0))
    tile = pl.BlockSpec((TM, D), lambda i: (i, 0))
    in_specs = [tile, tile, pl.BlockSpec((None, D, D), lambda i: (layer, 0, 0)), row]
    args = [z, x, w, _row(g)]
    if ln_g is not None:
        in_specs += [row, row]
        args += [_row(ln_g), _row(ln_b)]
    if b is not None:
        in_specs.append(row)
        args.append(_row(b))
    return pl.pallas_call(
        functools.partial(_proj_kernel, ln=ln_g is not None, bias=b is not None),
        grid=(T // TM,),
        in_specs=in_specs,
        out_specs=tile,
        out_shape=jax.ShapeDtypeStruct((T, D), F32),
        compiler_params=_params("arbitrary"),
        name="proj_ln" if ln_g is not None else "proj",
    )(*args)


def _trunk(x3, conv_st, shift_st, wkv_st, ffn_st, p):
    B, T, D = x3.shape
    x = x3.reshape(B * T, D)
    depth = p['norm_g'].shape[0]
    new_conv, new_shift, new_wkv, new_ffn = [], [], [], []
    v_first = None
    for i in range(depth):
        g = p['norm_g'][i]
        j = i // 2
        if i % 2 == 0:
            u = _glu_call(x, g[0], p['conv_pw1_w'], p['conv_pw1_b'][j], layer=j)
            c = _dwconv_call(u, p['conv_dw_w'][j], p['conv_dw_b'][j],
                             None if conv_st is None else conv_st[j], seq_len=T)
            new_conv.append(u.reshape(B, T, D)[:, T - (CONV_WIDTH - 1):])
            x = _proj_call(c, x, p['conv_pw2_w'], g[1], layer=j, ln_g=p['conv_ln_g'][j], ln_b=p['conv_ln_b'][j],
                           b=p['conv_pw2_b'][j])
        else:
            q = {k[5:]: v for k, v in p.items() if k.startswith('rwkv_')}
            rkvkal, gate, h_last = _rwkv_in_call(x, g[0], q, None if shift_st is None else shift_st[j],
                                                 v_first if j > 0 else None, seq_len=T, layer=j)
            if j == 0:
                v_first = rkvkal[2]
            s0 = jnp.zeros((B, D // HEAD_SIZE, HEAD_SIZE, HEAD_SIZE), F32) if wkv_st is None else wkv_st[j]
            z, s_new = _wkv_call(rkvkal, gate, q['r_k'][j], q['ln_g'][j], q['ln_b'][j], s0, seq_len=T)
            new_shift.append(h_last)
            new_wkv.append(s_new)
            x = _proj_call(z, x, q['w_o'], g[1], layer=j)
        x, nb = _ffn_call(x, g[2], g[3], p['ffn_w_in'], p['ffn_dw_w'][i], p['ffn_dw_b'][i], p['ffn_w_out'],
                          None if ffn_st is None else ffn_st[i], seq_len=T, layer=i)
        new_ffn.append(nb)
    return (x.reshape(B, T, D), jnp.stack(new_conv), jnp.stack(new_shift), jnp.stack(new_wkv), jnp.stack(new_ffn))


def kernel(x_prompt, x_sample, state_conv_mix, state_rwkv_shift, state_rwkv_wkv, state_ffn_conv, norm_g,
           conv_pw1_w, conv_pw1_b, conv_dw_w, conv_dw_b, conv_ln_g, conv_ln_b, conv_pw2_w, conv_pw2_b,
           rwkv_mix, rwkv_w_r, rwkv_w_k, rwkv_w_v, rwkv_w_o, rwkv_w0, rwkv_w1, rwkv_w2, rwkv_a0, rwkv_a1,
           rwkv_a2, rwkv_v0, rwkv_v1, rwkv_v2, rwkv_g1, rwkv_g2, rwkv_k_k, rwkv_k_a, rwkv_r_k, rwkv_ln_g,
           rwkv_ln_b, ffn_w_in, ffn_dw_w, ffn_dw_b, ffn_w_out):
    bf = lambda w: w.astype(BF16)
    pad_c = lambda w: _pad_rank(w, 2).astype(BF16)
    pad_r = lambda w: _pad_rank(w, 1).astype(BF16)
    p = dict(norm_g=norm_g, conv_pw1_w=bf(conv_pw1_w), conv_pw1_b=conv_pw1_b, conv_dw_w=conv_dw_w,
             conv_dw_b=conv_dw_b, conv_ln_g=conv_ln_g, conv_ln_b=conv_ln_b, conv_pw2_w=bf(conv_pw2_w),
             conv_pw2_b=conv_pw2_b, rwkv_mix=rwkv_mix, rwkv_w_r=bf(rwkv_w_r), rwkv_w_k=bf(rwkv_w_k),
             rwkv_w_v=bf(rwkv_w_v), rwkv_w_o=bf(rwkv_w_o), rwkv_w0=rwkv_w0, rwkv_w1=pad_c(rwkv_w1),
             rwkv_w2=pad_r(rwkv_w2), rwkv_a0=rwkv_a0, rwkv_a1=pad_c(rwkv_a1), rwkv_a2=pad_r(rwkv_a2),
             rwkv_v0=rwkv_v0, rwkv_v1=pad_c(rwkv_v1), rwkv_v2=pad_r(rwkv_v2), rwkv_g1=pad_c(rwkv_g1),
             rwkv_g2=pad_r(rwkv_g2), rwkv_k_k=rwkv_k_k, rwkv_k_a=rwkv_k_a, rwkv_r_k=rwkv_r_k,
             rwkv_ln_g=rwkv_ln_g, rwkv_ln_b=rwkv_ln_b, ffn_w_in=bf(ffn_w_in), ffn_dw_w=ffn_dw_w,
             ffn_dw_b=ffn_dw_b, ffn_w_out=bf(ffn_w_out))
    y_p, conv_p, shift_p, wkv_p, ffn_p = _trunk(x_prompt, None, None, None, None, p)
    y_s, conv_s, shift_s, wkv_s, ffn_s = _trunk(x_sample, state_conv_mix, state_rwkv_shift, state_rwkv_wkv,
                                                state_ffn_conv, p)
    return (y_p, y_s, conv_p, conv_s, shift_p, shift_s, wkv_p, wkv_s, ffn_p, ffn_s)
```

```python
import functools

import jax
import jax.numpy as jnp
from jax import lax
from jax.experimental import pallas as pl
from jax.experimental.pallas import tpu as pltpu

F32 = jnp.float32
BF16 = jnp.bfloat16

RMS_EPS = 1e-6
LN_EPS = 1e-5
GN_EPS = 64e-5
HEAD_SIZE = 64
CONV_WIDTH = 31
FFN_CONV_WIDTH = 3

LANES = 128
SUBLANES = 8
VMEM_LIMIT_BYTES = 56 * 1024 * 1024

TOKEN_TILE = 512
WKV_CHUNK = 64


def _params(*semantics):
    return pltpu.CompilerParams(dimension_semantics=semantics, vmem_limit_bytes=VMEM_LIMIT_BYTES)


def _rmsnorm(x, g):
    return x * lax.rsqrt(jnp.mean(x * x, axis=-1, keepdims=True) + RMS_EPS) * g


def _row(v):
    return v.reshape(1, -1)


def _ffn_kernel(*refs, S, L, carry):
    if carry:
        (x_ref, g2_ref, g3_ref, wu_ref, wg_ref, dww_ref, dwb_ref, wo_ref,
         o_ref, nb_ref, h_scr, gpx_scr, carry_scr) = refs
    else:
        (x_ref, g2_ref, g3_ref, wu_ref, wg_ref, dww_ref, dwb_ref, wo_ref, halo_ref,
         o_ref, nb_ref, h_scr, gpx_scr) = refs
    i = pl.program_id(0)
    j = pl.program_id(1)
    nj = pl.num_programs(1)
    K = FFN_CONV_WIDTH - 1
    tf = wu_ref.shape[-1]

    @pl.when(j == 0)
    def _():
        h_scr[...] = _rmsnorm(x_ref[...], g2_ref[...]).astype(BF16)
        o_ref[...] = jnp.zeros(o_ref.shape, F32)

    if carry:
        @pl.when(i == 0)
        def _():
            carry_scr[j] = jnp.zeros((SUBLANES, tf), F32)

    h = h_scr[...]
    w = dww_ref[...]
    nsub = 2 if tf % (2 * LANES) == 0 else 1
    ts = tf // nsub
    cols = [slice(s * ts, (s + 1) * ts) for s in range(nsub)]
    up = [(jnp.dot(h, wu_ref[:, cs], preferred_element_type=F32),
           jnp.dot(h, wg_ref[:, cs], preferred_element_type=F32)) for cs in cols]
    contrib = None
    for cs, (u, gp) in zip(cols, up):
        gp3 = gp.reshape(S, L, ts)
        gpx_scr[:, SUBLANES:SUBLANES + L, cs] = gp3
        if carry:
            gpx_scr[0, 0:SUBLANES, cs] = carry_scr[j, :, cs]
        else:
            gpx_scr[:, SUBLANES - K:SUBLANES, cs] = halo_ref[:, :, cs]
        gc = (gpx_scr[:, SUBLANES - 2:SUBLANES - 2 + L, cs] * w[0:1, cs]
              + gpx_scr[:, SUBLANES - 1:SUBLANES - 1 + L, cs] * w[1:2, cs]
              + gp3 * w[2:3, cs] + dwb_ref[:, cs])
        act = (jax.nn.gelu(gc) * u.reshape(S, L, ts)).reshape(S * L, ts).astype(BF16)
        part = jnp.dot(act, wo_ref[cs, :], preferred_element_type=F32)
        contrib = part if contrib is None else contrib + part
    o_ref[...] += contrib

    nb_ref[...] = gpx_scr[:, SUBLANES + L - K:SUBLANES + L, :]
    if carry:
        carry_scr[j] = gpx_scr[0, L:L + SUBLANES, :]

    @pl.when(j == nj - 1)
    def _():
        o_ref[...] = x_ref[...] + _rmsnorm(o_ref[...], g3_ref[...])


def _ffn_chunk(d_ff):
    for tf in (512, 256, 128):
        if d_ff % tf == 0:
            return tf
    raise ValueError(f"d_ff={d_ff} is not a multiple of {LANES}")


def _ffn_call(x, g2, g3, w_in, dw_w, dw_b, w_out, halo, *, seq_len, layer):
    T, D = x.shape
    d_ff = w_out.shape[1]
    tf = _ffn_chunk(d_ff)
    nj = d_ff // tf
    carry = halo is None
    TM = min(TOKEN_TILE, T)
    L = TM if carry else seq_len
    S = TM // L
    assert T % TM == 0 and TM % L == 0 and (carry or L == seq_len)
    nt = T // TM
    in_specs = [
        pl.BlockSpec((TM, D), lambda i, j: (i, 0)),
        pl.BlockSpec((1, D), lambda i, j: (0, 0)),
        pl.BlockSpec((1, D), lambda i, j: (0, 0)),
        pl.BlockSpec((None, D, tf), lambda i, j: (layer, 0, j)),
        pl.BlockSpec((None, D, tf), lambda i, j: (layer, 0, nj + j)),
        pl.BlockSpec((FFN_CONV_WIDTH, tf), lambda i, j: (0, j)),
        pl.BlockSpec((1, tf), lambda i, j: (0, j)),
        pl.BlockSpec((None, tf, D), lambda i, j: (layer, j, 0)),
    ]
    args = [x, _row(g2), _row(g3), w_in, w_in, dw_w, _row(dw_b), w_out]
    scratch = [pltpu.VMEM((TM, D), BF16), pltpu.VMEM((S, SUBLANES + L, tf), F32)]
    nb_spec = pl.BlockSpec((S, FFN_CONV_WIDTH - 1, tf), lambda i, j: (i, 0, j))
    if carry:
        scratch.append(pltpu.VMEM((nj, SUBLANES, tf), F32))
    else:
        in_specs.append(nb_spec)
        args.append(halo)
    out, nb = pl.pallas_call(
        functools.partial(_ffn_kernel, S=S, L=L, carry=carry),
        grid=(nt, nj),
        in_specs=in_specs,
        out_specs=[pl.BlockSpec((TM, D), lambda i, j: (i, 0)), nb_spec],
        out_shape=[jax.ShapeDtypeStruct((T, D), F32),
                   jax.ShapeDtypeStruct((nt * S, FFN_CONV_WIDTH - 1, d_ff), F32)],
        scratch_shapes=scratch,
        compiler_params=_params("arbitrary", "arbitrary"),
        name="ffn_carry" if carry else "ffn_state",
    )(*args)
    return out, (nb[-1:] if carry else nb)


def _tiling(T, seq_len, carry, tile=TOKEN_TILE):
    TM = min(tile, T)
    L = TM if carry else seq_len
    assert T % TM == 0 and TM % L == 0 and L % SUBLANES == 0
    return TM, L, TM // L


def _glu_kernel(x_ref, g_ref, wa_ref, wb_ref, ba_ref, bb_ref, u_ref, h_scr):
    @pl.when(pl.program_id(1) == 0)
    def _():
        h_scr[...] = _rmsnorm(x_ref[...], g_ref[...]).astype(BF16)

    h = h_scr[...]
    tn = wa_ref.shape[-1]
    nsub = 2 if tn % (2 * LANES) == 0 else 1
    cols = [slice(s * (tn // nsub), (s + 1) * (tn // nsub)) for s in range(nsub)]
    ab = [(jnp.dot(h, wa_ref[:, cs], preferred_element_type=F32),
           jnp.dot(h, wb_ref[:, cs], preferred_element_type=F32)) for cs in cols]
    for cs, (a, b) in zip(cols, ab):
        u_ref[:, cs] = (a + ba_ref[:, cs]) * jax.nn.sigmoid(b + bb_ref[:, cs])


def _glu_call(x, g, w1, b1, *, layer):
    T, D = x.shape
    TM = min(TOKEN_TILE, T)
    tn = min(512, D)
    nj = D // tn
    assert T % TM == 0 and D % tn == 0
    return pl.pallas_call(
        _glu_kernel,
        grid=(T // TM, nj),
        in_specs=[
            pl.BlockSpec((TM, D), lambda i, j: (i, 0)),
            pl.BlockSpec((1, D), lambda i, j: (0, 0)),
            pl.BlockSpec((None, D, tn), lambda i, j: (layer, 0, j)),
            pl.BlockSpec((None, D, tn), lambda i, j: (layer, 0, nj + j)),
            pl.BlockSpec((1, tn), lambda i, j: (0, j)),
            pl.BlockSpec((1, tn), lambda i, j: (0, nj + j)),
        ],
        out_specs=pl.BlockSpec((TM, tn), lambda i, j: (i, j)),
        out_shape=jax.ShapeDtypeStruct((T, D), F32),
        scratch_shapes=[pltpu.VMEM((TM, D), BF16)],
        compiler_params=_params("arbitrary", "arbitrary"),
        name="conv_glu",
    )(x, _row(g), w1, w1, _row(b1), _row(b1))


HIST_PAD = 32
DW_ROWS = 64


def _dwconv_kernel(*refs, S, L, carry):
    if carry:
        u_ref, w_ref, b_ref, c_ref, ux_scr, sh_scr, carry_scr = refs
    else:
        u_ref, w_ref, b_ref, st_ref, c_ref, ux_scr, sh_scr = refs
    i = pl.program_id(0)
    j = pl.program_id(1)
    tc = u_ref.shape[-1]
    K = CONV_WIDTH - 1
    ux_scr[:, HIST_PAD:HIST_PAD + L, :] = u_ref[...].reshape(S, L, tc)
    if carry:
        @pl.when(i == 0)
        def _():
            carry_scr[j] = jnp.zeros((HIST_PAD, tc), F32)
        ux_scr[0, 0:HIST_PAD, :] = carry_scr[j]
    else:
        ux_scr[:, HIST_PAD - K:HIST_PAD, :] = st_ref[...]
    nr = HIST_PAD + L - SUBLANES
    for r in range(1, SUBLANES):
        sh_scr[r - 1] = ux_scr[:, r:r + nr, :]
    w = w_ref[...]
    rb = min(DW_ROWS, L)
    for s in range(S):
        for r0 in range(0, L, rb):
            acc = jnp.broadcast_to(b_ref[...], (rb, tc))
            for k in range(CONV_WIDTH):
                m = HIST_PAD - K + k
                r, lo = m % SUBLANES, r0 + m - m % SUBLANES
                tap = ux_scr[s, lo:lo + rb, :] if r == 0 else sh_scr[r - 1, s, lo:lo + rb, :]
                acc = acc + tap * w[k:k + 1, :]
            c_ref[s * L + r0:s * L + r0 + rb, :] = acc
    if carry:
        carry_scr[j] = ux_scr[0, L:L + HIST_PAD, :]


def _dwconv_call(u, w, b, state, *, seq_len):
    T, D = u.shape
    carry = state is None
    TM, L, S = _tiling(T, seq_len, carry)
    tc = min(256, D)
    nc = D // tc
    in_specs = [
        pl.BlockSpec((TM, tc), lambda i, j: (i, j)),
        pl.BlockSpec((CONV_WIDTH, tc), lambda i, j: (0, j)),
        pl.BlockSpec((1, tc), lambda i, j: (0, j)),
    ]
    args = [u, w, _row(b)]
    scratch = [pltpu.VMEM((S, HIST_PAD + L, tc), F32),
               pltpu.VMEM((SUBLANES - 1, S, HIST_PAD + L - SUBLANES, tc), F32)]
    if carry:
        scratch.append(pltpu.VMEM((nc, HIST_PAD, tc), F32))
    else:
        in_specs.append(pl.BlockSpec((S, CONV_WIDTH - 1, tc), lambda i, j: (i, 0, j)))
        args.append(state)
    return pl.pallas_call(
        functools.partial(_dwconv_kernel, S=S, L=L, carry=carry),
        grid=(T // TM, nc),
        in_specs=in_specs,
        out_specs=pl.BlockSpec((TM, tc), lambda i, j: (i, j)),
        out_shape=jax.ShapeDtypeStruct((T, D), F32),
        scratch_shapes=scratch,
        compiler_params=_params("arbitrary", "arbitrary"),
        name="conv_dw_carry" if carry else "conv_dw_state",
    )(*args)


def _proj_kernel(*refs, ln, bias):
    refs = list(refs)
    z_ref, x_ref, w_ref, g_ref = refs[:4]
    rest = refs[4:]
    z = z_ref[...].astype(F32)
    if ln:
        lg_ref, lb_ref = rest[:2]
        rest = rest[2:]
        mu = jnp.mean(z, axis=-1, keepdims=True)
        zc = z - mu
        var = jnp.mean(zc * zc, axis=-1, keepdims=True)
        z = zc * lax.rsqrt(var + LN_EPS) * lg_ref[...] + lb_ref[...]
        z = z * jax.nn.sigmoid(z)
    out = jnp.dot(z.astype(BF16), w_ref[...], preferred_element_type=F32)
    if bias:
        out = out + rest[0][...]
        rest = rest[1:]
    o_ref, = rest
    o_ref[...] = x_ref[...] + _rmsnorm(out, g_ref[...])


PROJ_TILE = 512


def _head_ones(width=LANES):
    r = lax.broadcasted_iota(jnp.int32, (width, width), 0) // HEAD_SIZE
    c = lax.broadcasted_iota(jnp.int32, (width, width), 1) // HEAD_SIZE
    return (r == c).astype(BF16)


def _head_sum(x, ones):
    outs = []
    for p in range(x.shape[1] // LANES):
        xp = x[:, p * LANES:(p + 1) * LANES]
        hi = xp.astype(BF16)
        lo = (xp - hi.astype(F32)).astype(BF16)
        outs.append(jnp.dot(hi, ones, preferred_element_type=F32) + jnp.dot(lo, ones, preferred_element_type=F32))
    return outs[0] if len(outs) == 1 else jnp.concatenate(outs, axis=1)


LORA_PAD = 128
RWKV_IN_TILE = 512
RWKV_IN_COLS = 512


def _softplus(z):
    return jnp.maximum(z, 0.0) + jnp.log(1.0 + jnp.exp(-jnp.abs(z)))


MIX_CHUNK = 512


def _rwkv_in_kernel(*refs, S, L, carry, vres):
    refs = list(refs)
    x_ref, g_ref, mix_ref, wr_ref, wk_ref, wv_ref, w1_ref, a1_ref, g1_ref = refs[:9]
    w2_ref, a2_ref, g2_ref, w0_ref, a0_ref, kk_ref, ka_ref = refs[9:16]
    rest = refs[16:]
    if vres:
        v1_ref, v2_ref, v0_ref, vf_ref = rest[:4]
        rest = rest[4:]
    if not carry:
        sh_ref = rest[0]
        rest = rest[1:]
    r_ref, k_ref, v_ref, kkn_ref, a_ref, lw_ref, go_ref, hl_ref = rest[:8]
    rest = rest[8:]
    mx_scr, tw_scr, ta_scr, tg_scr = rest[:4]
    rest = rest[4:]
    if vres:
        tv_scr = rest[0]
        rest = rest[1:]
    if carry:
        carry_scr, = rest
    i = pl.program_id(0)
    j = pl.program_id(1)
    TM, D = x_ref.shape

    @pl.when(j == 0)
    def _():
        x = x_ref[...]
        rs = lax.rsqrt(jnp.mean(x * x, axis=-1, keepdims=True) + RMS_EPS)
        if carry:
            @pl.when(i == 0)
            def _():
                carry_scr[...] = jnp.zeros((1, D), F32)
        mc = min(MIX_CHUNK, D)
        seg_start = lax.broadcasted_iota(jnp.int32, (TM, mc), 0) % L == 0
        lora = {}

        def acc(name, xm, w_ref, cs):
            part = jnp.dot(xm, w_ref[cs, :], preferred_element_type=F32)
            lora[name] = part if name not in lora else lora[name] + part

        for c0 in range(0, D, mc):
            cs = slice(c0, c0 + mc)
            h = x_ref[:, cs] * rs * g_ref[:, cs]
            if carry:
                first = jnp.broadcast_to(carry_scr[:, cs], (TM, mc))
                carry_scr[:, cs] = h[TM - 1:TM, :]
            else:
                first = jnp.broadcast_to(sh_ref[:, :, cs], (S, L, mc)).reshape(TM, mc)
            for s in range(S):
                hl_ref[s, :, cs] = h[s * L + L - 1:s * L + L, :]
            xx = jnp.where(seg_start, first, pltpu.roll(h, 1, axis=0)) - h
            mixed = [(h + xx * mix_ref[m:m + 1, cs]).astype(BF16) for m in range(6)]
            mx_scr[0, :, cs] = mixed[0]
            mx_scr[1, :, cs] = mixed[2]
            mx_scr[2, :, cs] = mixed[3]
            acc('w', mixed[1], w1_ref, cs)
            acc('a', mixed[4], a1_ref, cs)
            acc('g', mixed[5], g1_ref, cs)
            if vres:
                acc('v', mixed[3], v1_ref, cs)
        tw_scr[...] = jnp.tanh(lora['w']).astype(BF16)
        ta_scr[...] = lora['a'].astype(BF16)
        tg_scr[...] = jax.nn.sigmoid(lora['g']).astype(BF16)
        if vres:
            tv_scr[...] = lora['v'].astype(BF16)

    r = jnp.dot(mx_scr[0], wr_ref[...], preferred_element_type=F32)
    k = jnp.dot(mx_scr[1], wk_ref[...], preferred_element_type=F32)
    v = jnp.dot(mx_scr[2], wv_ref[...], preferred_element_type=F32)
    wl = jnp.dot(tw_scr[...], w2_ref[...], preferred_element_type=F32)
    al = jnp.dot(ta_scr[...], a2_ref[...], preferred_element_type=F32)
    go_ref[...] = jnp.dot(tg_scr[...], g2_ref[...], preferred_element_type=F32).astype(go_ref.dtype)
    w = -_softplus(-(w0_ref[...] + wl)) - 0.5
    lw_ref[...] = -jnp.exp(w)
    a = jax.nn.sigmoid(a0_ref[...] + al)
    if vres:
        vl = jnp.dot(tv_scr[...], v2_ref[...], preferred_element_type=F32)
        v = v + (vf_ref[...].astype(F32) - v) * jax.nn.sigmoid(v0_ref[...] + vl)
    kk = k * kk_ref[...]
    norm = jnp.sqrt(_head_sum(kk * kk, _head_ones()))
    kkn_ref[...] = (kk / jnp.maximum(norm, 1e-12)).astype(kkn_ref.dtype)
    r_ref[...] = r.astype(r_ref.dtype)
    k_ref[...] = (k * (1.0 + (a - 1.0) * ka_ref[...])).astype(k_ref.dtype)
    v_ref[...] = v.astype(v_ref.dtype)
    a_ref[...] = a.astype(a_ref.dtype)


WKV_GROUP_HEADS = LANES // HEAD_SIZE
WKV_GROUPS = 16
INV_BLOCK = 16


def _mm(a, b):
    return jnp.dot(a.astype(BF16), b.astype(BF16), preferred_element_type=F32)


def _mm_nt(a, b):
    return lax.dot_general(a.astype(BF16), b.astype(BF16), (((1,), (1,)), ((), ())), preferred_element_type=F32)


def _mm_tn(a, b):
    return lax.dot_general(a.astype(BF16), b.astype(BF16), (((0,), (0,)), ((), ())), preferred_element_type=F32)


def _wkv_kernel(r_ref, k_ref, v_ref, kk_ref, a_ref, lw_ref, g_ref, rk_ref, lng_ref, lnb_ref, s_ref,
                z_ref, so_ref, st_scr, *, NG, GH):
    c = pl.program_id(2)
    nc = pl.num_programs(2)
    C = r_ref.shape[0]
    GW = GH * C
    assert C == HEAD_SIZE

    @pl.when(c == 0)
    def _():
        for gi in range(NG):
            st_scr[gi] = jnp.concatenate([s_ref[0, GH * gi + h] for h in range(GH)], axis=1)

    def ld(ref, sl):
        return ref[:, sl].astype(F32)

    row = lax.broadcasted_iota(jnp.int32, (C, GW), 0)
    lane = lax.broadcasted_iota(jnp.int32, (C, GW), 1)
    sidx = lane % C
    head = lane // C
    strict = sidx < row
    incl = sidx <= row
    eye = (sidx == row).astype(F32)
    ones = _head_ones(GW)

    def bd(x):
        z = jnp.zeros_like(x)
        return jnp.concatenate([jnp.where(head == h, x, z) for h in range(GH)], axis=0)

    def diag_blocks(zz):
        out = zz[:C]
        for h in range(1, GH):
            out = jnp.where(head == h, zz[h * C:(h + 1) * C], out)
        return out

    lw = lw_ref[...]
    rows_full = lax.broadcasted_iota(jnp.int32, lw.shape, 0)
    lc = lw
    sh = 1
    while sh < C:
        lc = lc + jnp.where(rows_full >= sh, pltpu.roll(lc, sh, axis=0), 0.0)
        sh *= 2

    pairs = range(NG)
    sls = [slice(gi * GW, (gi + 1) * GW) for gi in pairs]
    lhs, bh_kh, v_bd, ltot, sp = [], [], [], [], []
    gm = []
    for sl in sls:
        lwp, lp = lw[:, sl], lc[:, sl]
        lt = lp[C - 1:C, :]
        kp, kkp = ld(k_ref, sl), ld(kk_ref, sl)
        bp = kkp * ld(a_ref, sl)
        e_neg = jnp.exp(-lp)
        e_rem = jnp.exp(lt - lp)
        at = -(kkp * jnp.exp(lp - lwp))
        rt = ld(r_ref, sl) * jnp.exp(lp)
        lhs.append(jnp.concatenate([at, rt], axis=0).astype(BF16))
        gm.append(_mm_nt(lhs[-1], jnp.concatenate([bd(bp * e_neg), bd(kp * e_neg)], axis=0)))
        bh_kh.append(jnp.concatenate([bp * e_rem, kp * e_rem], axis=0).astype(BF16))
        v_bd.append(bd(ld(v_ref, sl)).astype(BF16))
        ltot.append(lt)
    n_ab = [jnp.where(strict, g[:C, :GW], 0.0) for g in gm]
    a_rb = [jnp.where(incl, g[C:, :GW], 0.0) for g in gm]
    a_k = [jnp.concatenate([jnp.where(strict, g[:C, GW:], 0.0), jnp.where(incl, g[C:, GW:], 0.0)], axis=0)
           for g in gm]
    for pp in pairs:
        sp.append(st_scr[pp])
    s_prod = [_mm_nt(lhs[pp], bd(sp[pp])) for pp in pairs]
    av = [_mm(a_k[pp], v_bd[pp]) for pp in pairs]
    blk = [(row // (INV_BLOCK << lvl)) == (sidx // (INV_BLOCK << lvl)) for lvl in range(3)]
    assert INV_BLOCK << 2 == C
    pw = [jnp.where(blk[0], m, 0.0) for m in n_ab]
    tm = [eye + m for m in pw]
    pw_bd = [bd(m).astype(BF16) for m in pw]
    n = 2
    while n < INV_BLOCK:
        pw = [_mm(pw[pp], pw_bd[pp]) for pp in pairs]
        pw_bd = [bd(m).astype(BF16) for m in pw]
        tm = [tm[pp] + _mm(tm[pp], pw_bd[pp]) for pp in pairs]
        n *= 2
    for lvl in (1, 2):
        off = jnp.logical_and(blk[lvl], jnp.logical_not(blk[lvl - 1]))
        t_bd = [bd(m).astype(BF16) for m in tm]
        tl = [_mm(tm[pp], bd(jnp.where(off, n_ab[pp], 0.0))) for pp in pairs]
        tm = [tm[pp] + _mm(tl[pp], t_bd[pp]) for pp in pairs]
    u = [_mm(tm[pp], bd(s_prod[pp][:C] + av[pp][:C])) for pp in pairs]
    y = [s_prod[pp][C:] + av[pp][C:] + _mm(a_rb[pp], bd(u[pp])) for pp in pairs]
    for pp in pairs:
        zz = _mm_tn(jnp.concatenate([u[pp], ld(v_ref, sls[pp])], axis=0), bh_kh[pp])
        st_scr[pp] = sp[pp] * jnp.exp(ltot[pp]) + diag_blocks(zz)
    def head_sums(parts):
        stacked = jnp.concatenate(parts, axis=0) if NG > 1 else parts[0]
        m = _mm(stacked, ones)
        return [m[pp * C:(pp + 1) * C] for pp in pairs]

    mu = head_sums([t * (1.0 / HEAD_SIZE) for t in y])
    yc = [y[pp] - mu[pp] for pp in pairs]
    var = head_sums([t * t * (1.0 / HEAD_SIZE) for t in yc])
    rk_sum = head_sums([ld(r_ref, sl) * ld(k_ref, sl) * rk_ref[:, sl] for sl in sls])
    for pp, sl in enumerate(sls):
        yn = yc[pp] * lax.rsqrt(var[pp] + GN_EPS) * lng_ref[:, sl] + lnb_ref[:, sl]
        z_ref[:, sl] = ((yn + rk_sum[pp] * ld(v_ref, sl)) * ld(g_ref, sl)).astype(z_ref.dtype)

    @pl.when(c == nc - 1)
    def _():
        for gi in range(NG):
            for h in range(GH):
                so_ref[0, GH * gi + h] = st_scr[gi, :, h * C:(h + 1) * C]


def _wkv_call(rkvkal, gate, r_k, ln_g, ln_b, state, *, seq_len):
    T, D = gate.shape
    B = T // seq_len
    C = WKV_CHUNK
    NC = seq_len // C
    GH = min(WKV_GROUP_HEADS, D // HEAD_SIZE)
    GW = GH * HEAD_SIZE
    NG = min(WKV_GROUPS, D // GW)
    Q = D // (NG * GW)
    assert seq_len % C == 0 and D % (NG * GW) == 0 and GW % LANES == 0
    tile = pl.BlockSpec((C, NG * GW), lambda b, q, c: (b * NC + c, q))
    row = pl.BlockSpec((1, NG * GW), lambda b, q, c: (0, q))
    st = pl.BlockSpec((1, NG * GH, HEAD_SIZE, HEAD_SIZE), lambda b, q, c: (b, q, 0, 0))
    return pl.pallas_call(
        functools.partial(_wkv_kernel, NG=NG, GH=GH),
        grid=(B, Q, NC),
        in_specs=[tile] * 7 + [row] * 3 + [st],
        out_specs=[tile, st],
        out_shape=[jax.ShapeDtypeStruct((T, D), BF16), jax.ShapeDtypeStruct(state.shape, F32)],
        scratch_shapes=[pltpu.VMEM((NG, HEAD_SIZE, GW), F32)],
        compiler_params=_params("arbitrary", "arbitrary", "arbitrary"),
        name="rwkv_wkv",
    )(*rkvkal, gate, _row(r_k.reshape(-1)), _row(ln_g), _row(ln_b), state)


def _pad_rank(w, axis):
    pad = [(0, 0)] * w.ndim
    pad[axis] = (0, -w.shape[axis] % LORA_PAD)
    return jnp.pad(w, pad)


def _rwkv_in_call(x, g, p, shift, v_first, *, seq_len, layer):
    T, D = x.shape
    carry = shift is None
    vres = v_first is not None
    vl = layer - 1
    TM, L, S = _tiling(T, seq_len, carry, RWKV_IN_TILE)
    tn = min(RWKV_IN_COLS, D)
    nj = D // tn
    row_d = pl.BlockSpec((1, D), lambda i, j: (0, 0))
    row_n = pl.BlockSpec((1, tn), lambda i, j: (0, j))
    tile_n = pl.BlockSpec((TM, tn), lambda i, j: (i, j))

    def col_w(lyr):
        return pl.BlockSpec((None, D, tn), lambda i, j: (lyr, 0, j))

    def lora1(a, lyr):
        return pl.BlockSpec((None,) + a.shape[1:], lambda i, j: (lyr, 0, 0))

    def lora2(a, lyr):
        return pl.BlockSpec((None, a.shape[1], tn), lambda i, j: (lyr, 0, j))

    in_specs = [pl.BlockSpec((TM, D), lambda i, j: (i, 0)), row_d, pl.BlockSpec((6, D), lambda i, j: (0, 0)),
                col_w(layer), col_w(layer), col_w(layer),
                lora1(p['w1'], layer), lora1(p['a1'], layer), lora1(p['g1'], layer),
                lora2(p['w2'], layer), lora2(p['a2'], layer), lora2(p['g2'], layer),
                row_n, row_n, row_n, row_n]
    args = [x, _row(g), p['mix'][layer], p['w_r'], p['w_k'], p['w_v'], p['w1'], p['a1'], p['g1'],
            p['w2'], p['a2'], p['g2'],
            _row(p['w0'][layer]), _row(p['a0'][layer]), _row(p['k_k'][layer]), _row(p['k_a'][layer])]
    if vres:
        in_specs += [lora1(p['v1'], vl), lora2(p['v2'], vl), row_n, tile_n]
        args += [p['v1'], p['v2'], _row(p['v0'][vl]), v_first]
    if not carry:
        in_specs.append(pl.BlockSpec((S, 1, D), lambda i, j: (i, 0, 0)))
        args.append(shift.reshape(-1, 1, D))
    nseg = (T // TM) * S
    out_specs = [tile_n] * 7 + [pl.BlockSpec((S, 1, D), lambda i, j: (i, 0, 0))]
    out_shape = ([jax.ShapeDtypeStruct((T, D), BF16)] * 5 + [jax.ShapeDtypeStruct((T, D), F32)]
                 + [jax.ShapeDtypeStruct((T, D), BF16), jax.ShapeDtypeStruct((nseg, 1, D), F32)])
    scratch = [pltpu.VMEM((3, TM, D), BF16), pltpu.VMEM((TM, p['w1'].shape[2]), BF16),
               pltpu.VMEM((TM, p['a1'].shape[2]), BF16), pltpu.VMEM((TM, p['g1'].shape[2]), BF16)]
    if vres:
        scratch.append(pltpu.VMEM((TM, p['v1'].shape[2]), BF16))
    if carry:
        scratch.append(pltpu.VMEM((1, D), F32))
    outs = pl.pallas_call(
        functools.partial(_rwkv_in_kernel, S=S, L=L, carry=carry, vres=vres),
        grid=(T // TM, nj),
        in_specs=in_specs,
        out_specs=out_specs,
        out_shape=out_shape,
        scratch_shapes=scratch,
        compiler_params=_params("arbitrary", "arbitrary"),
        name="rwkv_in_carry" if carry else "rwkv_in_state",
    )(*args)
    r, k, v, kk, a, lw, gate, hl = outs
    h_last = hl[-1:, 0, :] if carry else hl[:, 0, :]
    return (r, k, v, kk, a, lw), gate, h_last


def _proj_call(z, x, w, g, *, layer, ln_g=None, ln_b=None, b=None):
    T, D = x.shape
    TM = min(PROJ_TILE, T)
    assert T % TM == 0
    row = pl.BlockSpec((1, D), lambda i: (0, 0))
    tile = pl.BlockSpec((TM, D), lambda i: (i, 0))
    in_specs = [tile, tile, pl.BlockSpec((None, D, D), lambda i: (layer, 0, 0)), row]
    args = [z, x, w, _row(g)]
    if ln_g is not None:
        in_specs += [row, row]
        args += [_row(ln_g), _row(ln_b)]
    if b is not None:
        in_specs.append(row)
        args.append(_row(b))
    return pl.pallas_call(
        functools.partial(_proj_kernel, ln=ln_g is not None, bias=b is not None),
        grid=(T // TM,),
        in_specs=in_specs,
        out_specs=tile,
        out_shape=jax.ShapeDtypeStruct((T, D), F32),
        compiler_params=_params("arbitrary"),
        name="proj_ln" if ln_g is not None else "proj",
    )(*args)


def _trunk(x3, conv_st, shift_st, wkv_st, ffn_st, p):
    B, T, D = x3.shape
    x = x3.reshape(B * T, D)
    depth = p['norm_g'].shape[0]
    new_conv, new_shift, new_wkv, new_ffn = [], [], [], []
    v_first = None
    for i in range(depth):
        g = p['norm_g'][i]
        j = i // 2
        if i % 2 == 0:
            u = _glu_call(x, g[0], p['conv_pw1_w'], p['conv_pw1_b'][j], layer=j)
            c = _dwconv_call(u, p['conv_dw_w'][j], p['conv_dw_b'][j],
                             None if conv_st is None else conv_st[j], seq_len=T)
            new_conv.append(u.reshape(B, T, D)[:, T - (CONV_WIDTH - 1):])
            x = _proj_call(c, x, p['conv_pw2_w'], g[1], layer=j, ln_g=p['conv_ln_g'][j], ln_b=p['conv_ln_b'][j],
                           b=p['conv_pw2_b'][j])
        else:
            q = {k[5:]: v for k, v in p.items() if k.startswith('rwkv_')}
            rkvkal, gate, h_last = _rwkv_in_call(x, g[0], q, None if shift_st is None else shift_st[j],
                                                 v_first if j > 0 else None, seq_len=T, layer=j)
            if j == 0:
                v_first = rkvkal[2]
            s0 = jnp.zeros((B, D // HEAD_SIZE, HEAD_SIZE, HEAD_SIZE), F32) if wkv_st is None else wkv_st[j]
            z, s_new = _wkv_call(rkvkal, gate, q['r_k'][j], q['ln_g'][j], q['ln_b'][j], s0, seq_len=T)
            new_shift.append(h_last)
            new_wkv.append(s_new)
            x = _proj_call(z, x, q['w_o'], g[1], layer=j)
        x, nb = _ffn_call(x, g[2], g[3], p['ffn_w_in'], p['ffn_dw_w'][i], p['ffn_dw_b'][i], p['ffn_w_out'],
                          None if ffn_st is None else ffn_st[i], seq_len=T, layer=i)
        new_ffn.append(nb)
    return (x.reshape(B, T, D), jnp.stack(new_conv), jnp.stack(new_shift), jnp.stack(new_wkv), jnp.stack(new_ffn))


def kernel(x_prompt, x_sample, state_conv_mix, state_rwkv_shift, state_rwkv_wkv, state_ffn_conv, norm_g,
           conv_pw1_w, conv_pw1_b, conv_dw_w, conv_dw_b, conv_ln_g, conv_ln_b, conv_pw2_w, conv_pw2_b,
           rwkv_mix, rwkv_w_r, rwkv_w_k, rwkv_w_v, rwkv_w_o, rwkv_w0, rwkv_w1, rwkv_w2, rwkv_a0, rwkv_a1,
           rwkv_a2, rwkv_v0, rwkv_v1, rwkv_v2, rwkv_g1, rwkv_g2, rwkv_k_k, rwkv_k_a, rwkv_r_k, rwkv_ln_g,
           rwkv_ln_b, ffn_w_in, ffn_dw_w, ffn_dw_b, ffn_w_out):
    bf = lambda w: w.astype(BF16)
    pad_c = lambda w: _pad_rank(w, 2).astype(BF16)
    pad_r = lambda w: _pad_rank(w, 1).astype(BF16)
    p = dict(norm_g=norm_g, conv_pw1_w=bf(conv_pw1_w), conv_pw1_b=conv_pw1_b, conv_dw_w=conv_dw_w,
             conv_dw_b=conv_dw_b, conv_ln_g=conv_ln_g, conv_ln_b=conv_ln_b, conv_pw2_w=bf(conv_pw2_w),
             conv_pw2_b=conv_pw2_b, rwkv_mix=rwkv_mix, rwkv_w_r=bf(rwkv_w_r), rwkv_w_k=bf(rwkv_w_k),
             rwkv_w_v=bf(rwkv_w_v), rwkv_w_o=bf(rwkv_w_o), rwkv_w0=rwkv_w0, rwkv_w1=pad_c(rwkv_w1),
             rwkv_w2=pad_r(rwkv_w2), rwkv_a0=rwkv_a0, rwkv_a1=pad_c(rwkv_a1), rwkv_a2=pad_r(rwkv_a2),
             rwkv_v0=rwkv_v0, rwkv_v1=pad_c(rwkv_v1), rwkv_v2=pad_r(rwkv_v2), rwkv_g1=pad_c(rwkv_g1),
             rwkv_g2=pad_r(rwkv_g2), rwkv_k_k=rwkv_k_k, rwkv_k_a=rwkv_k_a, rwkv_r_k=rwkv_r_k,
             rwkv_ln_g=rwkv_ln_g, rwkv_ln_b=rwkv_ln_b, ffn_w_in=bf(ffn_w_in), ffn_dw_w=ffn_dw_w,
             ffn_dw_b=ffn_dw_b, ffn_w_out=bf(ffn_w_out))
    y_p, conv_p, shift_p, wkv_p, ffn_p = _trunk(x_prompt, None, None, None, None, p)
    y_s, conv_s, shift_s, wkv_s, ffn_s = _trunk(x_sample, state_conv_mix, state_rwkv_shift, state_rwkv_wkv,
                                                state_ffn_conv, p)
    return (y_p, y_s, conv_p, conv_s, shift_p, shift_s, wkv_p, wkv_s, ffn_p, ffn_s)
```

```python
import functools

import jax
import jax.numpy as jnp
from jax import lax
from jax.experimental import pallas as pl
from jax.experimental.pallas import tpu as pltpu

F32 = jnp.float32
BF16 = jnp.bfloat16

RMS_EPS = 1e-6
LN_EPS = 1e-5
GN_EPS = 64e-5
HEAD_SIZE = 64
CONV_WIDTH = 31
FFN_CONV_WIDTH = 3

LANES = 128
SUBLANES = 8
VMEM_LIMIT_BYTES = 56 * 1024 * 1024

TOKEN_TILE = 512
WKV_CHUNK = 64


def _params(*semantics):
    return pltpu.CompilerParams(dimension_semantics=semantics, vmem_limit_bytes=VMEM_LIMIT_BYTES)


def _rmsnorm(x, g):
    return x * lax.rsqrt(jnp.mean(x * x, axis=-1, keepdims=True) + RMS_EPS) * g


def _row(v):
    return v.reshape(1, -1)


GELU_C = 0.7978845608028654
GELU_K = 0.044715


def _gelu_tanh_times(x, u):
    inner = x * (GELU_C + (GELU_C * GELU_K) * (x * x))
    return (x * u) * (0.5 + 0.5 * jnp.tanh(inner))


def _ffn_kernel(*refs, S, L, carry):
    if carry:
        (x_ref, g2_ref, g3_ref, wu_ref, wg_ref, dww_ref, dwb_ref, wo_ref,
         o_ref, nb_ref, h_scr, carry_scr) = refs
    else:
        (x_ref, g2_ref, g3_ref, wu_ref, wg_ref, dww_ref, dwb_ref, wo_ref, halo_ref,
         o_ref, nb_ref, h_scr) = refs
    i = pl.program_id(0)
    j = pl.program_id(1)
    nj = pl.num_programs(1)
    K = FFN_CONV_WIDTH - 1
    tf = wu_ref.shape[-1]

    @pl.when(j == 0)
    def _():
        h_scr[...] = _rmsnorm(x_ref[...], g2_ref[...]).astype(BF16)
        o_ref[...] = jnp.zeros(o_ref.shape, F32)

    if carry:
        @pl.when(i == 0)
        def _():
            carry_scr[j] = jnp.zeros((SUBLANES, tf), F32)

    h = h_scr[...]
    w = dww_ref[...]
    nsub = 2 if tf % (2 * LANES) == 0 else 1
    ts = tf // nsub
    cols = [slice(s * ts, (s + 1) * ts) for s in range(nsub)]
    up = [(jnp.dot(h, wu_ref[:, cs], preferred_element_type=F32),
           jnp.dot(h, wg_ref[:, cs], preferred_element_type=F32)) for cs in cols]
    contrib = None
    sub = lax.broadcasted_iota(jnp.int32, (SUBLANES, ts), 0)
    for cs, (u, gp) in zip(cols, up):
        prev1 = pltpu.roll(gp, 1, axis=0)
        prev2 = pltpu.roll(gp, 2, axis=0)
        fix1, fix2 = [], []
        for s in range(S):
            if carry:
                h0, h1 = carry_scr[j, SUBLANES - 2:SUBLANES - 1, cs], carry_scr[j, SUBLANES - 1:SUBLANES, cs]
            else:
                h0, h1 = halo_ref[s, 0:1, cs], halo_ref[s, 1:2, cs]
            top = slice(s * L, s * L + SUBLANES)
            fix1.append(jnp.where(sub == 0, h1, prev1[top]))
            fix2.append(jnp.where(sub == 0, h0, jnp.where(sub == 1, h1, prev2[top])))
        p1 = jnp.concatenate([t for s in range(S) for t in (fix1[s], prev1[s * L + SUBLANES:(s + 1) * L])], axis=0)
        p2 = jnp.concatenate([t for s in range(S) for t in (fix2[s], prev2[s * L + SUBLANES:(s + 1) * L])], axis=0)
        gc = p2 * w[0:1, cs] + p1 * w[1:2, cs] + gp * w[2:3, cs] + dwb_ref[:, cs]
        act = (_gelu_tanh_times(gc, u)).astype(BF16)
        part = jnp.dot(act, wo_ref[cs, :], preferred_element_type=F32)
        contrib = part if contrib is None else contrib + part
        for s in range(S):
            nb_ref[s, :, cs] = gp[(s + 1) * L - K:(s + 1) * L]
        if carry:
            carry_scr[j, :, cs] = gp[L - SUBLANES:L]
    o_ref[...] += contrib

    @pl.when(j == nj - 1)
    def _():
        o_ref[...] = x_ref[...] + _rmsnorm(o_ref[...], g3_ref[...])


def _ffn_chunk(d_ff):
    for tf in (512, 256, 128):
        if d_ff % tf == 0:
            return tf
    raise ValueError(f"d_ff={d_ff} is not a multiple of {LANES}")


def _ffn_call(x, g2, g3, w_in, dw_w, dw_b, w_out, halo, *, seq_len, layer):
    T, D = x.shape
    d_ff = w_out.shape[1]
    tf = _ffn_chunk(d_ff)
    nj = d_ff // tf
    carry = halo is None
    TM = min(TOKEN_TILE, T)
    L = TM if carry else seq_len
    S = TM // L
    assert T % TM == 0 and TM % L == 0 and (carry or L == seq_len)
    nt = T // TM
    in_specs = [
        pl.BlockSpec((TM, D), lambda i, j: (i, 0)),
        pl.BlockSpec((1, D), lambda i, j: (0, 0)),
        pl.BlockSpec((1, D), lambda i, j: (0, 0)),
        pl.BlockSpec((None, D, tf), lambda i, j: (layer, 0, j)),
        pl.BlockSpec((None, D, tf), lambda i, j: (layer, 0, nj + j)),
        pl.BlockSpec((FFN_CONV_WIDTH, tf), lambda i, j: (0, j)),
        pl.BlockSpec((1, tf), lambda i, j: (0, j)),
        pl.BlockSpec((None, tf, D), lambda i, j: (layer, j, 0)),
    ]
    args = [x, _row(g2), _row(g3), w_in, w_in, dw_w, _row(dw_b), w_out]
    scratch = [pltpu.VMEM((TM, D), BF16)]
    nb_spec = pl.BlockSpec((S, FFN_CONV_WIDTH - 1, tf), lambda i, j: (i, 0, j))
    if carry:
        scratch.append(pltpu.VMEM((nj, SUBLANES, tf), F32))
    else:
        in_specs.append(nb_spec)
        args.append(halo)
    out, nb = pl.pallas_call(
        functools.partial(_ffn_kernel, S=S, L=L, carry=carry),
        grid=(nt, nj),
        in_specs=in_specs,
        out_specs=[pl.BlockSpec((TM, D), lambda i, j: (i, 0)), nb_spec],
        out_shape=[jax.ShapeDtypeStruct((T, D), F32),
                   jax.ShapeDtypeStruct((nt * S, FFN_CONV_WIDTH - 1, d_ff), F32)],
        scratch_shapes=scratch,
        compiler_params=_params("arbitrary", "arbitrary"),
        name="ffn_carry" if carry else "ffn_state",
    )(*args)
    return out, (nb[-1:] if carry else nb)


def _tiling(T, seq_len, carry, tile=TOKEN_TILE):
    TM = min(tile, T)
    L = TM if carry else seq_len
    assert T % TM == 0 and TM % L == 0 and L % SUBLANES == 0
    return TM, L, TM // L


def _glu_kernel(x_ref, g_ref, wa_ref, wb_ref, ba_ref, bb_ref, u_ref, h_scr):
    @pl.when(pl.program_id(1) == 0)
    def _():
        h_scr[...] = _rmsnorm(x_ref[...], g_ref[...]).astype(BF16)

    h = h_scr[...]
    tn = wa_ref.shape[-1]
    nsub = 2 if tn % (2 * LANES) == 0 else 1
    cols = [slice(s * (tn // nsub), (s + 1) * (tn // nsub)) for s in range(nsub)]
    ab = [(jnp.dot(h, wa_ref[:, cs], preferred_element_type=F32),
           jnp.dot(h, wb_ref[:, cs], preferred_element_type=F32)) for cs in cols]
    for cs, (a, b) in zip(cols, ab):
        u_ref[:, cs] = (a + ba_ref[:, cs]) * jax.nn.sigmoid(b + bb_ref[:, cs])


def _glu_call(x, g, w1, b1, *, layer):
    T, D = x.shape
    TM = min(TOKEN_TILE, T)
    tn = min(512, D)
    nj = D // tn
    assert T % TM == 0 and D % tn == 0
    return pl.pallas_call(
        _glu_kernel,
        grid=(T // TM, nj),
        in_specs=[
            pl.BlockSpec((TM, D), lambda i, j: (i, 0)),
            pl.BlockSpec((1, D), lambda i, j: (0, 0)),
            pl.BlockSpec((None, D, tn), lambda i, j: (layer, 0, j)),
            pl.BlockSpec((None, D, tn), lambda i, j: (layer, 0, nj + j)),
            pl.BlockSpec((1, tn), lambda i, j: (0, j)),
            pl.BlockSpec((1, tn), lambda i, j: (0, nj + j)),
        ],
        out_specs=pl.BlockSpec((TM, tn), lambda i, j: (i, j)),
        out_shape=jax.ShapeDtypeStruct((T, D), F32),
        scratch_shapes=[pltpu.VMEM((TM, D), BF16)],
        compiler_params=_params("arbitrary", "arbitrary"),
        name="conv_glu",
    )(x, _row(g), w1, w1, _row(b1), _row(b1))


HIST_PAD = 32
DW_ROWS = 64


def _dwconv_kernel(*refs, S, L, carry):
    if carry:
        u_ref, w_ref, b_ref, c_ref, ux_scr, sh_scr, carry_scr = refs
    else:
        u_ref, w_ref, b_ref, st_ref, c_ref, ux_scr, sh_scr = refs
    i = pl.program_id(0)
    j = pl.program_id(1)
    tc = u_ref.shape[-1]
    K = CONV_WIDTH - 1
    ux_scr[:, HIST_PAD:HIST_PAD + L, :] = u_ref[...].reshape(S, L, tc)
    if carry:
        @pl.when(i == 0)
        def _():
            carry_scr[j] = jnp.zeros((HIST_PAD, tc), F32)
        ux_scr[0, 0:HIST_PAD, :] = carry_scr[j]
    else:
        ux_scr[:, HIST_PAD - K:HIST_PAD, :] = st_ref[...]
    nr = HIST_PAD + L - SUBLANES
    for r in range(1, SUBLANES):
        sh_scr[r - 1] = ux_scr[:, r:r + nr, :]
    w = w_ref[...]
    rb = min(DW_ROWS, L)
    for s in range(S):
        for r0 in range(0, L, rb):
            acc = jnp.broadcast_to(b_ref[...], (rb, tc))
            for k in range(CONV_WIDTH):
                m = HIST_PAD - K + k
                r, lo = m % SUBLANES, r0 + m - m % SUBLANES
                tap = ux_scr[s, lo:lo + rb, :] if r == 0 else sh_scr[r - 1, s, lo:lo + rb, :]
                acc = acc + tap * w[k:k + 1, :]
            c_ref[s * L + r0:s * L + r0 + rb, :] = acc
    if carry:
        carry_scr[j] = ux_scr[0, L:L + HIST_PAD, :]


def _dwconv_call(u, w, b, state, *, seq_len):
    T, D = u.shape
    carry = state is None
    TM, L, S = _tiling(T, seq_len, carry)
    tc = min(256, D)
    nc = D // tc
    in_specs = [
        pl.BlockSpec((TM, tc), lambda i, j: (i, j)),
        pl.BlockSpec((CONV_WIDTH, tc), lambda i, j: (0, j)),
        pl.BlockSpec((1, tc), lambda i, j: (0, j)),
    ]
    args = [u, w, _row(b)]
    scratch = [pltpu.VMEM((S, HIST_PAD + L, tc), F32),
               pltpu.VMEM((SUBLANES - 1, S, HIST_PAD + L - SUBLANES, tc), F32)]
    if carry:
        scratch.append(pltpu.VMEM((nc, HIST_PAD, tc), F32))
    else:
        in_specs.append(pl.BlockSpec((S, CONV_WIDTH - 1, tc), lambda i, j: (i, 0, j)))
        args.append(state)
    return pl.pallas_call(
        functools.partial(_dwconv_kernel, S=S, L=L, carry=carry),
        grid=(T // TM, nc),
        in_specs=in_specs,
        out_specs=pl.BlockSpec((TM, tc), lambda i, j: (i, j)),
        out_shape=jax.ShapeDtypeStruct((T, D), F32),
        scratch_shapes=scratch,
        compiler_params=_params("arbitrary", "arbitrary"),
        name="conv_dw_carry" if carry else "conv_dw_state",
    )(*args)


def _proj_kernel(*refs, ln, bias):
    refs = list(refs)
    z_ref, x_ref, w_ref, g_ref = refs[:4]
    rest = refs[4:]
    z = z_ref[...].astype(F32)
    if ln:
        lg_ref, lb_ref = rest[:2]
        rest = rest[2:]
        mu = jnp.mean(z, axis=-1, keepdims=True)
        zc = z - mu
        var = jnp.mean(zc * zc, axis=-1, keepdims=True)
        z = zc * lax.rsqrt(var + LN_EPS) * lg_ref[...] + lb_ref[...]
        z = z * jax.nn.sigmoid(z)
    out = jnp.dot(z.astype(BF16), w_ref[...], preferred_element_type=F32)
    if bias:
        out = out + rest[0][...]
        rest = rest[1:]
    o_ref, = rest
    o_ref[...] = x_ref[...] + _rmsnorm(out, g_ref[...])


PROJ_TILE = 512


def _head_ones(width=LANES):
    r = lax.broadcasted_iota(jnp.int32, (width, width), 0) // HEAD_SIZE
    c = lax.broadcasted_iota(jnp.int32, (width, width), 1) // HEAD_SIZE
    return (r == c).astype(BF16)


def _head_sum(x, ones):
    outs = []
    for p in range(x.shape[1] // LANES):
        xp = x[:, p * LANES:(p + 1) * LANES]
        hi = xp.astype(BF16)
        lo = (xp - hi.astype(F32)).astype(BF16)
        outs.append(jnp.dot(hi, ones, preferred_element_type=F32) + jnp.dot(lo, ones, preferred_element_type=F32))
    return outs[0] if len(outs) == 1 else jnp.concatenate(outs, axis=1)


LORA_PAD = 128
RWKV_IN_TILE = 512
RWKV_IN_COLS = 512


def _softplus(z):
    return jnp.maximum(z, 0.0) + jnp.log(1.0 + jnp.exp(-jnp.abs(z)))


MIX_CHUNK = 512


def _rwkv_in_kernel(*refs, S, L, carry, vres):
    refs = list(refs)
    x_ref, g_ref, mix_ref, wr_ref, wk_ref, wv_ref, w1_ref, a1_ref, g1_ref = refs[:9]
    w2_ref, a2_ref, g2_ref, w0_ref, a0_ref, kk_ref, ka_ref = refs[9:16]
    rest = refs[16:]
    if vres:
        v1_ref, v2_ref, v0_ref, vf_ref = rest[:4]
        rest = rest[4:]
    if not carry:
        sh_ref = rest[0]
        rest = rest[1:]
    r_ref, k_ref, v_ref, kkn_ref, a_ref, lw_ref, go_ref, hl_ref = rest[:8]
    rest = rest[8:]
    mx_scr, tw_scr, ta_scr, tg_scr = rest[:4]
    rest = rest[4:]
    if vres:
        tv_scr = rest[0]
        rest = rest[1:]
    if carry:
        carry_scr, = rest
    i = pl.program_id(0)
    j = pl.program_id(1)
    TM, D = x_ref.shape

    @pl.when(j == 0)
    def _():
        x = x_ref[...]
        rs = lax.rsqrt(jnp.mean(x * x, axis=-1, keepdims=True) + RMS_EPS)
        if carry:
            @pl.when(i == 0)
            def _():
                carry_scr[...] = jnp.zeros((1, D), F32)
        mc = min(MIX_CHUNK, D)
        seg_start = lax.broadcasted_iota(jnp.int32, (TM, mc), 0) % L == 0
        lora = {}

        def acc(name, xm, w_ref, cs):
            part = jnp.dot(xm, w_ref[cs, :], preferred_element_type=F32)
            lora[name] = part if name not in lora else lora[name] + part

        for c0 in range(0, D, mc):
            cs = slice(c0, c0 + mc)
            h = x_ref[:, cs] * rs * g_ref[:, cs]
            if carry:
                first = jnp.broadcast_to(carry_scr[:, cs], (TM, mc))
                carry_scr[:, cs] = h[TM - 1:TM, :]
            else:
                first = jnp.broadcast_to(sh_ref[:, :, cs], (S, L, mc)).reshape(TM, mc)
            for s in range(S):
                hl_ref[s, :, cs] = h[s * L + L - 1:s * L + L, :]
            xx = jnp.where(seg_start, first, pltpu.roll(h, 1, axis=0)) - h
            mixed = [(h + xx * mix_ref[m:m + 1, cs]).astype(BF16) for m in range(6)]
            mx_scr[0, :, cs] = mixed[0]
            mx_scr[1, :, cs] = mixed[2]
            mx_scr[2, :, cs] = mixed[3]
            acc('w', mixed[1], w1_ref, cs)
            acc('a', mixed[4], a1_ref, cs)
            acc('g', mixed[5], g1_ref, cs)
            if vres:
                acc('v', mixed[3], v1_ref, cs)
        tw_scr[...] = jnp.tanh(lora['w']).astype(BF16)
        ta_scr[...] = lora['a'].astype(BF16)
        tg_scr[...] = jax.nn.sigmoid(lora['g']).astype(BF16)
        if vres:
            tv_scr[...] = lora['v'].astype(BF16)

    r = jnp.dot(mx_scr[0], wr_ref[...], preferred_element_type=F32)
    k = jnp.dot(mx_scr[1], wk_ref[...], preferred_element_type=F32)
    v = jnp.dot(mx_scr[2], wv_ref[...], preferred_element_type=F32)
    wl = jnp.dot(tw_scr[...], w2_ref[...], preferred_element_type=F32)
    al = jnp.dot(ta_scr[...], a2_ref[...], preferred_element_type=F32)
    go_ref[...] = jnp.dot(tg_scr[...], g2_ref[...], preferred_element_type=F32).astype(go_ref.dtype)
    w = -_softplus(-(w0_ref[...] + wl)) - 0.5
    lw_ref[...] = -jnp.exp(w)
    a = jax.nn.sigmoid(a0_ref[...] + al)
    if vres:
        vl = jnp.dot(tv_scr[...], v2_ref[...], preferred_element_type=F32)
        v = v + (vf_ref[...].astype(F32) - v) * jax.nn.sigmoid(v0_ref[...] + vl)
    kk = k * kk_ref[...]
    norm = jnp.sqrt(_head_sum(kk * kk, _head_ones()))
    kkn_ref[...] = (kk / jnp.maximum(norm, 1e-12)).astype(kkn_ref.dtype)
    r_ref[...] = r.astype(r_ref.dtype)
    k_ref[...] = (k * (1.0 + (a - 1.0) * ka_ref[...])).astype(k_ref.dtype)
    v_ref[...] = v.astype(v_ref.dtype)
    a_ref[...] = a.astype(a_ref.dtype)


WKV_GROUP_HEADS = LANES // HEAD_SIZE
WKV_GROUPS = 16
INV_BLOCK = 16


def _mm(a, b):
    return jnp.dot(a.astype(BF16), b.astype(BF16), preferred_element_type=F32)


def _mm_nt(a, b):
    return lax.dot_general(a.astype(BF16), b.astype(BF16), (((1,), (1,)), ((), ())), preferred_element_type=F32)


def _mm_tn(a, b):
    return lax.dot_general(a.astype(BF16), b.astype(BF16), (((0,), (0,)), ((), ())), preferred_element_type=F32)


def _wkv_kernel(r_ref, k_ref, v_ref, kk_ref, a_ref, lw_ref, g_ref, rk_ref, lng_ref, lnb_ref, s_ref,
                z_ref, so_ref, st_scr, *, NG, GH):
    c = pl.program_id(2)
    nc = pl.num_programs(2)
    C = r_ref.shape[0]
    GW = GH * C
    assert C == HEAD_SIZE

    @pl.when(c == 0)
    def _():
        for gi in range(NG):
            st_scr[gi] = jnp.concatenate([s_ref[0, GH * gi + h] for h in range(GH)], axis=1)

    def ld(ref, sl):
        return ref[:, sl].astype(F32)

    row = lax.broadcasted_iota(jnp.int32, (C, GW), 0)
    lane = lax.broadcasted_iota(jnp.int32, (C, GW), 1)
    sidx = lane % C
    head = lane // C
    strict = sidx < row
    incl = sidx <= row
    eye = (sidx == row).astype(F32)
    ones = _head_ones(GW)

    def bd(x):
        z = jnp.zeros_like(x)
        return jnp.concatenate([jnp.where(head == h, x, z) for h in range(GH)], axis=0)

    def diag_blocks(zz):
        out = zz[:C]
        for h in range(1, GH):
            out = jnp.where(head == h, zz[h * C:(h + 1) * C], out)
        return out

    lw = lw_ref[...]
    rows_full = lax.broadcasted_iota(jnp.int32, lw.shape, 0)
    lc = lw
    sh = 1
    while sh < C:
        lc = lc + jnp.where(rows_full >= sh, pltpu.roll(lc, sh, axis=0), 0.0)
        sh *= 2

    pairs = range(NG)
    sls = [slice(gi * GW, (gi + 1) * GW) for gi in pairs]
    lhs, bh_kh, v_bd, ltot, sp = [], [], [], [], []
    gm = []
    for sl in sls:
        lwp, lp = lw[:, sl], lc[:, sl]
        lt = lp[C - 1:C, :]
        kp, kkp = ld(k_ref, sl), ld(kk_ref, sl)
        bp = kkp * ld(a_ref, sl)
        e_neg = jnp.exp(-lp)
        e_rem = jnp.exp(lt - lp)
        at = -(kkp * jnp.exp(lp - lwp))
        rt = ld(r_ref, sl) * jnp.exp(lp)
        lhs.append(jnp.concatenate([at, rt], axis=0).astype(BF16))
        gm.append(_mm_nt(lhs[-1], jnp.concatenate([bd(bp * e_neg), bd(kp * e_neg)], axis=0)))
        bh_kh.append(jnp.concatenate([bp * e_rem, kp * e_rem], axis=0).astype(BF16))
        v_bd.append(bd(ld(v_ref, sl)).astype(BF16))
        ltot.append(lt)
    n_ab = [jnp.where(strict, g[:C, :GW], 0.0) for g in gm]
    a_rb = [jnp.where(incl, g[C:, :GW], 0.0) for g in gm]
    a_k = [jnp.concatenate([jnp.where(strict, g[:C, GW:], 0.0), jnp.where(incl, g[C:, GW:], 0.0)], axis=0)
           for g in gm]
    for pp in pairs:
        sp.append(st_scr[pp])
    s_prod = [_mm_nt(lhs[pp], bd(sp[pp])) for pp in pairs]
    av = [_mm(a_k[pp], v_bd[pp]) for pp in pairs]
    blk = [(row // (INV_BLOCK << lvl)) == (sidx // (INV_BLOCK << lvl)) for lvl in range(3)]
    assert INV_BLOCK << 2 == C
    pw = [jnp.where(blk[0], m, 0.0) for m in n_ab]
    tm = [eye + m for m in pw]
    pw_bd = [bd(m).astype(BF16) for m in pw]
    n = 2
    while n < INV_BLOCK:
        pw = [_mm(pw[pp], pw_bd[pp]) for pp in pairs]
        pw_bd = [bd(m).astype(BF16) for m in pw]
        tm = [tm[pp] + _mm(tm[pp], pw_bd[pp]) for pp in pairs]
        n *= 2
    for lvl in (1, 2):
        off = jnp.logical_and(blk[lvl], jnp.logical_not(blk[lvl - 1]))
        t_bd = [bd(m).astype(BF16) for m in tm]
        tl = [_mm(tm[pp], bd(jnp.where(off, n_ab[pp], 0.0))) for pp in pairs]
        tm = [tm[pp] + _mm(tl[pp], t_bd[pp]) for pp in pairs]
    u = [_mm(tm[pp], bd(s_prod[pp][:C] + av[pp][:C])) for pp in pairs]
    y = [s_prod[pp][C:] + av[pp][C:] + _mm(a_rb[pp], bd(u[pp])) for pp in pairs]
    for pp in pairs:
        zz = _mm_tn(jnp.concatenate([u[pp], ld(v_ref, sls[pp])], axis=0), bh_kh[pp])
        st_scr[pp] = sp[pp] * jnp.exp(ltot[pp]) + diag_blocks(zz)
    def head_sums(parts):
        stacked = jnp.concatenate(parts, axis=0) if NG > 1 else parts[0]
        m = _mm(stacked, ones)
        return [m[pp * C:(pp + 1) * C] for pp in pairs]

    mu = head_sums([t * (1.0 / HEAD_SIZE) for t in y])
    yc = [y[pp] - mu[pp] for pp in pairs]
    var = head_sums([t * t * (1.0 / HEAD_SIZE) for t in yc])
    rk_sum = head_sums([ld(r_ref, sl) * ld(k_ref, sl) * rk_ref[:, sl] for sl in sls])
    for pp, sl in enumerate(sls):
        yn = yc[pp] * lax.rsqrt(var[pp] + GN_EPS) * lng_ref[:, sl] + lnb_ref[:, sl]
        z_ref[:, sl] = ((yn + rk_sum[pp] * ld(v_ref, sl)) * ld(g_ref, sl)).astype(z_ref.dtype)

    @pl.when(c == nc - 1)
    def _():
        for gi in range(NG):
            for h in range(GH):
                so_ref[0, GH * gi + h] = st_scr[gi, :, h * C:(h + 1) * C]


def _wkv_call(rkvkal, gate, r_k, ln_g, ln_b, state, *, seq_len):
    T, D = gate.shape
    B = T // seq_len
    C = WKV_CHUNK
    NC = seq_len // C
    GH = min(WKV_GROUP_HEADS, D // HEAD_SIZE)
    GW = GH * HEAD_SIZE
    NG = min(WKV_GROUPS, D // GW)
    Q = D // (NG * GW)
    assert seq_len % C == 0 and D % (NG * GW) == 0 and GW % LANES == 0
    tile = pl.BlockSpec((C, NG * GW), lambda b, q, c: (b * NC + c, q))
    row = pl.BlockSpec((1, NG * GW), lambda b, q, c: (0, q))
    st = pl.BlockSpec((1, NG * GH, HEAD_SIZE, HEAD_SIZE), lambda b, q, c: (b, q, 0, 0))
    return pl.pallas_call(
        functools.partial(_wkv_kernel, NG=NG, GH=GH),
        grid=(B, Q, NC),
        in_specs=[tile] * 7 + [row] * 3 + [st],
        out_specs=[tile, st],
        out_shape=[jax.ShapeDtypeStruct((T, D), BF16), jax.ShapeDtypeStruct(state.shape, F32)],
        scratch_shapes=[pltpu.VMEM((NG, HEAD_SIZE, GW), F32)],
        compiler_params=_params("arbitrary", "arbitrary", "arbitrary"),
        name="rwkv_wkv",
    )(*rkvkal, gate, _row(r_k.reshape(-1)), _row(ln_g), _row(ln_b), state)


def _pad_rank(w, axis):
    pad = [(0, 0)] * w.ndim
    pad[axis] = (0, -w.shape[axis] % LORA_PAD)
    return jnp.pad(w, pad)


def _rwkv_in_call(x, g, p, shift, v_first, *, seq_len, layer):
    T, D = x.shape
    carry = shift is None
    vres = v_first is not None
    vl = layer - 1
    TM, L, S = _tiling(T, seq_len, carry, RWKV_IN_TILE)
    tn = min(RWKV_IN_COLS, D)
    nj = D // tn
    row_d = pl.BlockSpec((1, D), lambda i, j: (0, 0))
    row_n = pl.BlockSpec((1, tn), lambda i, j: (0, j))
    tile_n = pl.BlockSpec((TM, tn), lambda i, j: (i, j))

    def col_w(lyr):
        return pl.BlockSpec((None, D, tn), lambda i, j: (lyr, 0, j))

    def lora1(a, lyr):
        return pl.BlockSpec((None,) + a.shape[1:], lambda i, j: (lyr, 0, 0))

    def lora2(a, lyr):
        return pl.BlockSpec((None, a.shape[1], tn), lambda i, j: (lyr, 0, j))

    in_specs = [pl.BlockSpec((TM, D), lambda i, j: (i, 0)), row_d, pl.BlockSpec((6, D), lambda i, j: (0, 0)),
                col_w(layer), col_w(layer), col_w(layer),
                lora1(p['w1'], layer), lora1(p['a1'], layer), lora1(p['g1'], layer),
                lora2(p['w2'], layer), lora2(p['a2'], layer), lora2(p['g2'], layer),
                row_n, row_n, row_n, row_n]
    args = [x, _row(g), p['mix'][layer], p['w_r'], p['w_k'], p['w_v'], p['w1'], p['a1'], p['g1'],
            p['w2'], p['a2'], p['g2'],
            _row(p['w0'][layer]), _row(p['a0'][layer]), _row(p['k_k'][layer]), _row(p['k_a'][layer])]
    if vres:
        in_specs += [lora1(p['v1'], vl), lora2(p['v2'], vl), row_n, tile_n]
        args += [p['v1'], p['v2'], _row(p['v0'][vl]), v_first]
    if not carry:
        in_specs.append(pl.BlockSpec((S, 1, D), lambda i, j: (i, 0, 0)))
        args.append(shift.reshape(-1, 1, D))
    nseg = (T // TM) * S
    out_specs = [tile_n] * 7 + [pl.BlockSpec((S, 1, D), lambda i, j: (i, 0, 0))]
    out_shape = ([jax.ShapeDtypeStruct((T, D), BF16)] * 5 + [jax.ShapeDtypeStruct((T, D), F32)]
                 + [jax.ShapeDtypeStruct((T, D), BF16), jax.ShapeDtypeStruct((nseg, 1, D), F32)])
    scratch = [pltpu.VMEM((3, TM, D), BF16), pltpu.VMEM((TM, p['w1'].shape[2]), BF16),
               pltpu.VMEM((TM, p['a1'].shape[2]), BF16), pltpu.VMEM((TM, p['g1'].shape[2]), BF16)]
    if vres:
        scratch.append(pltpu.VMEM((TM, p['v1'].shape[2]), BF16))
    if carry:
        scratch.append(pltpu.VMEM((1, D), F32))
    outs = pl.pallas_call(
        functools.partial(_rwkv_in_kernel, S=S, L=L, carry=carry, vres=vres),
        grid=(T // TM, nj),
        in_specs=in_specs,
        out_specs=out_specs,
        out_shape=out_shape,
        scratch_shapes=scratch,
        compiler_params=_params("arbitrary", "arbitrary"),
        name="rwkv_in_carry" if carry else "rwkv_in_state",
    )(*args)
    r, k, v, kk, a, lw, gate, hl = outs
    h_last = hl[-1:, 0, :] if carry else hl[:, 0, :]
    return (r, k, v, kk, a, lw), gate, h_last


def _proj_call(z, x, w, g, *, layer, ln_g=None, ln_b=None, b=None):
    T, D = x.shape
    TM = min(PROJ_TILE, T)
    assert T % TM == 0
    row = pl.BlockSpec((1, D), lambda i: (0, 0))
    tile = pl.BlockSpec((TM, D), lambda i: (i, 0))
    in_specs = [tile, tile, pl.BlockSpec((None, D, D), lambda i: (layer, 0, 0)), row]
    args = [z, x, w, _row(g)]
    if ln_g is not None:
        in_specs += [row, row]
        args += [_row(ln_g), _row(ln_b)]
    if b is not None:
        in_specs.append(row)
        args.append(_row(b))
    return pl.pallas_call(
        functools.partial(_proj_kernel, ln=ln_g is not None, bias=b is not None),
        grid=(T // TM,),
        in_specs=in_specs,
        out_specs=tile,
        out_shape=jax.ShapeDtypeStruct((T, D), F32),
        compiler_params=_params("arbitrary"),
        name="proj_ln" if ln_g is not None else "proj",
    )(*args)


def _trunk(x3, conv_st, shift_st, wkv_st, ffn_st, p):
    B, T, D = x3.shape
    x = x3.reshape(B * T, D)
    depth = p['norm_g'].shape[0]
    new_conv, new_shift, new_wkv, new_ffn = [], [], [], []
    v_first = None
    for i in range(depth):
        g = p['norm_g'][i]
        j = i // 2
        if i % 2 == 0:
            u = _glu_call(x, g[0], p['conv_pw1_w'], p['conv_pw1_b'][j], layer=j)
            c = _dwconv_call(u, p['conv_dw_w'][j], p['conv_dw_b'][j],
                             None if conv_st is None else conv_st[j], seq_len=T)
            new_conv.append(u.reshape(B, T, D)[:, T - (CONV_WIDTH - 1):])
            x = _proj_call(c, x, p['conv_pw2_w'], g[1], layer=j, ln_g=p['conv_ln_g'][j], ln_b=p['conv_ln_b'][j],
                           b=p['conv_pw2_b'][j])
        else:
            q = {k[5:]: v for k, v in p.items() if k.startswith('rwkv_')}
            rkvkal, gate, h_last = _rwkv_in_call(x, g[0], q, None if shift_st is None else shift_st[j],
                                                 v_first if j > 0 else None, seq_len=T, layer=j)
            if j == 0:
                v_first = rkvkal[2]
            s0 = jnp.zeros((B, D // HEAD_SIZE, HEAD_SIZE, HEAD_SIZE), F32) if wkv_st is None else wkv_st[j]
            z, s_new = _wkv_call(rkvkal, gate, q['r_k'][j], q['ln_g'][j], q['ln_b'][j], s0, seq_len=T)
            new_shift.append(h_last)
            new_wkv.append(s_new)
            x = _proj_call(z, x, q['w_o'], g[1], layer=j)
        x, nb = _ffn_call(x, g[2], g[3], p['ffn_w_in'], p['ffn_dw_w'][i], p['ffn_dw_b'][i], p['ffn_w_out'],
                          None if ffn_st is None else ffn_st[i], seq_len=T, layer=i)
        new_ffn.append(nb)
    return (x.reshape(B, T, D), jnp.stack(new_conv), jnp.stack(new_shift), jnp.stack(new_wkv), jnp.stack(new_ffn))


def kernel(x_prompt, x_sample, state_conv_mix, state_rwkv_shift, state_rwkv_wkv, state_ffn_conv, norm_g,
           conv_pw1_w, conv_pw1_b, conv_dw_w, conv_dw_b, conv_ln_g, conv_ln_b, conv_pw2_w, conv_pw2_b,
           rwkv_mix, rwkv_w_r, rwkv_w_k, rwkv_w_v, rwkv_w_o, rwkv_w0, rwkv_w1, rwkv_w2, rwkv_a0, rwkv_a1,
           rwkv_a2, rwkv_v0, rwkv_v1, rwkv_v2, rwkv_g1, rwkv_g2, rwkv_k_k, rwkv_k_a, rwkv_r_k, rwkv_ln_g,
           rwkv_ln_b, ffn_w_in, ffn_dw_w, ffn_dw_b, ffn_w_out):
    bf = lambda w: w.astype(BF16)
    pad_c = lambda w: _pad_rank(w, 2).astype(BF16)
    pad_r = lambda w: _pad_rank(w, 1).astype(BF16)
    p = dict(norm_g=norm_g, conv_pw1_w=bf(conv_pw1_w), conv_pw1_b=conv_pw1_b, conv_dw_w=conv_dw_w,
             conv_dw_b=conv_dw_b, conv_ln_g=conv_ln_g, conv_ln_b=conv_ln_b, conv_pw2_w=bf(conv_pw2_w),
             conv_pw2_b=conv_pw2_b, rwkv_mix=rwkv_mix, rwkv_w_r=bf(rwkv_w_r), rwkv_w_k=bf(rwkv_w_k),
             rwkv_w_v=bf(rwkv_w_v), rwkv_w_o=bf(rwkv_w_o), rwkv_w0=rwkv_w0, rwkv_w1=pad_c(rwkv_w1),
             rwkv_w2=pad_r(rwkv_w2), rwkv_a0=rwkv_a0, rwkv_a1=pad_c(rwkv_a1), rwkv_a2=pad_r(rwkv_a2),
             rwkv_v0=rwkv_v0, rwkv_v1=pad_c(rwkv_v1), rwkv_v2=pad_r(rwkv_v2), rwkv_g1=pad_c(rwkv_g1),
             rwkv_g2=pad_r(rwkv_g2), rwkv_k_k=rwkv_k_k, rwkv_k_a=rwkv_k_a, rwkv_r_k=rwkv_r_k,
             rwkv_ln_g=rwkv_ln_g, rwkv_ln_b=rwkv_ln_b, ffn_w_in=bf(ffn_w_in), ffn_dw_w=ffn_dw_w,
             ffn_dw_b=ffn_dw_b, ffn_w_out=bf(ffn_w_out))
    y_p, conv_p, shift_p, wkv_p, ffn_p = _trunk(x_prompt, None, None, None, None, p)
    y_s, conv_s, shift_s, wkv_s, ffn_s = _trunk(x_sample, state_conv_mix, state_rwkv_shift, state_rwkv_wkv,
                                                state_ffn_conv, p)
    return (y_p, y_s, conv_p, conv_s, shift_p, shift_s, wkv_p, wkv_s, ffn_p, ffn_s)
```

```python
import functools

import jax
import jax.numpy as jnp
from jax import lax
from jax.experimental import pallas as pl
from jax.experimental.pallas import tpu as pltpu

F32 = jnp.float32
BF16 = jnp.bfloat16

RMS_EPS = 1e-6
LN_EPS = 1e-5
GN_EPS = 64e-5
HEAD_SIZE = 64
CONV_WIDTH = 31
FFN_CONV_WIDTH = 3

LANES = 128
SUBLANES = 8
VMEM_LIMIT_BYTES = 56 * 1024 * 1024

TOKEN_TILE = 512
WKV_CHUNK = 64


def _params(*semantics):
    return pltpu.CompilerParams(dimension_semantics=semantics, vmem_limit_bytes=VMEM_LIMIT_BYTES)


def _rmsnorm(x, g):
    return x * lax.rsqrt(jnp.mean(x * x, axis=-1, keepdims=True) + RMS_EPS) * g


def _row(v):
    return v.reshape(1, -1)


GELU_C = 0.7978845608028654
GELU_K = 0.044715


def _gelu_tanh_times(x, u):
    inner = x * (GELU_C + (GELU_C * GELU_K) * (x * x))
    return (x * u) * (0.5 + 0.5 * jnp.tanh(inner))


def _ffn_kernel(*refs, S, L, carry):
    if carry:
        (x_ref, g2_ref, g3_ref, wu_ref, wg_ref, dww_ref, dwb_ref, wo_ref,
         o_ref, nb_ref, h_scr, gpx_scr, carry_scr) = refs
    else:
        (x_ref, g2_ref, g3_ref, wu_ref, wg_ref, dww_ref, dwb_ref, wo_ref, halo_ref,
         o_ref, nb_ref, h_scr, gpx_scr) = refs
    i = pl.program_id(0)
    j = pl.program_id(1)
    nj = pl.num_programs(1)
    K = FFN_CONV_WIDTH - 1
    tf = wu_ref.shape[-1]

    @pl.when(j == 0)
    def _():
        h_scr[...] = _rmsnorm(x_ref[...], g2_ref[...]).astype(BF16)
        o_ref[...] = jnp.zeros(o_ref.shape, F32)

    if carry:
        @pl.when(i == 0)
        def _():
            carry_scr[j] = jnp.zeros((SUBLANES, tf), F32)

    h = h_scr[...]
    w = dww_ref[...]
    nsub = 2 if tf % (2 * LANES) == 0 else 1
    ts = tf // nsub
    cols = [slice(s * ts, (s + 1) * ts) for s in range(nsub)]
    up = [(jnp.dot(h, wu_ref[:, cs], preferred_element_type=F32),
           jnp.dot(h, wg_ref[:, cs], preferred_element_type=F32)) for cs in cols]
    contrib = None
    for cs, (u, gp) in zip(cols, up):
        gp3 = gp.reshape(S, L, ts)
        gpx_scr[:, SUBLANES:SUBLANES + L, cs] = gp3
        if carry:
            gpx_scr[0, 0:SUBLANES, cs] = carry_scr[j, :, cs]
        else:
            gpx_scr[:, SUBLANES - K:SUBLANES, cs] = halo_ref[:, :, cs]
        gc = (gpx_scr[:, SUBLANES - 2:SUBLANES - 2 + L, cs] * w[0:1, cs]
              + gpx_scr[:, SUBLANES - 1:SUBLANES - 1 + L, cs] * w[1:2, cs]
              + gp3 * w[2:3, cs] + dwb_ref[:, cs])
        act = _gelu_tanh_times(gc, u.reshape(S, L, ts)).reshape(S * L, ts).astype(BF16)
        part = jnp.dot(act, wo_ref[cs, :], preferred_element_type=F32)
        contrib = part if contrib is None else contrib + part
    o_ref[...] += contrib

    nb_ref[...] = gpx_scr[:, SUBLANES + L - K:SUBLANES + L, :]
    if carry:
        carry_scr[j] = gpx_scr[0, L:L + SUBLANES, :]

    @pl.when(j == nj - 1)
    def _():
        o_ref[...] = x_ref[...] + _rmsnorm(o_ref[...], g3_ref[...])


def _ffn_chunk(d_ff):
    for tf in (512, 256, 128):
        if d_ff % tf == 0:
            return tf
    raise ValueError(f"d_ff={d_ff} is not a multiple of {LANES}")


def _ffn_call(x, g2, g3, w_in, dw_w, dw_b, w_out, halo, *, seq_len, layer):
    T, D = x.shape
    d_ff = w_out.shape[1]
    tf = _ffn_chunk(d_ff)
    nj = d_ff // tf
    carry = halo is None
    TM = min(TOKEN_TILE, T)
    L = TM if carry else seq_len
    S = TM // L
    assert T % TM == 0 and TM % L == 0 and (carry or L == seq_len)
    nt = T // TM
    in_specs = [
        pl.BlockSpec((TM, D), lambda i, j: (i, 0)),
        pl.BlockSpec((1, D), lambda i, j: (0, 0)),
        pl.BlockSpec((1, D), lambda i, j: (0, 0)),
        pl.BlockSpec((None, D, tf), lambda i, j: (layer, 0, j)),
        pl.BlockSpec((None, D, tf), lambda i, j: (layer, 0, nj + j)),
        pl.BlockSpec((FFN_CONV_WIDTH, tf), lambda i, j: (0, j)),
        pl.BlockSpec((1, tf), lambda i, j: (0, j)),
        pl.BlockSpec((None, tf, D), lambda i, j: (layer, j, 0)),
    ]
    args = [x, _row(g2), _row(g3), w_in, w_in, dw_w, _row(dw_b), w_out]
    scratch = [pltpu.VMEM((TM, D), BF16), pltpu.VMEM((S, SUBLANES + L, tf), F32)]
    nb_spec = pl.BlockSpec((S, FFN_CONV_WIDTH - 1, tf), lambda i, j: (i, 0, j))
    if carry:
        scratch.append(pltpu.VMEM((nj, SUBLANES, tf), F32))
    else:
        in_specs.append(pl.BlockSpec((None, S, FFN_CONV_WIDTH - 1, tf), lambda i, j: (layer, i, 0, j)))
        args.append(halo)
    out, nb = pl.pallas_call(
        functools.partial(_ffn_kernel, S=S, L=L, carry=carry),
        grid=(nt, nj),
        in_specs=in_specs,
        out_specs=[pl.BlockSpec((TM, D), lambda i, j: (i, 0)), nb_spec],
        out_shape=[jax.ShapeDtypeStruct((T, D), F32),
                   jax.ShapeDtypeStruct((nt * S, FFN_CONV_WIDTH - 1, d_ff), F32)],
        scratch_shapes=scratch,
        compiler_params=_params("arbitrary", "arbitrary"),
        name="ffn_carry" if carry else "ffn_state",
    )(*args)
    return out, (nb[-1:] if carry else nb)


def _tiling(T, seq_len, carry, tile=TOKEN_TILE):
    TM = min(tile, T)
    L = TM if carry else seq_len
    assert T % TM == 0 and TM % L == 0 and L % SUBLANES == 0
    return TM, L, TM // L


def _glu_kernel(x_ref, g_ref, wa_ref, wb_ref, ba_ref, bb_ref, u_ref, h_scr):
    @pl.when(pl.program_id(1) == 0)
    def _():
        h_scr[...] = _rmsnorm(x_ref[...], g_ref[...]).astype(BF16)

    h = h_scr[...]
    tn = wa_ref.shape[-1]
    nsub = 2 if tn % (2 * LANES) == 0 else 1
    cols = [slice(s * (tn // nsub), (s + 1) * (tn // nsub)) for s in range(nsub)]
    ab = [(jnp.dot(h, wa_ref[:, cs], preferred_element_type=F32),
           jnp.dot(h, wb_ref[:, cs], preferred_element_type=F32)) for cs in cols]
    for cs, (a, b) in zip(cols, ab):
        u_ref[:, cs] = (a + ba_ref[:, cs]) * jax.nn.sigmoid(b + bb_ref[:, cs])


def _glu_call(x, g, w1, b1, *, layer):
    T, D = x.shape
    TM = min(TOKEN_TILE, T)
    tn = min(512, D)
    nj = D // tn
    assert T % TM == 0 and D % tn == 0
    return pl.pallas_call(
        _glu_kernel,
        grid=(T // TM, nj),
        in_specs=[
            pl.BlockSpec((TM, D), lambda i, j: (i, 0)),
            pl.BlockSpec((1, D), lambda i, j: (0, 0)),
            pl.BlockSpec((None, D, tn), lambda i, j: (layer, 0, j)),
            pl.BlockSpec((None, D, tn), lambda i, j: (layer, 0, nj + j)),
            pl.BlockSpec((1, tn), lambda i, j: (0, j)),
            pl.BlockSpec((1, tn), lambda i, j: (0, nj + j)),
        ],
        out_specs=pl.BlockSpec((TM, tn), lambda i, j: (i, j)),
        out_shape=jax.ShapeDtypeStruct((T, D), F32),
        scratch_shapes=[pltpu.VMEM((TM, D), BF16)],
        compiler_params=_params("arbitrary", "arbitrary"),
        name="conv_glu",
    )(x, _row(g), w1, w1, _row(b1), _row(b1))


HIST_PAD = 32
DW_ROWS = 64


def _dwconv_kernel(*refs, S, L, carry):
    if carry:
        u_ref, w_ref, b_ref, c_ref, ux_scr, sh_scr, carry_scr = refs
    else:
        u_ref, w_ref, b_ref, st_ref, c_ref, ux_scr, sh_scr = refs
    i = pl.program_id(0)
    j = pl.program_id(1)
    tc = u_ref.shape[-1]
    K = CONV_WIDTH - 1
    ux_scr[:, HIST_PAD:HIST_PAD + L, :] = u_ref[...].reshape(S, L, tc)
    if carry:
        @pl.when(i == 0)
        def _():
            carry_scr[j] = jnp.zeros((HIST_PAD, tc), F32)
        ux_scr[0, 0:HIST_PAD, :] = carry_scr[j]
    else:
        ux_scr[:, HIST_PAD - K:HIST_PAD, :] = st_ref[...]
    nr = HIST_PAD + L - SUBLANES
    for r in range(1, SUBLANES):
        sh_scr[r - 1] = ux_scr[:, r:r + nr, :]
    w = w_ref[...]
    rb = min(DW_ROWS, L)
    for s in range(S):
        for r0 in range(0, L, rb):
            acc = jnp.broadcast_to(b_ref[...], (rb, tc))
            for k in range(CONV_WIDTH):
                m = HIST_PAD - K + k
                r, lo = m % SUBLANES, r0 + m - m % SUBLANES
                tap = ux_scr[s, lo:lo + rb, :] if r == 0 else sh_scr[r - 1, s, lo:lo + rb, :]
                acc = acc + tap * w[k:k + 1, :]
            c_ref[s * L + r0:s * L + r0 + rb, :] = acc
    if carry:
        carry_scr[j] = ux_scr[0, L:L + HIST_PAD, :]


def _dwconv_call(u, w, b, state, *, seq_len, layer):
    T, D = u.shape
    carry = state is None
    TM, L, S = _tiling(T, seq_len, carry)
    tc = min(256, D)
    nc = D // tc
    in_specs = [
        pl.BlockSpec((TM, tc), lambda i, j: (i, j)),
        pl.BlockSpec((CONV_WIDTH, tc), lambda i, j: (0, j)),
        pl.BlockSpec((1, tc), lambda i, j: (0, j)),
    ]
    args = [u, w, _row(b)]
    scratch = [pltpu.VMEM((S, HIST_PAD + L, tc), F32),
               pltpu.VMEM((SUBLANES - 1, S, HIST_PAD + L - SUBLANES, tc), F32)]
    if carry:
        scratch.append(pltpu.VMEM((nc, HIST_PAD, tc), F32))
    else:
        in_specs.append(pl.BlockSpec((None, S, CONV_WIDTH - 1, tc), lambda i, j: (layer, i, 0, j)))
        args.append(state)
    return pl.pallas_call(
        functools.partial(_dwconv_kernel, S=S, L=L, carry=carry),
        grid=(T // TM, nc),
        in_specs=in_specs,
        out_specs=pl.BlockSpec((TM, tc), lambda i, j: (i, j)),
        out_shape=jax.ShapeDtypeStruct((T, D), F32),
        scratch_shapes=scratch,
        compiler_params=_params("arbitrary", "arbitrary"),
        name="conv_dw_carry" if carry else "conv_dw_state",
    )(*args)


def _proj_kernel(*refs, ln, bias):
    refs = list(refs)
    z_ref, x_ref, w_ref, g_ref = refs[:4]
    rest = refs[4:]
    z = z_ref[...].astype(F32)
    if ln:
        lg_ref, lb_ref = rest[:2]
        rest = rest[2:]
        mu = jnp.mean(z, axis=-1, keepdims=True)
        zc = z - mu
        var = jnp.mean(zc * zc, axis=-1, keepdims=True)
        z = zc * lax.rsqrt(var + LN_EPS) * lg_ref[...] + lb_ref[...]
        z = z * jax.nn.sigmoid(z)
    out = jnp.dot(z.astype(BF16), w_ref[...], preferred_element_type=F32)
    if bias:
        out = out + rest[0][...]
        rest = rest[1:]
    o_ref, = rest
    o_ref[...] = x_ref[...] + _rmsnorm(out, g_ref[...])


PROJ_TILE = 512


def _head_ones(width=LANES):
    r = lax.broadcasted_iota(jnp.int32, (width, width), 0) // HEAD_SIZE
    c = lax.broadcasted_iota(jnp.int32, (width, width), 1) // HEAD_SIZE
    return (r == c).astype(BF16)


def _head_sum(x, ones):
    outs = []
    for p in range(x.shape[1] // LANES):
        xp = x[:, p * LANES:(p + 1) * LANES]
        hi = xp.astype(BF16)
        lo = (xp - hi.astype(F32)).astype(BF16)
        outs.append(jnp.dot(hi, ones, preferred_element_type=F32) + jnp.dot(lo, ones, preferred_element_type=F32))
    return outs[0] if len(outs) == 1 else jnp.concatenate(outs, axis=1)


LORA_PAD = 128
RWKV_IN_TILE = 512
RWKV_IN_COLS = 512


def _softplus(z):
    return jnp.maximum(z, 0.0) + jnp.log(1.0 + jnp.exp(-jnp.abs(z)))


MIX_CHUNK = 512


def _rwkv_in_kernel(*refs, S, L, carry, vres):
    refs = list(refs)
    x_ref, g_ref, mix_ref, wr_ref, wk_ref, wv_ref, w1_ref, a1_ref, g1_ref = refs[:9]
    w2_ref, a2_ref, g2_ref, w0_ref, a0_ref, kk_ref, ka_ref = refs[9:16]
    rest = refs[16:]
    if vres:
        v1_ref, v2_ref, v0_ref, vf_ref = rest[:4]
        rest = rest[4:]
    if not carry:
        sh_ref = rest[0]
        rest = rest[1:]
    r_ref, k_ref, v_ref, kkn_ref, a_ref, lw_ref, go_ref, hl_ref = rest[:8]
    rest = rest[8:]
    mx_scr, tw_scr, ta_scr, tg_scr = rest[:4]
    rest = rest[4:]
    if vres:
        tv_scr = rest[0]
        rest = rest[1:]
    if carry:
        carry_scr, = rest
    i = pl.program_id(0)
    j = pl.program_id(1)
    TM, D = x_ref.shape

    @pl.when(j == 0)
    def _():
        x = x_ref[...]
        rs = lax.rsqrt(jnp.mean(x * x, axis=-1, keepdims=True) + RMS_EPS)
        if carry:
            @pl.when(i == 0)
            def _():
                carry_scr[...] = jnp.zeros((1, D), F32)
        mc = min(MIX_CHUNK, D)
        seg_start = lax.broadcasted_iota(jnp.int32, (TM, mc), 0) % L == 0
        lora = {}

        def acc(name, xm, w_ref, cs):
            part = jnp.dot(xm, w_ref[cs, :], preferred_element_type=F32)
            lora[name] = part if name not in lora else lora[name] + part

        for c0 in range(0, D, mc):
            cs = slice(c0, c0 + mc)
            h = x_ref[:, cs] * rs * g_ref[:, cs]
            if carry:
                first = jnp.broadcast_to(carry_scr[:, cs], (TM, mc))
                carry_scr[:, cs] = h[TM - 1:TM, :]
            else:
                first = jnp.broadcast_to(sh_ref[:, :, cs], (S, L, mc)).reshape(TM, mc)
            for s in range(S):
                hl_ref[s, :, cs] = h[s * L + L - 1:s * L + L, :]
            xx = jnp.where(seg_start, first, pltpu.roll(h, 1, axis=0)) - h
            mixed = [(h + xx * mix_ref[m:m + 1, cs]).astype(BF16) for m in range(6)]
            mx_scr[0, :, cs] = mixed[0]
            mx_scr[1, :, cs] = mixed[2]
            mx_scr[2, :, cs] = mixed[3]
            acc('w', mixed[1], w1_ref, cs)
            acc('a', mixed[4], a1_ref, cs)
            acc('g', mixed[5], g1_ref, cs)
            if vres:
                acc('v', mixed[3], v1_ref, cs)
        tw_scr[...] = jnp.tanh(lora['w']).astype(BF16)
        ta_scr[...] = lora['a'].astype(BF16)
        tg_scr[...] = jax.nn.sigmoid(lora['g']).astype(BF16)
        if vres:
            tv_scr[...] = lora['v'].astype(BF16)

    r = jnp.dot(mx_scr[0], wr_ref[...], preferred_element_type=F32)
    k = jnp.dot(mx_scr[1], wk_ref[...], preferred_element_type=F32)
    v = jnp.dot(mx_scr[2], wv_ref[...], preferred_element_type=F32)
    wl = jnp.dot(tw_scr[...], w2_ref[...], preferred_element_type=F32)
    al = jnp.dot(ta_scr[...], a2_ref[...], preferred_element_type=F32)
    go_ref[...] = jnp.dot(tg_scr[...], g2_ref[...], preferred_element_type=F32).astype(go_ref.dtype)
    w = -_softplus(-(w0_ref[...] + wl)) - 0.5
    lw_ref[...] = -jnp.exp(w)
    a = jax.nn.sigmoid(a0_ref[...] + al)
    if vres:
        vl = jnp.dot(tv_scr[...], v2_ref[...], preferred_element_type=F32)
        v = v + (vf_ref[...].astype(F32) - v) * jax.nn.sigmoid(v0_ref[...] + vl)
    kk = k * kk_ref[...]
    norm = jnp.sqrt(_head_sum(kk * kk, _head_ones()))
    kkn_ref[...] = (kk / jnp.maximum(norm, 1e-12)).astype(kkn_ref.dtype)
    r_ref[...] = r.astype(r_ref.dtype)
    k_ref[...] = (k * (1.0 + (a - 1.0) * ka_ref[...])).astype(k_ref.dtype)
    v_ref[...] = v.astype(v_ref.dtype)
    a_ref[...] = a.astype(a_ref.dtype)


WKV_GROUP_HEADS = LANES // HEAD_SIZE
WKV_GROUPS = 16
INV_BLOCK = 16


def _mm(a, b):
    return jnp.dot(a.astype(BF16), b.astype(BF16), preferred_element_type=F32)


def _mm_nt(a, b):
    return lax.dot_general(a.astype(BF16), b.astype(BF16), (((1,), (1,)), ((), ())), preferred_element_type=F32)


def _mm_tn(a, b):
    return lax.dot_general(a.astype(BF16), b.astype(BF16), (((0,), (0,)), ((), ())), preferred_element_type=F32)


def _wkv_kernel(r_ref, k_ref, v_ref, kk_ref, a_ref, lw_ref, g_ref, rk_ref, lng_ref, lnb_ref, s_ref,
                z_ref, so_ref, st_scr, *, NG, GH):
    c = pl.program_id(2)
    nc = pl.num_programs(2)
    C = r_ref.shape[0]
    GW = GH * C
    assert C == HEAD_SIZE

    @pl.when(c == 0)
    def _():
        for gi in range(NG):
            st_scr[gi] = jnp.concatenate([s_ref[0, GH * gi + h] for h in range(GH)], axis=1)

    def ld(ref, sl):
        return ref[:, sl].astype(F32)

    row = lax.broadcasted_iota(jnp.int32, (C, GW), 0)
    lane = lax.broadcasted_iota(jnp.int32, (C, GW), 1)
    sidx = lane % C
    head = lane // C
    strict = sidx < row
    incl = sidx <= row
    eye = (sidx == row).astype(F32)
    ones = _head_ones(GW)

    def bd(x):
        z = jnp.zeros_like(x)
        return jnp.concatenate([jnp.where(head == h, x, z) for h in range(GH)], axis=0)

    def diag_blocks(zz):
        out = zz[:C]
        for h in range(1, GH):
            out = jnp.where(head == h, zz[h * C:(h + 1) * C], out)
        return out

    lw = lw_ref[...]
    rows_full = lax.broadcasted_iota(jnp.int32, lw.shape, 0)
    lc = lw
    sh = 1
    while sh < C:
        lc = lc + jnp.where(rows_full >= sh, pltpu.roll(lc, sh, axis=0), 0.0)
        sh *= 2

    pairs = range(NG)
    sls = [slice(gi * GW, (gi + 1) * GW) for gi in pairs]
    lhs, bh_kh, v_bd, ltot, sp = [], [], [], [], []
    gm = []
    for sl in sls:
        lwp, lp = lw[:, sl], lc[:, sl]
        lt = lp[C - 1:C, :]
        kp, kkp = ld(k_ref, sl), ld(kk_ref, sl)
        bp = kkp * ld(a_ref, sl)
        e_neg = jnp.exp(-lp)
        e_rem = jnp.exp(lt - lp)
        at = -(kkp * jnp.exp(lp - lwp))
        rt = ld(r_ref, sl) * jnp.exp(lp)
        lhs.append(jnp.concatenate([at, rt], axis=0).astype(BF16))
        gm.append(_mm_nt(lhs[-1], jnp.concatenate([bd(bp * e_neg), bd(kp * e_neg)], axis=0)))
        bh_kh.append(jnp.concatenate([bp * e_rem, kp * e_rem], axis=0).astype(BF16))
        v_bd.append(bd(ld(v_ref, sl)).astype(BF16))
        ltot.append(lt)
    n_ab = [jnp.where(strict, g[:C, :GW], 0.0) for g in gm]
    a_rb = [jnp.where(incl, g[C:, :GW], 0.0) for g in gm]
    a_k = [jnp.concatenate([jnp.where(strict, g[:C, GW:], 0.0), jnp.where(incl, g[C:, GW:], 0.0)], axis=0)
           for g in gm]
    for pp in pairs:
        sp.append(st_scr[pp])
    s_prod = [_mm_nt(lhs[pp], bd(sp[pp])) for pp in pairs]
    av = [_mm(a_k[pp], v_bd[pp]) for pp in pairs]
    blk = [(row // (INV_BLOCK << lvl)) == (sidx // (INV_BLOCK << lvl)) for lvl in range(3)]
    assert INV_BLOCK << 2 == C
    pw = [jnp.where(blk[0], m, 0.0) for m in n_ab]
    tm = [eye + m for m in pw]
    pw_bd = [bd(m).astype(BF16) for m in pw]
    n = 2
    while n < INV_BLOCK:
        pw = [_mm(pw[pp], pw_bd[pp]) for pp in pairs]
        pw_bd = [bd(m).astype(BF16) for m in pw]
        tm = [tm[pp] + _mm(tm[pp], pw_bd[pp]) for pp in pairs]
        n *= 2
    for lvl in (1, 2):
        off = jnp.logical_and(blk[lvl], jnp.logical_not(blk[lvl - 1]))
        t_bd = [bd(m).astype(BF16) for m in tm]
        tl = [_mm(tm[pp], bd(jnp.where(off, n_ab[pp], 0.0))) for pp in pairs]
        tm = [tm[pp] + _mm(tl[pp], t_bd[pp]) for pp in pairs]
    u = [_mm(tm[pp], bd(s_prod[pp][:C] + av[pp][:C])) for pp in pairs]
    y = [s_prod[pp][C:] + av[pp][C:] + _mm(a_rb[pp], bd(u[pp])) for pp in pairs]
    for pp in pairs:
        zz = _mm_tn(jnp.concatenate([u[pp], ld(v_ref, sls[pp])], axis=0), bh_kh[pp])
        st_scr[pp] = sp[pp] * jnp.exp(ltot[pp]) + diag_blocks(zz)
    def head_sums(parts):
        stacked = jnp.concatenate(parts, axis=0) if NG > 1 else parts[0]
        m = _mm(stacked, ones)
        return [m[pp * C:(pp + 1) * C] for pp in pairs]

    mu = head_sums([t * (1.0 / HEAD_SIZE) for t in y])
    yc = [y[pp] - mu[pp] for pp in pairs]
    var = head_sums([t * t * (1.0 / HEAD_SIZE) for t in yc])
    rk_sum = head_sums([ld(r_ref, sl) * ld(k_ref, sl) * rk_ref[:, sl] for sl in sls])
    for pp, sl in enumerate(sls):
        yn = yc[pp] * lax.rsqrt(var[pp] + GN_EPS) * lng_ref[:, sl] + lnb_ref[:, sl]
        z_ref[:, sl] = ((yn + rk_sum[pp] * ld(v_ref, sl)) * ld(g_ref, sl)).astype(z_ref.dtype)

    @pl.when(c == nc - 1)
    def _():
        for gi in range(NG):
            for h in range(GH):
                so_ref[0, GH * gi + h] = st_scr[gi, :, h * C:(h + 1) * C]


def _wkv_call(rkvkal, gate, r_k, ln_g, ln_b, state, *, seq_len, layer):
    T, D = gate.shape
    B = T // seq_len
    C = WKV_CHUNK
    NC = seq_len // C
    GH = min(WKV_GROUP_HEADS, D // HEAD_SIZE)
    GW = GH * HEAD_SIZE
    NG = min(WKV_GROUPS, D // GW)
    Q = D // (NG * GW)
    assert seq_len % C == 0 and D % (NG * GW) == 0 and GW % LANES == 0
    tile = pl.BlockSpec((C, NG * GW), lambda b, q, c: (b * NC + c, q))
    row = pl.BlockSpec((1, NG * GW), lambda b, q, c: (0, q))
    st = pl.BlockSpec((1, NG * GH, HEAD_SIZE, HEAD_SIZE), lambda b, q, c: (b, q, 0, 0))
    st_in = pl.BlockSpec((None, 1, NG * GH, HEAD_SIZE, HEAD_SIZE), lambda b, q, c: (layer, b, q, 0, 0))
    return pl.pallas_call(
        functools.partial(_wkv_kernel, NG=NG, GH=GH),
        grid=(B, Q, NC),
        in_specs=[tile] * 7 + [row] * 3 + [st_in],
        out_specs=[tile, st],
        out_shape=[jax.ShapeDtypeStruct((T, D), BF16), jax.ShapeDtypeStruct(state.shape[1:], F32)],
        scratch_shapes=[pltpu.VMEM((NG, HEAD_SIZE, GW), F32)],
        compiler_params=_params("arbitrary", "arbitrary", "arbitrary"),
        name="rwkv_wkv",
    )(*rkvkal, gate, _row(r_k.reshape(-1)), _row(ln_g), _row(ln_b), state)


def _pad_rank(w, axis):
    pad = [(0, 0)] * w.ndim
    pad[axis] = (0, -w.shape[axis] % LORA_PAD)
    return jnp.pad(w, pad)


def _rwkv_in_call(x, g, p, shift, v_first, *, seq_len, layer):
    T, D = x.shape
    carry = shift is None
    vres = v_first is not None
    vl = layer - 1
    TM, L, S = _tiling(T, seq_len, carry, RWKV_IN_TILE)
    tn = min(RWKV_IN_COLS, D)
    nj = D // tn
    row_d = pl.BlockSpec((1, D), lambda i, j: (0, 0))
    row_n = pl.BlockSpec((1, tn), lambda i, j: (0, j))
    tile_n = pl.BlockSpec((TM, tn), lambda i, j: (i, j))

    def col_w(lyr):
        return pl.BlockSpec((None, D, tn), lambda i, j: (lyr, 0, j))

    def lora1(a, lyr):
        return pl.BlockSpec((None,) + a.shape[1:], lambda i, j: (lyr, 0, 0))

    def lora2(a, lyr):
        return pl.BlockSpec((None, a.shape[1], tn), lambda i, j: (lyr, 0, j))

    in_specs = [pl.BlockSpec((TM, D), lambda i, j: (i, 0)), row_d, pl.BlockSpec((6, D), lambda i, j: (0, 0)),
                col_w(layer), col_w(layer), col_w(layer),
                lora1(p['w1'], layer), lora1(p['a1'], layer), lora1(p['g1'], layer),
                lora2(p['w2'], layer), lora2(p['a2'], layer), lora2(p['g2'], layer),
                row_n, row_n, row_n, row_n]
    args = [x, _row(g), p['mix'][layer], p['w_r'], p['w_k'], p['w_v'], p['w1'], p['a1'], p['g1'],
            p['w2'], p['a2'], p['g2'],
            _row(p['w0'][layer]), _row(p['a0'][layer]), _row(p['k_k'][layer]), _row(p['k_a'][layer])]
    if vres:
        in_specs += [lora1(p['v1'], vl), lora2(p['v2'], vl), row_n, tile_n]
        args += [p['v1'], p['v2'], _row(p['v0'][vl]), v_first]
    if not carry:
        in_specs.append(pl.BlockSpec((S, 1, D), lambda i, j: (i, 0, 0)))
        args.append(shift.reshape(-1, 1, D))
    nseg = (T // TM) * S
    out_specs = [tile_n] * 7 + [pl.BlockSpec((S, 1, D), lambda i, j: (i, 0, 0))]
    out_shape = ([jax.ShapeDtypeStruct((T, D), BF16)] * 5 + [jax.ShapeDtypeStruct((T, D), F32)]
                 + [jax.ShapeDtypeStruct((T, D), BF16), jax.ShapeDtypeStruct((nseg, 1, D), F32)])
    scratch = [pltpu.VMEM((3, TM, D), BF16), pltpu.VMEM((TM, p['w1'].shape[2]), BF16),
               pltpu.VMEM((TM, p['a1'].shape[2]), BF16), pltpu.VMEM((TM, p['g1'].shape[2]), BF16)]
    if vres:
        scratch.append(pltpu.VMEM((TM, p['v1'].shape[2]), BF16))
    if carry:
        scratch.append(pltpu.VMEM((1, D), F32))
    outs = pl.pallas_call(
        functools.partial(_rwkv_in_kernel, S=S, L=L, carry=carry, vres=vres),
        grid=(T // TM, nj),
        in_specs=in_specs,
        out_specs=out_specs,
        out_shape=out_shape,
        scratch_shapes=scratch,
        compiler_params=_params("arbitrary", "arbitrary"),
        name="rwkv_in_carry" if carry else "rwkv_in_state",
    )(*args)
    r, k, v, kk, a, lw, gate, hl = outs
    h_last = hl[-1:, 0, :] if carry else hl[:, 0, :]
    return (r, k, v, kk, a, lw), gate, h_last


def _proj_call(z, x, w, g, *, layer, ln_g=None, ln_b=None, b=None):
    T, D = x.shape
    TM = min(PROJ_TILE, T)
    assert T % TM == 0
    row = pl.BlockSpec((1, D), lambda i: (0, 0))
    tile = pl.BlockSpec((TM, D), lambda i: (i, 0))
    in_specs = [tile, tile, pl.BlockSpec((None, D, D), lambda i: (layer, 0, 0)), row]
    args = [z, x, w, _row(g)]
    if ln_g is not None:
        in_specs += [row, row]
        args += [_row(ln_g), _row(ln_b)]
    if b is not None:
        in_specs.append(row)
        args.append(_row(b))
    return pl.pallas_call(
        functools.partial(_proj_kernel, ln=ln_g is not None, bias=b is not None),
        grid=(T // TM,),
        in_specs=in_specs,
        out_specs=tile,
        out_shape=jax.ShapeDtypeStruct((T, D), F32),
        compiler_params=_params("arbitrary"),
        name="proj_ln" if ln_g is not None else "proj",
    )(*args)


def _trunk(x3, conv_st, shift_st, wkv_st, ffn_st, p):
    B, T, D = x3.shape
    x = x3.reshape(B * T, D)
    depth = p['norm_g'].shape[0]
    new_conv, new_shift, new_wkv, new_ffn = [], [], [], []
    v_first = None
    for i in range(depth):
        g = p['norm_g'][i]
        j = i // 2
        if i % 2 == 0:
            u = _glu_call(x, g[0], p['conv_pw1_w'], p['conv_pw1_b'][j], layer=j)
            c = _dwconv_call(u, p['conv_dw_w'][j], p['conv_dw_b'][j], conv_st, seq_len=T, layer=j)
            new_conv.append(u.reshape(B, T, D)[:, T - (CONV_WIDTH - 1):])
            x = _proj_call(c, x, p['conv_pw2_w'], g[1], layer=j, ln_g=p['conv_ln_g'][j], ln_b=p['conv_ln_b'][j],
                           b=p['conv_pw2_b'][j])
        else:
            q = {k[5:]: v for k, v in p.items() if k.startswith('rwkv_')}
            rkvkal, gate, h_last = _rwkv_in_call(x, g[0], q, None if shift_st is None else shift_st[j],
                                                 v_first if j > 0 else None, seq_len=T, layer=j)
            if j == 0:
                v_first = rkvkal[2]
            if wkv_st is None:
                s0, sl = jnp.zeros((1, B, D // HEAD_SIZE, HEAD_SIZE, HEAD_SIZE), F32), 0
            else:
                s0, sl = wkv_st, j
            z, s_new = _wkv_call(rkvkal, gate, q['r_k'][j], q['ln_g'][j], q['ln_b'][j], s0, seq_len=T, layer=sl)
            new_shift.append(h_last)
            new_wkv.append(s_new)
            x = _proj_call(z, x, q['w_o'], g[1], layer=j)
        x, nb = _ffn_call(x, g[2], g[3], p['ffn_w_in'], p['ffn_dw_w'][i], p['ffn_dw_b'][i], p['ffn_w_out'],
                          ffn_st, seq_len=T, layer=i)
        new_ffn.append(nb)
    return (x.reshape(B, T, D), jnp.stack(new_conv), jnp.stack(new_shift), jnp.stack(new_wkv), jnp.stack(new_ffn))


def kernel(x_prompt, x_sample, state_conv_mix, state_rwkv_shift, state_rwkv_wkv, state_ffn_conv, norm_g,
           conv_pw1_w, conv_pw1_b, conv_dw_w, conv_dw_b, conv_ln_g, conv_ln_b, conv_pw2_w, conv_pw2_b,
           rwkv_mix, rwkv_w_r, rwkv_w_k, rwkv_w_v, rwkv_w_o, rwkv_w0, rwkv_w1, rwkv_w2, rwkv_a0, rwkv_a1,
           rwkv_a2, rwkv_v0, rwkv_v1, rwkv_v2, rwkv_g1, rwkv_g2, rwkv_k_k, rwkv_k_a, rwkv_r_k, rwkv_ln_g,
           rwkv_ln_b, ffn_w_in, ffn_dw_w, ffn_dw_b, ffn_w_out):
    bf = lambda w: w.astype(BF16)
    pad_c = lambda w: _pad_rank(w, 2).astype(BF16)
    pad_r = lambda w: _pad_rank(w, 1).astype(BF16)
    p = dict(norm_g=norm_g, conv_pw1_w=bf(conv_pw1_w), conv_pw1_b=conv_pw1_b, conv_dw_w=conv_dw_w,
             conv_dw_b=conv_dw_b, conv_ln_g=conv_ln_g, conv_ln_b=conv_ln_b, conv_pw2_w=bf(conv_pw2_w),
             conv_pw2_b=conv_pw2_b, rwkv_mix=rwkv_mix, rwkv_w_r=bf(rwkv_w_r), rwkv_w_k=bf(rwkv_w_k),
             rwkv_w_v=bf(rwkv_w_v), rwkv_w_o=bf(rwkv_w_o), rwkv_w0=rwkv_w0, rwkv_w1=pad_c(rwkv_w1),
             rwkv_w2=pad_r(rwkv_w2), rwkv_a0=rwkv_a0, rwkv_a1=pad_c(rwkv_a1), rwkv_a2=pad_r(rwkv_a2),
             rwkv_v0=rwkv_v0, rwkv_v1=pad_c(rwkv_v1), rwkv_v2=pad_r(rwkv_v2), rwkv_g1=pad_c(rwkv_g1),
             rwkv_g2=pad_r(rwkv_g2), rwkv_k_k=rwkv_k_k, rwkv_k_a=rwkv_k_a, rwkv_r_k=rwkv_r_k,
             rwkv_ln_g=rwkv_ln_g, rwkv_ln_b=rwkv_ln_b, ffn_w_in=bf(ffn_w_in), ffn_dw_w=ffn_dw_w,
             ffn_dw_b=ffn_dw_b, ffn_w_out=bf(ffn_w_out))
    y_p, conv_p, shift_p, wkv_p, ffn_p = _trunk(x_prompt, None, None, None, None, p)
    y_s, conv_s, shift_s, wkv_s, ffn_s = _trunk(x_sample, state_conv_mix, state_rwkv_shift, state_rwkv_wkv,
                                                state_ffn_conv, p)
    return (y_p, y_s, conv_p, conv_s, shift_p, shift_s, wkv_p, wkv_s, ffn_p, ffn_s)
```

```python
import functools

import jax
import jax.numpy as jnp
from jax import lax
from jax.experimental import pallas as pl
from jax.experimental.pallas import tpu as pltpu

F32 = jnp.float32
BF16 = jnp.bfloat16

RMS_EPS = 1e-6
LN_EPS = 1e-5
GN_EPS = 64e-5
HEAD_SIZE = 64
CONV_WIDTH = 31
FFN_CONV_WIDTH = 3

LANES = 128
SUBLANES = 8
VMEM_LIMIT_BYTES = 56 * 1024 * 1024

TOKEN_TILE = 512
WKV_CHUNK = 64


def _params(*semantics):
    return pltpu.CompilerParams(dimension_semantics=semantics, vmem_limit_bytes=VMEM_LIMIT_BYTES)


def _rmsnorm(x, g):
    return x * lax.rsqrt(jnp.mean(x * x, axis=-1, keepdims=True) + RMS_EPS) * g


def _row(v):
    return v.reshape(1, -1)


GELU_C = 0.7978845608028654
GELU_K = 0.044715


def _gelu_tanh_times(x, u):
    inner = x * (GELU_C + (GELU_C * GELU_K) * (x * x))
    return (x * u) * (0.5 + 0.5 * jnp.tanh(inner))


def _ffn_kernel(*refs, S, L, carry):
    if carry:
        (x_ref, g2_ref, g3_ref, wu_ref, wg_ref, dww_ref, dwb_ref, wo_ref,
         o_ref, nb_ref, h_scr, gpx_scr, carry_scr) = refs
    else:
        (x_ref, g2_ref, g3_ref, wu_ref, wg_ref, dww_ref, dwb_ref, wo_ref, halo_ref,
         o_ref, nb_ref, h_scr, gpx_scr) = refs
    i = pl.program_id(0)
    j = pl.program_id(1)
    nj = pl.num_programs(1)
    K = FFN_CONV_WIDTH - 1
    tf = wu_ref.shape[-1]

    @pl.when(j == 0)
    def _():
        h_scr[...] = _rmsnorm(x_ref[...], g2_ref[...]).astype(BF16)
        o_ref[...] = jnp.zeros(o_ref.shape, F32)

    if carry:
        @pl.when(i == 0)
        def _():
            carry_scr[j] = jnp.zeros((SUBLANES, tf), F32)

    h = h_scr[...]
    w = dww_ref[...]
    nsub = 2 if tf % (2 * LANES) == 0 else 1
    ts = tf // nsub
    cols = [slice(s * ts, (s + 1) * ts) for s in range(nsub)]
    up = [(jnp.dot(h, wu_ref[:, cs], preferred_element_type=F32),
           jnp.dot(h, wg_ref[:, cs], preferred_element_type=F32)) for cs in cols]
    contrib = None
    for cs, (u, gp) in zip(cols, up):
        gp3 = gp.reshape(S, L, ts)
        gpx_scr[:, SUBLANES:SUBLANES + L, cs] = gp3
        if carry:
            gpx_scr[0, 0:SUBLANES, cs] = carry_scr[j, :, cs]
        else:
            gpx_scr[:, SUBLANES - K:SUBLANES, cs] = halo_ref[:, :, cs]
        gc = (gpx_scr[:, SUBLANES - 2:SUBLANES - 2 + L, cs] * w[0:1, cs]
              + gpx_scr[:, SUBLANES - 1:SUBLANES - 1 + L, cs] * w[1:2, cs]
              + gp3 * w[2:3, cs] + dwb_ref[:, cs])
        act = _gelu_tanh_times(gc, u.reshape(S, L, ts)).reshape(S * L, ts).astype(BF16)
        part = jnp.dot(act, wo_ref[cs, :], preferred_element_type=F32)
        contrib = part if contrib is None else contrib + part
    o_ref[...] += contrib

    nb_ref[...] = gpx_scr[:, SUBLANES + L - K:SUBLANES + L, :]
    if carry:
        carry_scr[j] = gpx_scr[0, L:L + SUBLANES, :]

    @pl.when(j == nj - 1)
    def _():
        o_ref[...] = x_ref[...] + _rmsnorm(o_ref[...], g3_ref[...])


def _ffn_chunk(d_ff):
    for tf in (512, 256, 128):
        if d_ff % tf == 0:
            return tf
    raise ValueError(f"d_ff={d_ff} is not a multiple of {LANES}")


def _ffn_call(x, g2, g3, w_in, dw_w, dw_b, w_out, halo, *, seq_len, layer):
    T, D = x.shape
    d_ff = w_out.shape[1]
    tf = _ffn_chunk(d_ff)
    nj = d_ff // tf
    carry = halo is None
    TM = min(TOKEN_TILE, T)
    L = TM if carry else seq_len
    S = TM // L
    assert T % TM == 0 and TM % L == 0 and (carry or L == seq_len)
    nt = T // TM
    in_specs = [
        pl.BlockSpec((TM, D), lambda i, j: (i, 0)),
        pl.BlockSpec((1, D), lambda i, j: (0, 0)),
        pl.BlockSpec((1, D), lambda i, j: (0, 0)),
        pl.BlockSpec((None, D, tf), lambda i, j: (layer, 0, j)),
        pl.BlockSpec((None, D, tf), lambda i, j: (layer, 0, nj + j)),
        pl.BlockSpec((FFN_CONV_WIDTH, tf), lambda i, j: (0, j)),
        pl.BlockSpec((1, tf), lambda i, j: (0, j)),
        pl.BlockSpec((None, tf, D), lambda i, j: (layer, j, 0)),
    ]
    args = [x, _row(g2), _row(g3), w_in, w_in, dw_w, _row(dw_b), w_out]
    scratch = [pltpu.VMEM((TM, D), BF16), pltpu.VMEM((S, SUBLANES + L, tf), F32)]
    nb_spec = pl.BlockSpec((S, FFN_CONV_WIDTH - 1, tf), lambda i, j: (i, 0, j))
    if carry:
        scratch.append(pltpu.VMEM((nj, SUBLANES, tf), F32))
    else:
        in_specs.append(pl.BlockSpec((None, S, FFN_CONV_WIDTH - 1, tf), lambda i, j: (layer, i, 0, j)))
        args.append(halo)
    out, nb = pl.pallas_call(
        functools.partial(_ffn_kernel, S=S, L=L, carry=carry),
        grid=(nt, nj),
        in_specs=in_specs,
        out_specs=[pl.BlockSpec((TM, D), lambda i, j: (i, 0)), nb_spec],
        out_shape=[jax.ShapeDtypeStruct((T, D), F32),
                   jax.ShapeDtypeStruct((nt * S, FFN_CONV_WIDTH - 1, d_ff), F32)],
        scratch_shapes=scratch,
        compiler_params=_params("arbitrary", "arbitrary"),
        name="ffn_carry" if carry else "ffn_state",
    )(*args)
    return out, (nb[-1:] if carry else nb)


def _tiling(T, seq_len, carry, tile=TOKEN_TILE):
    TM = min(tile, T)
    L = TM if carry else seq_len
    assert T % TM == 0 and TM % L == 0 and L % SUBLANES == 0
    return TM, L, TM // L


def _glu_kernel(x_ref, g_ref, wa_ref, wb_ref, ba_ref, bb_ref, u_ref, h_scr):
    @pl.when(pl.program_id(1) == 0)
    def _():
        h_scr[...] = _rmsnorm(x_ref[...], g_ref[...]).astype(BF16)

    h = h_scr[...]
    tn = wa_ref.shape[-1]
    ts = 2 * LANES if tn % (2 * LANES) == 0 else tn
    cols = [slice(c0, c0 + ts) for c0 in range(0, tn, ts)]
    ab = [(jnp.dot(h, wa_ref[:, cs], preferred_element_type=F32),
           jnp.dot(h, wb_ref[:, cs], preferred_element_type=F32)) for cs in cols]
    for cs, (a, b) in zip(cols, ab):
        u_ref[:, cs] = (a + ba_ref[:, cs]) * jax.nn.sigmoid(b + bb_ref[:, cs])


def _glu_call(x, g, w1, b1, *, layer):
    T, D = x.shape
    TM = min(TOKEN_TILE, T)
    tn = min(GLU_COLS, D)
    nj = D // tn
    assert T % TM == 0 and D % tn == 0
    return pl.pallas_call(
        _glu_kernel,
        grid=(T // TM, nj),
        in_specs=[
            pl.BlockSpec((TM, D), lambda i, j: (i, 0)),
            pl.BlockSpec((1, D), lambda i, j: (0, 0)),
            pl.BlockSpec((None, D, tn), lambda i, j: (layer, 0, j)),
            pl.BlockSpec((None, D, tn), lambda i, j: (layer, 0, nj + j)),
            pl.BlockSpec((1, tn), lambda i, j: (0, j)),
            pl.BlockSpec((1, tn), lambda i, j: (0, nj + j)),
        ],
        out_specs=pl.BlockSpec((TM, tn), lambda i, j: (i, j)),
        out_shape=jax.ShapeDtypeStruct((T, D), F32),
        scratch_shapes=[pltpu.VMEM((TM, D), BF16)],
        compiler_params=_params("arbitrary", "arbitrary"),
        name="conv_glu",
    )(x, _row(g), w1, w1, _row(b1), _row(b1))


HIST_PAD = 32
DW_ROWS = 64
DW_COLS = 256
GLU_COLS = 1024


def _dwconv_kernel(*refs, S, L, carry):
    if carry:
        u_ref, w_ref, b_ref, c_ref, ux_scr, sh_scr, carry_scr = refs
    else:
        u_ref, w_ref, b_ref, st_ref, c_ref, ux_scr, sh_scr = refs
    i = pl.program_id(0)
    j = pl.program_id(1)
    tc = u_ref.shape[-1]
    K = CONV_WIDTH - 1
    ux_scr[:, HIST_PAD:HIST_PAD + L, :] = u_ref[...].reshape(S, L, tc)
    if carry:
        @pl.when(i == 0)
        def _():
            carry_scr[j] = jnp.zeros((HIST_PAD, tc), F32)
        ux_scr[0, 0:HIST_PAD, :] = carry_scr[j]
    else:
        ux_scr[:, HIST_PAD - K:HIST_PAD, :] = st_ref[...]
    nr = HIST_PAD + L - SUBLANES
    for r in range(1, SUBLANES):
        sh_scr[r - 1] = ux_scr[:, r:r + nr, :]
    w = w_ref[...]
    rb = min(DW_ROWS, L)
    for s in range(S):
        for r0 in range(0, L, rb):
            acc = jnp.broadcast_to(b_ref[...], (rb, tc))
            for k in range(CONV_WIDTH):
                m = HIST_PAD - K + k
                r, lo = m % SUBLANES, r0 + m - m % SUBLANES
                tap = ux_scr[s, lo:lo + rb, :] if r == 0 else sh_scr[r - 1, s, lo:lo + rb, :]
                acc = acc + tap * w[k:k + 1, :]
            c_ref[s * L + r0:s * L + r0 + rb, :] = acc
    if carry:
        carry_scr[j] = ux_scr[0, L:L + HIST_PAD, :]


def _dwconv_call(u, w, b, state, *, seq_len, layer):
    T, D = u.shape
    carry = state is None
    TM, L, S = _tiling(T, seq_len, carry)
    tc = min(DW_COLS, D)
    nc = D // tc
    in_specs = [
        pl.BlockSpec((TM, tc), lambda i, j: (i, j)),
        pl.BlockSpec((CONV_WIDTH, tc), lambda i, j: (0, j)),
        pl.BlockSpec((1, tc), lambda i, j: (0, j)),
    ]
    args = [u, w, _row(b)]
    scratch = [pltpu.VMEM((S, HIST_PAD + L, tc), F32),
               pltpu.VMEM((SUBLANES - 1, S, HIST_PAD + L - SUBLANES, tc), F32)]
    if carry:
        scratch.append(pltpu.VMEM((nc, HIST_PAD, tc), F32))
    else:
        in_specs.append(pl.BlockSpec((None, S, CONV_WIDTH - 1, tc), lambda i, j: (layer, i, 0, j)))
        args.append(state)
    return pl.pallas_call(
        functools.partial(_dwconv_kernel, S=S, L=L, carry=carry),
        grid=(T // TM, nc),
        in_specs=in_specs,
        out_specs=pl.BlockSpec((TM, tc), lambda i, j: (i, j)),
        out_shape=jax.ShapeDtypeStruct((T, D), F32),
        scratch_shapes=scratch,
        compiler_params=_params("arbitrary", "arbitrary"),
        name="conv_dw_carry" if carry else "conv_dw_state",
    )(*args)


def _proj_kernel(*refs, ln, bias):
    refs = list(refs)
    z_ref, x_ref, w_ref, g_ref = refs[:4]
    rest = refs[4:]
    z = z_ref[...].astype(F32)
    if ln:
        lg_ref, lb_ref = rest[:2]
        rest = rest[2:]
        mu = jnp.mean(z, axis=-1, keepdims=True)
        zc = z - mu
        var = jnp.mean(zc * zc, axis=-1, keepdims=True)
        z = zc * lax.rsqrt(var + LN_EPS) * lg_ref[...] + lb_ref[...]
        z = z * jax.nn.sigmoid(z)
    out = jnp.dot(z.astype(BF16), w_ref[...], preferred_element_type=F32)
    if bias:
        out = out + rest[0][...]
        rest = rest[1:]
    o_ref, = rest
    o_ref[...] = x_ref[...] + _rmsnorm(out, g_ref[...])


PROJ_TILE = 512


def _head_ones(width=LANES):
    r = lax.broadcasted_iota(jnp.int32, (width, width), 0) // HEAD_SIZE
    c = lax.broadcasted_iota(jnp.int32, (width, width), 1) // HEAD_SIZE
    return (r == c).astype(BF16)


def _head_sum(x, ones):
    outs = []
    for p in range(x.shape[1] // LANES):
        xp = x[:, p * LANES:(p + 1) * LANES]
        hi = xp.astype(BF16)
        lo = (xp - hi.astype(F32)).astype(BF16)
        outs.append(jnp.dot(hi, ones, preferred_element_type=F32) + jnp.dot(lo, ones, preferred_element_type=F32))
    return outs[0] if len(outs) == 1 else jnp.concatenate(outs, axis=1)


LORA_PAD = 128
RWKV_IN_TILE = 512
RWKV_IN_COLS = 512


def _softplus(z):
    return jnp.maximum(z, 0.0) + jnp.log(1.0 + jnp.exp(-jnp.abs(z)))


MIX_CHUNK = 512


def _rwkv_in_kernel(*refs, S, L, carry, vres):
    refs = list(refs)
    x_ref, g_ref, mix_ref, wr_ref, wk_ref, wv_ref, w1_ref, a1_ref, g1_ref = refs[:9]
    w2_ref, a2_ref, g2_ref, w0_ref, a0_ref, kk_ref, ka_ref = refs[9:16]
    rest = refs[16:]
    if vres:
        v1_ref, v2_ref, v0_ref, vf_ref = rest[:4]
        rest = rest[4:]
    if not carry:
        sh_ref = rest[0]
        rest = rest[1:]
    r_ref, k_ref, v_ref, kkn_ref, a_ref, lw_ref, go_ref, hl_ref = rest[:8]
    rest = rest[8:]
    mx_scr, tw_scr, ta_scr, tg_scr = rest[:4]
    rest = rest[4:]
    if vres:
        tv_scr = rest[0]
        rest = rest[1:]
    if carry:
        carry_scr, = rest
    i = pl.program_id(0)
    j = pl.program_id(1)
    TM, D = x_ref.shape

    @pl.when(j == 0)
    def _():
        x = x_ref[...]
        rs = lax.rsqrt(jnp.mean(x * x, axis=-1, keepdims=True) + RMS_EPS)
        if carry:
            @pl.when(i == 0)
            def _():
                carry_scr[...] = jnp.zeros((1, D), F32)
        mc = min(MIX_CHUNK, D)
        seg_start = lax.broadcasted_iota(jnp.int32, (TM, mc), 0) % L == 0
        lora = {}

        def acc(name, xm, w_ref, cs):
            part = jnp.dot(xm, w_ref[cs, :], preferred_element_type=F32)
            lora[name] = part if name not in lora else lora[name] + part

        for c0 in range(0, D, mc):
            cs = slice(c0, c0 + mc)
            h = x_ref[:, cs] * rs * g_ref[:, cs]
            if carry:
                first = jnp.broadcast_to(carry_scr[:, cs], (TM, mc))
                carry_scr[:, cs] = h[TM - 1:TM, :]
            else:
                first = jnp.broadcast_to(sh_ref[:, :, cs], (S, L, mc)).reshape(TM, mc)
            for s in range(S):
                hl_ref[s, :, cs] = h[s * L + L - 1:s * L + L, :]
            xx = jnp.where(seg_start, first, pltpu.roll(h, 1, axis=0)) - h
            mixed = [(h + xx * mix_ref[m:m + 1, cs]).astype(BF16) for m in range(6)]
            mx_scr[0, :, cs] = mixed[0]
            mx_scr[1, :, cs] = mixed[2]
            mx_scr[2, :, cs] = mixed[3]
            acc('w', mixed[1], w1_ref, cs)
            acc('a', mixed[4], a1_ref, cs)
            acc('g', mixed[5], g1_ref, cs)
            if vres:
                acc('v', mixed[3], v1_ref, cs)
        tw_scr[...] = jnp.tanh(lora['w']).astype(BF16)
        ta_scr[...] = lora['a'].astype(BF16)
        tg_scr[...] = jax.nn.sigmoid(lora['g']).astype(BF16)
        if vres:
            tv_scr[...] = lora['v'].astype(BF16)

    r = jnp.dot(mx_scr[0], wr_ref[...], preferred_element_type=F32)
    k = jnp.dot(mx_scr[1], wk_ref[...], preferred_element_type=F32)
    v = jnp.dot(mx_scr[2], wv_ref[...], preferred_element_type=F32)
    wl = jnp.dot(tw_scr[...], w2_ref[...], preferred_element_type=F32)
    al = jnp.dot(ta_scr[...], a2_ref[...], preferred_element_type=F32)
    go_ref[...] = jnp.dot(tg_scr[...], g2_ref[...], preferred_element_type=F32).astype(go_ref.dtype)
    w = -_softplus(-(w0_ref[...] + wl)) - 0.5
    lw_ref[...] = -jnp.exp(w)
    a = jax.nn.sigmoid(a0_ref[...] + al)
    if vres:
        vl = jnp.dot(tv_scr[...], v2_ref[...], preferred_element_type=F32)
        v = v + (vf_ref[...].astype(F32) - v) * jax.nn.sigmoid(v0_ref[...] + vl)
    kk = k * kk_ref[...]
    norm = jnp.sqrt(_head_sum(kk * kk, _head_ones()))
    kkn_ref[...] = (kk / jnp.maximum(norm, 1e-12)).astype(kkn_ref.dtype)
    r_ref[...] = r.astype(r_ref.dtype)
    k_ref[...] = (k * (1.0 + (a - 1.0) * ka_ref[...])).astype(k_ref.dtype)
    v_ref[...] = v.astype(v_ref.dtype)
    a_ref[...] = a.astype(a_ref.dtype)


WKV_GROUP_HEADS = LANES // HEAD_SIZE
WKV_GROUPS = 16
INV_BLOCK = 16


def _mm(a, b):
    return jnp.dot(a.astype(BF16), b.astype(BF16), preferred_element_type=F32)


def _mm_nt(a, b):
    return lax.dot_general(a.astype(BF16), b.astype(BF16), (((1,), (1,)), ((), ())), preferred_element_type=F32)


def _mm_tn(a, b):
    return lax.dot_general(a.astype(BF16), b.astype(BF16), (((0,), (0,)), ((), ())), preferred_element_type=F32)


def _wkv_kernel(r_ref, k_ref, v_ref, kk_ref, a_ref, lw_ref, g_ref, rk_ref, lng_ref, lnb_ref, s_ref,
                z_ref, so_ref, st_scr, *, NG, GH):
    c = pl.program_id(2)
    nc = pl.num_programs(2)
    C = r_ref.shape[0]
    GW = GH * C
    assert C == HEAD_SIZE

    @pl.when(c == 0)
    def _():
        for gi in range(NG):
            st_scr[gi] = jnp.concatenate([s_ref[0, GH * gi + h] for h in range(GH)], axis=1)

    def ld(ref, sl):
        return ref[:, sl].astype(F32)

    row = lax.broadcasted_iota(jnp.int32, (C, GW), 0)
    lane = lax.broadcasted_iota(jnp.int32, (C, GW), 1)
    sidx = lane % C
    head = lane // C
    strict = sidx < row
    incl = sidx <= row
    eye = (sidx == row).astype(F32)
    ones = _head_ones(GW)

    def bd(x):
        z = jnp.zeros_like(x)
        return jnp.concatenate([jnp.where(head == h, x, z) for h in range(GH)], axis=0)

    def diag_blocks(zz):
        out = zz[:C]
        for h in range(1, GH):
            out = jnp.where(head == h, zz[h * C:(h + 1) * C], out)
        return out

    lw = lw_ref[...]
    rows_full = lax.broadcasted_iota(jnp.int32, lw.shape, 0)
    lc = lw
    sh = 1
    while sh < C:
        lc = lc + jnp.where(rows_full >= sh, pltpu.roll(lc, sh, axis=0), 0.0)
        sh *= 2

    groups = range(NG)
    sls = [slice(gi * GW, (gi + 1) * GW) for gi in groups]
    lhs, bh_kh, v_bd, ltot, sp = [], [], [], [], []
    gm = []
    for sl in sls:
        lwp, lp = lw[:, sl], lc[:, sl]
        lt = lp[C - 1:C, :]
        kp, kkp = ld(k_ref, sl), ld(kk_ref, sl)
        bp = kkp * ld(a_ref, sl)
        e_neg = jnp.exp(-lp)
        e_rem = jnp.exp(lt - lp)
        at = -(kkp * jnp.exp(lp - lwp))
        rt = ld(r_ref, sl) * jnp.exp(lp)
        lhs.append(jnp.concatenate([at, rt], axis=0).astype(BF16))
        gm.append(_mm_nt(lhs[-1], jnp.concatenate([bd(bp * e_neg), bd(kp * e_neg)], axis=0)))
        bh_kh.append(jnp.concatenate([bp * e_rem, kp * e_rem], axis=0).astype(BF16))
        v_bd.append(bd(ld(v_ref, sl)).astype(BF16))
        ltot.append(lt)
    n_ab = [jnp.where(strict, g[:C, :GW], 0.0) for g in gm]
    a_rb = [jnp.where(incl, g[C:, :GW], 0.0) for g in gm]
    a_k = [jnp.concatenate([jnp.where(strict, g[:C, GW:], 0.0), jnp.where(incl, g[C:, GW:], 0.0)], axis=0)
           for g in gm]
    for pp in groups:
        sp.append(st_scr[pp])
    s_prod = [_mm_nt(lhs[pp], bd(sp[pp])) for pp in groups]
    av = [_mm(a_k[pp], v_bd[pp]) for pp in groups]
    blk = [(row // (INV_BLOCK << lvl)) == (sidx // (INV_BLOCK << lvl)) for lvl in range(3)]
    assert INV_BLOCK << 2 == C
    pw = [jnp.where(blk[0], m, 0.0) for m in n_ab]
    tm = [eye + m for m in pw]
    pw_bd = [bd(m).astype(BF16) for m in pw]
    n = 2
    while n < INV_BLOCK:
        pw = [_mm(pw[pp], pw_bd[pp]) for pp in groups]
        pw_bd = [bd(m).astype(BF16) for m in pw]
        tm = [tm[pp] + _mm(tm[pp], pw_bd[pp]) for pp in groups]
        n *= 2
    for lvl in (1, 2):
        off = jnp.logical_and(blk[lvl], jnp.logical_not(blk[lvl - 1]))
        t_bd = [bd(m).astype(BF16) for m in tm]
        tl = [_mm(tm[pp], bd(jnp.where(off, n_ab[pp], 0.0))) for pp in groups]
        tm = [tm[pp] + _mm(tl[pp], t_bd[pp]) for pp in groups]
    u = [_mm(tm[pp], bd(s_prod[pp][:C] + av[pp][:C])) for pp in groups]
    y = [s_prod[pp][C:] + av[pp][C:] + _mm(a_rb[pp], bd(u[pp])) for pp in groups]
    for pp in groups:
        zz = _mm_tn(jnp.concatenate([u[pp], ld(v_ref, sls[pp])], axis=0), bh_kh[pp])
        st_scr[pp] = sp[pp] * jnp.exp(ltot[pp]) + diag_blocks(zz)
    def head_sums(parts):
        stacked = jnp.concatenate(parts, axis=0) if NG > 1 else parts[0]
        m = _mm(stacked, ones)
        return [m[pp * C:(pp + 1) * C] for pp in groups]

    mu = head_sums([t * (1.0 / HEAD_SIZE) for t in y])
    yc = [y[pp] - mu[pp] for pp in groups]
    var = head_sums([t * t * (1.0 / HEAD_SIZE) for t in yc])
    rk_sum = head_sums([ld(r_ref, sl) * ld(k_ref, sl) * rk_ref[:, sl] for sl in sls])
    for pp, sl in enumerate(sls):
        yn = yc[pp] * lax.rsqrt(var[pp] + GN_EPS) * lng_ref[:, sl] + lnb_ref[:, sl]
        z_ref[:, sl] = ((yn + rk_sum[pp] * ld(v_ref, sl)) * ld(g_ref, sl)).astype(z_ref.dtype)

    @pl.when(c == nc - 1)
    def _():
        for gi in range(NG):
            for h in range(GH):
                so_ref[0, GH * gi + h] = st_scr[gi, :, h * C:(h + 1) * C]


def _wkv_call(rkvkal, gate, r_k, ln_g, ln_b, state, *, seq_len, layer):
    T, D = gate.shape
    B = T // seq_len
    C = WKV_CHUNK
    NC = seq_len // C
    GH = min(WKV_GROUP_HEADS, D // HEAD_SIZE)
    GW = GH * HEAD_SIZE
    NG = min(WKV_GROUPS, D // GW)
    Q = D // (NG * GW)
    assert seq_len % C == 0 and D % (NG * GW) == 0 and GW % LANES == 0
    tile = pl.BlockSpec((C, NG * GW), lambda b, q, c: (b * NC + c, q))
    row = pl.BlockSpec((1, NG * GW), lambda b, q, c: (0, q))
    st = pl.BlockSpec((1, NG * GH, HEAD_SIZE, HEAD_SIZE), lambda b, q, c: (b, q, 0, 0))
    st_in = pl.BlockSpec((None, 1, NG * GH, HEAD_SIZE, HEAD_SIZE), lambda b, q, c: (layer, b, q, 0, 0))
    return pl.pallas_call(
        functools.partial(_wkv_kernel, NG=NG, GH=GH),
        grid=(B, Q, NC),
        in_specs=[tile] * 7 + [row] * 3 + [st_in],
        out_specs=[tile, st],
        out_shape=[jax.ShapeDtypeStruct((T, D), BF16), jax.ShapeDtypeStruct(state.shape[1:], F32)],
        scratch_shapes=[pltpu.VMEM((NG, HEAD_SIZE, GW), F32)],
        compiler_params=_params("arbitrary", "arbitrary", "arbitrary"),
        name="rwkv_wkv",
    )(*rkvkal, gate, _row(r_k.reshape(-1)), _row(ln_g), _row(ln_b), state)


def _pad_rank(w, axis):
    pad = [(0, 0)] * w.ndim
    pad[axis] = (0, -w.shape[axis] % LORA_PAD)
    return jnp.pad(w, pad)


def _rwkv_in_call(x, g, p, shift, v_first, *, seq_len, layer):
    T, D = x.shape
    carry = shift is None
    vres = v_first is not None
    vl = layer - 1
    TM, L, S = _tiling(T, seq_len, carry, RWKV_IN_TILE)
    tn = min(RWKV_IN_COLS, D)
    nj = D // tn
    row_d = pl.BlockSpec((1, D), lambda i, j: (0, 0))
    row_n = pl.BlockSpec((1, tn), lambda i, j: (0, j))
    tile_n = pl.BlockSpec((TM, tn), lambda i, j: (i, j))

    def col_w(lyr):
        return pl.BlockSpec((None, D, tn), lambda i, j: (lyr, 0, j))

    def lora1(a, lyr):
        return pl.BlockSpec((None,) + a.shape[1:], lambda i, j: (lyr, 0, 0))

    def lora2(a, lyr):
        return pl.BlockSpec((None, a.shape[1], tn), lambda i, j: (lyr, 0, j))

    in_specs = [pl.BlockSpec((TM, D), lambda i, j: (i, 0)), row_d, pl.BlockSpec((6, D), lambda i, j: (0, 0)),
                col_w(layer), col_w(layer), col_w(layer),
                lora1(p['w1'], layer), lora1(p['a1'], layer), lora1(p['g1'], layer),
                lora2(p['w2'], layer), lora2(p['a2'], layer), lora2(p['g2'], layer),
                row_n, row_n, row_n, row_n]
    args = [x, _row(g), p['mix'][layer], p['w_r'], p['w_k'], p['w_v'], p['w1'], p['a1'], p['g1'],
            p['w2'], p['a2'], p['g2'],
            _row(p['w0'][layer]), _row(p['a0'][layer]), _row(p['k_k'][layer]), _row(p['k_a'][layer])]
    if vres:
        in_specs += [lora1(p['v1'], vl), lora2(p['v2'], vl), row_n, tile_n]
        args += [p['v1'], p['v2'], _row(p['v0'][vl]), v_first]
    if not carry:
        in_specs.append(pl.BlockSpec((S, 1, D), lambda i, j: (i, 0, 0)))
        args.append(shift.reshape(-1, 1, D))
    nseg = (T // TM) * S
    out_specs = [tile_n] * 7 + [pl.BlockSpec((S, 1, D), lambda i, j: (i, 0, 0))]
    out_shape = ([jax.ShapeDtypeStruct((T, D), BF16)] * 5 + [jax.ShapeDtypeStruct((T, D), F32)]
                 + [jax.ShapeDtypeStruct((T, D), BF16), jax.ShapeDtypeStruct((nseg, 1, D), F32)])
    scratch = [pltpu.VMEM((3, TM, D), BF16), pltpu.VMEM((TM, p['w1'].shape[2]), BF16),
               pltpu.VMEM((TM, p['a1'].shape[2]), BF16), pltpu.VMEM((TM, p['g1'].shape[2]), BF16)]
    if vres:
        scratch.append(pltpu.VMEM((TM, p['v1'].shape[2]), BF16))
    if carry:
        scratch.append(pltpu.VMEM((1, D), F32))
    outs = pl.pallas_call(
        functools.partial(_rwkv_in_kernel, S=S, L=L, carry=carry, vres=vres),
        grid=(T // TM, nj),
        in_specs=in_specs,
        out_specs=out_specs,
        out_shape=out_shape,
        scratch_shapes=scratch,
        compiler_params=_params("arbitrary", "arbitrary"),
        name="rwkv_in_carry" if carry else "rwkv_in_state",
    )(*args)
    r, k, v, kk, a, lw, gate, hl = outs
    h_last = hl[-1:, 0, :] if carry else hl[:, 0, :]
    return (r, k, v, kk, a, lw), gate, h_last


def _proj_call(z, x, w, g, *, layer, ln_g=None, ln_b=None, b=None):
    T, D = x.shape
    TM = min(PROJ_TILE, T)
    assert T % TM == 0
    row = pl.BlockSpec((1, D), lambda i: (0, 0))
    tile = pl.BlockSpec((TM, D), lambda i: (i, 0))
    in_specs = [tile, tile, pl.BlockSpec((None, D, D), lambda i: (layer, 0, 0)), row]
    args = [z, x, w, _row(g)]
    if ln_g is not None:
        in_specs += [row, row]
        args += [_row(ln_g), _row(ln_b)]
    if b is not None:
        in_specs.append(row)
        args.append(_row(b))
    return pl.pallas_call(
        functools.partial(_proj_kernel, ln=ln_g is not None, bias=b is not None),
        grid=(T // TM,),
        in_specs=in_specs,
        out_specs=tile,
        out_shape=jax.ShapeDtypeStruct((T, D), F32),
        compiler_params=_params("arbitrary"),
        name="proj_ln" if ln_g is not None else "proj",
    )(*args)


def _trunk(x3, conv_st, shift_st, wkv_st, ffn_st, p):
    B, T, D = x3.shape
    x = x3.reshape(B * T, D)
    depth = p['norm_g'].shape[0]
    new_conv, new_shift, new_wkv, new_ffn = [], [], [], []
    v_first = None
    for i in range(depth):
        g = p['norm_g'][i]
        j = i // 2
        if i % 2 == 0:
            u = _glu_call(x, g[0], p['conv_pw1_w'], p['conv_pw1_b'][j], layer=j)
            c = _dwconv_call(u, p['conv_dw_w'][j], p['conv_dw_b'][j], conv_st, seq_len=T, layer=j)
            new_conv.append(u.reshape(B, T, D)[:, T - (CONV_WIDTH - 1):])
            x = _proj_call(c, x, p['conv_pw2_w'], g[1], layer=j, ln_g=p['conv_ln_g'][j], ln_b=p['conv_ln_b'][j],
                           b=p['conv_pw2_b'][j])
        else:
            q = {k[5:]: v for k, v in p.items() if k.startswith('rwkv_')}
            rkvkal, gate, h_last = _rwkv_in_call(x, g[0], q, None if shift_st is None else shift_st[j],
                                                 v_first if j > 0 else None, seq_len=T, layer=j)
            if j == 0:
                v_first = rkvkal[2]
            if wkv_st is None:
                s0, sl = jnp.zeros((1, B, D // HEAD_SIZE, HEAD_SIZE, HEAD_SIZE), F32), 0
            else:
                s0, sl = wkv_st, j
            z, s_new = _wkv_call(rkvkal, gate, q['r_k'][j], q['ln_g'][j], q['ln_b'][j], s0, seq_len=T, layer=sl)
            new_shift.append(h_last)
            new_wkv.append(s_new)
            x = _proj_call(z, x, q['w_o'], g[1], layer=j)
        x, nb = _ffn_call(x, g[2], g[3], p['ffn_w_in'], p['ffn_dw_w'][i], p['ffn_dw_b'][i], p['ffn_w_out'],
                          ffn_st, seq_len=T, layer=i)
        new_ffn.append(nb)
    return (x.reshape(B, T, D), jnp.stack(new_conv), jnp.stack(new_shift), jnp.stack(new_wkv), jnp.stack(new_ffn))


def kernel(x_prompt, x_sample, state_conv_mix, state_rwkv_shift, state_rwkv_wkv, state_ffn_conv, norm_g,
           conv_pw1_w, conv_pw1_b, conv_dw_w, conv_dw_b, conv_ln_g, conv_ln_b, conv_pw2_w, conv_pw2_b,
           rwkv_mix, rwkv_w_r, rwkv_w_k, rwkv_w_v, rwkv_w_o, rwkv_w0, rwkv_w1, rwkv_w2, rwkv_a0, rwkv_a1,
           rwkv_a2, rwkv_v0, rwkv_v1, rwkv_v2, rwkv_g1, rwkv_g2, rwkv_k_k, rwkv_k_a, rwkv_r_k, rwkv_ln_g,
           rwkv_ln_b, ffn_w_in, ffn_dw_w, ffn_dw_b, ffn_w_out):
    bf = lambda w: w.astype(BF16)
    pad_c = lambda w: _pad_rank(w, 2).astype(BF16)
    pad_r = lambda w: _pad_rank(w, 1).astype(BF16)
    p = dict(norm_g=norm_g, conv_pw1_w=bf(conv_pw1_w), conv_pw1_b=conv_pw1_b, conv_dw_w=conv_dw_w,
             conv_dw_b=conv_dw_b, conv_ln_g=conv_ln_g, conv_ln_b=conv_ln_b, conv_pw2_w=bf(conv_pw2_w),
             conv_pw2_b=conv_pw2_b, rwkv_mix=rwkv_mix, rwkv_w_r=bf(rwkv_w_r), rwkv_w_k=bf(rwkv_w_k),
             rwkv_w_v=bf(rwkv_w_v), rwkv_w_o=bf(rwkv_w_o), rwkv_w0=rwkv_w0, rwkv_w1=pad_c(rwkv_w1),
             rwkv_w2=pad_r(rwkv_w2), rwkv_a0=rwkv_a0, rwkv_a1=pad_c(rwkv_a1), rwkv_a2=pad_r(rwkv_a2),
             rwkv_v0=rwkv_v0, rwkv_v1=pad_c(rwkv_v1), rwkv_v2=pad_r(rwkv_v2), rwkv_g1=pad_c(rwkv_g1),
             rwkv_g2=pad_r(rwkv_g2), rwkv_k_k=rwkv_k_k, rwkv_k_a=rwkv_k_a, rwkv_r_k=rwkv_r_k,
             rwkv_ln_g=rwkv_ln_g, rwkv_ln_b=rwkv_ln_b, ffn_w_in=bf(ffn_w_in), ffn_dw_w=ffn_dw_w,
             ffn_dw_b=ffn_dw_b, ffn_w_out=bf(ffn_w_out))
    y_p, conv_p, shift_p, wkv_p, ffn_p = _trunk(x_prompt, None, None, None, None, p)
    y_s, conv_s, shift_s, wkv_s, ffn_s = _trunk(x_sample, state_conv_mix, state_rwkv_shift, state_rwkv_wkv,
                                                state_ffn_conv, p)
    return (y_p, y_s, conv_p, conv_s, shift_p, shift_s, wkv_p, wkv_s, ffn_p, ffn_s)
```

```python
import functools

import jax
import jax.numpy as jnp
from jax import lax
from jax.experimental import pallas as pl
from jax.experimental.pallas import tpu as pltpu

F32 = jnp.float32
BF16 = jnp.bfloat16

RMS_EPS = 1e-6
LN_EPS = 1e-5
GN_EPS = 64e-5
HEAD_SIZE = 64
CONV_WIDTH = 31
FFN_CONV_WIDTH = 3

LANES = 128
SUBLANES = 8
VMEM_LIMIT_BYTES = 56 * 1024 * 1024

TOKEN_TILE = 512
WKV_CHUNK = 64


def _params(*semantics):
    return pltpu.CompilerParams(dimension_semantics=semantics, vmem_limit_bytes=VMEM_LIMIT_BYTES)


def _rmsnorm(x, g):
    return x * lax.rsqrt(jnp.mean(x * x, axis=-1, keepdims=True) + RMS_EPS) * g


def _row(v):
    return v.reshape(1, -1)


GELU_C = 0.7978845608028654
GELU_K = 0.044715


def _gelu_tanh_times(x, u):
    inner = x * (GELU_C + (GELU_C * GELU_K) * (x * x))
    return (x * u) * (0.5 + 0.5 * jnp.tanh(inner))


def _ffn_kernel(*refs, S, L, carry):
    if carry:
        (x_ref, g2_ref, g3_ref, wu_ref, wg_ref, dww_ref, dwb_ref, wo_ref,
         o_ref, nb_ref, h_scr, gpx_scr, carry_scr) = refs
    else:
        (x_ref, g2_ref, g3_ref, wu_ref, wg_ref, dww_ref, dwb_ref, wo_ref, halo_ref,
         o_ref, nb_ref, h_scr, gpx_scr) = refs
    i = pl.program_id(0)
    j = pl.program_id(1)
    nj = pl.num_programs(1)
    K = FFN_CONV_WIDTH - 1
    tf = wu_ref.shape[-1]

    @pl.when(j == 0)
    def _():
        h_scr[...] = _rmsnorm(x_ref[...], g2_ref[...]).astype(BF16)
        o_ref[...] = jnp.zeros(o_ref.shape, F32)

    if carry:
        @pl.when(i == 0)
        def _():
            carry_scr[j] = jnp.zeros((SUBLANES, tf), F32)

    h = h_scr[...]
    w = dww_ref[...]
    nsub = 2 if tf % (2 * LANES) == 0 else 1
    ts = tf // nsub
    cols = [slice(s * ts, (s + 1) * ts) for s in range(nsub)]
    up = [(jnp.dot(h, wu_ref[:, cs], preferred_element_type=F32),
           jnp.dot(h, wg_ref[:, cs], preferred_element_type=F32)) for cs in cols]
    contrib = None
    for cs, (u, gp) in zip(cols, up):
        gp3 = gp.reshape(S, L, ts)
        gpx_scr[:, SUBLANES:SUBLANES + L, cs] = gp3
        if carry:
            gpx_scr[0, 0:SUBLANES, cs] = carry_scr[j, :, cs]
        else:
            gpx_scr[:, SUBLANES - K:SUBLANES, cs] = halo_ref[:, :, cs]
        gc = (gpx_scr[:, SUBLANES - 2:SUBLANES - 2 + L, cs] * w[0:1, cs]
              + gpx_scr[:, SUBLANES - 1:SUBLANES - 1 + L, cs] * w[1:2, cs]
              + gp3 * w[2:3, cs] + dwb_ref[:, cs])
        act = _gelu_tanh_times(gc, u.reshape(S, L, ts)).reshape(S * L, ts).astype(BF16)
        part = jnp.dot(act, wo_ref[cs, :], preferred_element_type=F32)
        contrib = part if contrib is None else contrib + part
    o_ref[...] += contrib

    nb_ref[...] = gpx_scr[:, SUBLANES + L - K:SUBLANES + L, :]
    if carry:
        carry_scr[j] = gpx_scr[0, L:L + SUBLANES, :]

    @pl.when(j == nj - 1)
    def _():
        o_ref[...] = x_ref[...] + _rmsnorm(o_ref[...], g3_ref[...])


def _ffn_chunk(d_ff):
    for tf in (512, 256, 128):
        if d_ff % tf == 0:
            return tf
    raise ValueError(f"d_ff={d_ff} is not a multiple of {LANES}")


def _ffn_call(x, g2, g3, w_in, dw_w, dw_b, w_out, halo, *, seq_len, layer):
    T, D = x.shape
    d_ff = w_out.shape[1]
    tf = _ffn_chunk(d_ff)
    nj = d_ff // tf
    carry = halo is None
    TM = min(TOKEN_TILE, T)
    L = TM if carry else seq_len
    S = TM // L
    assert T % TM == 0 and TM % L == 0 and (carry or L == seq_len)
    nt = T // TM
    in_specs = [
        pl.BlockSpec((TM, D), lambda i, j: (i, 0)),
        pl.BlockSpec((1, D), lambda i, j: (0, 0)),
        pl.BlockSpec((1, D), lambda i, j: (0, 0)),
        pl.BlockSpec((None, D, tf), lambda i, j: (layer, 0, j)),
        pl.BlockSpec((None, D, tf), lambda i, j: (layer, 0, nj + j)),
        pl.BlockSpec((FFN_CONV_WIDTH, tf), lambda i, j: (0, j)),
        pl.BlockSpec((1, tf), lambda i, j: (0, j)),
        pl.BlockSpec((None, tf, D), lambda i, j: (layer, j, 0)),
    ]
    args = [x, _row(g2), _row(g3), w_in, w_in, dw_w, _row(dw_b), w_out]
    scratch = [pltpu.VMEM((TM, D), BF16), pltpu.VMEM((S, SUBLANES + L, tf), F32)]
    nb_spec = pl.BlockSpec((S, FFN_CONV_WIDTH - 1, tf), lambda i, j: (i, 0, j))
    if carry:
        scratch.append(pltpu.VMEM((nj, SUBLANES, tf), F32))
    else:
        in_specs.append(pl.BlockSpec((None, S, FFN_CONV_WIDTH - 1, tf), lambda i, j: (layer, i, 0, j)))
        args.append(halo)
    out, nb = pl.pallas_call(
        functools.partial(_ffn_kernel, S=S, L=L, carry=carry),
        grid=(nt, nj),
        in_specs=in_specs,
        out_specs=[pl.BlockSpec((TM, D), lambda i, j: (i, 0)), nb_spec],
        out_shape=[jax.ShapeDtypeStruct((T, D), F32),
                   jax.ShapeDtypeStruct((nt * S, FFN_CONV_WIDTH - 1, d_ff), F32)],
        scratch_shapes=scratch,
        compiler_params=_params("arbitrary", "arbitrary"),
        name="ffn_carry" if carry else "ffn_state",
    )(*args)
    return out, (nb[-1:] if carry else nb)


def _tiling(T, seq_len, carry, tile=TOKEN_TILE):
    TM = min(tile, T)
    L = TM if carry else seq_len
    assert T % TM == 0 and TM % L == 0 and L % SUBLANES == 0
    return TM, L, TM // L


def _glu_kernel(x_ref, g_ref, wa_ref, wb_ref, ba_ref, bb_ref, u_ref, h_scr):
    @pl.when(pl.program_id(1) == 0)
    def _():
        h_scr[...] = _rmsnorm(x_ref[...], g_ref[...]).astype(BF16)

    h = h_scr[...]
    tn = wa_ref.shape[-1]
    ts = 2 * LANES if tn % (2 * LANES) == 0 else tn
    cols = [slice(c0, c0 + ts) for c0 in range(0, tn, ts)]
    ab = [(jnp.dot(h, wa_ref[:, cs], preferred_element_type=F32),
           jnp.dot(h, wb_ref[:, cs], preferred_element_type=F32)) for cs in cols]
    for cs, (a, b) in zip(cols, ab):
        u_ref[:, cs] = (a + ba_ref[:, cs]) * jax.nn.sigmoid(b + bb_ref[:, cs])


def _glu_call(x, g, w1, b1, *, layer):
    T, D = x.shape
    TM = min(TOKEN_TILE, T)
    tn = min(GLU_COLS, D)
    nj = D // tn
    assert T % TM == 0 and D % tn == 0
    return pl.pallas_call(
        _glu_kernel,
        grid=(T // TM, nj),
        in_specs=[
            pl.BlockSpec((TM, D), lambda i, j: (i, 0)),
            pl.BlockSpec((1, D), lambda i, j: (0, 0)),
            pl.BlockSpec((None, D, tn), lambda i, j: (layer, 0, j)),
            pl.BlockSpec((None, D, tn), lambda i, j: (layer, 0, nj + j)),
            pl.BlockSpec((1, tn), lambda i, j: (0, j)),
            pl.BlockSpec((1, tn), lambda i, j: (0, nj + j)),
        ],
        out_specs=pl.BlockSpec((TM, tn), lambda i, j: (i, j)),
        out_shape=jax.ShapeDtypeStruct((T, D), F32),
        scratch_shapes=[pltpu.VMEM((TM, D), BF16)],
        compiler_params=_params("arbitrary", "arbitrary"),
        name="conv_glu",
    )(x, _row(g), w1, w1, _row(b1), _row(b1))


HIST_PAD = 32
DW_ROWS = 64
DW_COLS = 256
GLU_COLS = 1024


def _dwconv_kernel(*refs, S, L, carry):
    if carry:
        u_ref, w_ref, b_ref, c_ref, ux_scr, sh_scr, carry_scr = refs
    else:
        u_ref, w_ref, b_ref, st_ref, c_ref, ux_scr, sh_scr = refs
    i = pl.program_id(0)
    j = pl.program_id(1)
    tc = u_ref.shape[-1]
    K = CONV_WIDTH - 1
    ux_scr[:, HIST_PAD:HIST_PAD + L, :] = u_ref[...].reshape(S, L, tc)
    if carry:
        @pl.when(i == 0)
        def _():
            carry_scr[j] = jnp.zeros((HIST_PAD, tc), F32)
        ux_scr[0, 0:HIST_PAD, :] = carry_scr[j]
    else:
        ux_scr[:, HIST_PAD - K:HIST_PAD, :] = st_ref[...]
    nr = HIST_PAD + L - SUBLANES
    for r in range(1, SUBLANES):
        sh_scr[r - 1] = ux_scr[:, r:r + nr, :]
    w = w_ref[...]
    rb = min(DW_ROWS, L)
    for s in range(S):
        for r0 in range(0, L, rb):
            acc = jnp.broadcast_to(b_ref[...], (rb, tc))
            for k in range(CONV_WIDTH):
                m = HIST_PAD - K + k
                r, lo = m % SUBLANES, r0 + m - m % SUBLANES
                tap = ux_scr[s, lo:lo + rb, :] if r == 0 else sh_scr[r - 1, s, lo:lo + rb, :]
                acc = acc + tap * w[k:k + 1, :]
            c_ref[s * L + r0:s * L + r0 + rb, :] = acc
    if carry:
        carry_scr[j] = ux_scr[0, L:L + HIST_PAD, :]


def _dwconv_call(u, w, b, state, *, seq_len, layer):
    T, D = u.shape
    carry = state is None
    TM, L, S = _tiling(T, seq_len, carry)
    tc = min(DW_COLS, D)
    nc = D // tc
    in_specs = [
        pl.BlockSpec((TM, tc), lambda i, j: (i, j)),
        pl.BlockSpec((CONV_WIDTH, tc), lambda i, j: (0, j)),
        pl.BlockSpec((1, tc), lambda i, j: (0, j)),
    ]
    args = [u, w, _row(b)]
    scratch = [pltpu.VMEM((S, HIST_PAD + L, tc), F32),
               pltpu.VMEM((SUBLANES - 1, S, HIST_PAD + L - SUBLANES, tc), F32)]
    if carry:
        scratch.append(pltpu.VMEM((nc, HIST_PAD, tc), F32))
    else:
        in_specs.append(pl.BlockSpec((None, S, CONV_WIDTH - 1, tc), lambda i, j: (layer, i, 0, j)))
        args.append(state)
    return pl.pallas_call(
        functools.partial(_dwconv_kernel, S=S, L=L, carry=carry),
        grid=(T // TM, nc),
        in_specs=in_specs,
        out_specs=pl.BlockSpec((TM, tc), lambda i, j: (i, j)),
        out_shape=jax.ShapeDtypeStruct((T, D), F32),
        scratch_shapes=scratch,
        compiler_params=_params("arbitrary", "arbitrary"),
        name="conv_dw_carry" if carry else "conv_dw_state",
    )(*args)


def _proj_kernel(*refs, ln, bias):
    refs = list(refs)
    z_ref, x_ref, w_ref, g_ref = refs[:4]
    rest = refs[4:]
    z = z_ref[...].astype(F32)
    if ln:
        lg_ref, lb_ref = rest[:2]
        rest = rest[2:]
        mu = jnp.mean(z, axis=-1, keepdims=True)
        zc = z - mu
        var = jnp.mean(zc * zc, axis=-1, keepdims=True)
        z = zc * lax.rsqrt(var + LN_EPS) * lg_ref[...] + lb_ref[...]
        z = z * jax.nn.sigmoid(z)
    out = jnp.dot(z.astype(BF16), w_ref[...], preferred_element_type=F32)
    if bias:
        out = out + rest[0][...]
        rest = rest[1:]
    o_ref, = rest
    o_ref[...] = x_ref[...] + _rmsnorm(out, g_ref[...])


PROJ_TILE = 512


def _head_ones(width=LANES):
    r = lax.broadcasted_iota(jnp.int32, (width, width), 0) // HEAD_SIZE
    c = lax.broadcasted_iota(jnp.int32, (width, width), 1) // HEAD_SIZE
    return (r == c).astype(BF16)


def _head_sum(x, ones):
    outs = []
    for p in range(x.shape[1] // LANES):
        xp = x[:, p * LANES:(p + 1) * LANES]
        hi = xp.astype(BF16)
        lo = (xp - hi.astype(F32)).astype(BF16)
        outs.append(jnp.dot(hi, ones, preferred_element_type=F32) + jnp.dot(lo, ones, preferred_element_type=F32))
    return outs[0] if len(outs) == 1 else jnp.concatenate(outs, axis=1)


LORA_PAD = 128
RWKV_IN_TILE = 512
RWKV_IN_COLS = 512


def _softplus(z):
    return jnp.maximum(z, 0.0) + jnp.log(1.0 + jnp.exp(-jnp.abs(z)))


MIX_CHUNK = 512


def _rwkv_in_kernel(*refs, S, L, carry, vres):
    refs = list(refs)
    x_ref, g_ref, mix_ref, wr_ref, wk_ref, wv_ref, w1_ref, a1_ref, g1_ref = refs[:9]
    w2_ref, a2_ref, g2_ref, w0_ref, a0_ref, kk_ref, ka_ref = refs[9:16]
    rest = refs[16:]
    if vres:
        v1_ref, v2_ref, v0_ref, vf_ref = rest[:4]
        rest = rest[4:]
    if not carry:
        sh_ref = rest[0]
        rest = rest[1:]
    r_ref, k_ref, v_ref, kkn_ref, a_ref, lw_ref, go_ref, hl_ref = rest[:8]
    rest = rest[8:]
    mx_scr, tw_scr, ta_scr, tg_scr = rest[:4]
    rest = rest[4:]
    if vres:
        tv_scr = rest[0]
        rest = rest[1:]
    if carry:
        carry_scr, = rest
    i = pl.program_id(0)
    j = pl.program_id(1)
    TM, D = x_ref.shape

    @pl.when(j == 0)
    def _():
        x = x_ref[...]
        rs = lax.rsqrt(jnp.mean(x * x, axis=-1, keepdims=True) + RMS_EPS)
        if carry:
            @pl.when(i == 0)
            def _():
                carry_scr[...] = jnp.zeros((1, D), F32)
        mc = min(MIX_CHUNK, D)
        seg_start = lax.broadcasted_iota(jnp.int32, (TM, mc), 0) % L == 0
        lora = {}

        def acc(name, xm, w_ref, cs):
            part = jnp.dot(xm, w_ref[cs, :], preferred_element_type=F32)
            lora[name] = part if name not in lora else lora[name] + part

        for c0 in range(0, D, mc):
            cs = slice(c0, c0 + mc)
            h = x_ref[:, cs] * rs * g_ref[:, cs]
            if carry:
                first = jnp.broadcast_to(carry_scr[:, cs], (TM, mc))
                carry_scr[:, cs] = h[TM - 1:TM, :]
            else:
                first = jnp.broadcast_to(sh_ref[:, :, cs], (S, L, mc)).reshape(TM, mc)
            for s in range(S):
                hl_ref[s, :, cs] = h[s * L + L - 1:s * L + L, :]
            xx = jnp.where(seg_start, first, pltpu.roll(h, 1, axis=0)) - h
            mixed = [(h + xx * mix_ref[m:m + 1, cs]).astype(BF16) for m in range(6)]
            mx_scr[0, :, cs] = mixed[0]
            mx_scr[1, :, cs] = mixed[2]
            mx_scr[2, :, cs] = mixed[3]
            acc('w', mixed[1], w1_ref, cs)
            acc('a', mixed[4], a1_ref, cs)
            acc('g', mixed[5], g1_ref, cs)
            if vres:
                acc('v', mixed[3], v1_ref, cs)
        tw_scr[...] = jnp.tanh(lora['w']).astype(BF16)
        ta_scr[...] = lora['a'].astype(BF16)
        tg_scr[...] = jax.nn.sigmoid(lora['g']).astype(BF16)
        if vres:
            tv_scr[...] = lora['v'].astype(BF16)

    tn = wr_ref.shape[-1]
    ts = 2 * LANES if tn % (2 * LANES) == 0 else tn
    ones = _head_ones()
    for c0 in range(0, tn, ts):
        cs = slice(c0, c0 + ts)
        r = jnp.dot(mx_scr[0], wr_ref[:, cs], preferred_element_type=F32)
        k = jnp.dot(mx_scr[1], wk_ref[:, cs], preferred_element_type=F32)
        v = jnp.dot(mx_scr[2], wv_ref[:, cs], preferred_element_type=F32)
        wl = jnp.dot(tw_scr[...], w2_ref[:, cs], preferred_element_type=F32)
        al = jnp.dot(ta_scr[...], a2_ref[:, cs], preferred_element_type=F32)
        go_ref[:, cs] = jnp.dot(tg_scr[...], g2_ref[:, cs], preferred_element_type=F32).astype(go_ref.dtype)
        w = -_softplus(-(w0_ref[:, cs] + wl)) - 0.5
        lw_ref[:, cs] = -jnp.exp(w)
        a = jax.nn.sigmoid(a0_ref[:, cs] + al)
        if vres:
            vl = jnp.dot(tv_scr[...], v2_ref[:, cs], preferred_element_type=F32)
            v = v + (vf_ref[:, cs].astype(F32) - v) * jax.nn.sigmoid(v0_ref[:, cs] + vl)
        kk = k * kk_ref[:, cs]
        norm = jnp.sqrt(_head_sum(kk * kk, ones))
        kkn_ref[:, cs] = (kk / jnp.maximum(norm, 1e-12)).astype(kkn_ref.dtype)
        r_ref[:, cs] = r.astype(r_ref.dtype)
        k_ref[:, cs] = (k * (1.0 + (a - 1.0) * ka_ref[:, cs])).astype(k_ref.dtype)
        v_ref[:, cs] = v.astype(v_ref.dtype)
        a_ref[:, cs] = a.astype(a_ref.dtype)


WKV_GROUP_HEADS = LANES // HEAD_SIZE
WKV_GROUPS = 16
INV_BLOCK = 16


def _mm(a, b):
    return jnp.dot(a.astype(BF16), b.astype(BF16), preferred_element_type=F32)


def _mm_nt(a, b):
    return lax.dot_general(a.astype(BF16), b.astype(BF16), (((1,), (1,)), ((), ())), preferred_element_type=F32)


def _mm_tn(a, b):
    return lax.dot_general(a.astype(BF16), b.astype(BF16), (((0,), (0,)), ((), ())), preferred_element_type=F32)


def _wkv_kernel(r_ref, k_ref, v_ref, kk_ref, a_ref, lw_ref, g_ref, rk_ref, lng_ref, lnb_ref, s_ref,
                z_ref, so_ref, st_scr, *, NG, GH):
    c = pl.program_id(2)
    nc = pl.num_programs(2)
    C = r_ref.shape[0]
    GW = GH * C
    assert C == HEAD_SIZE

    @pl.when(c == 0)
    def _():
        for gi in range(NG):
            st_scr[gi] = jnp.concatenate([s_ref[0, GH * gi + h] for h in range(GH)], axis=1)

    def ld(ref, sl):
        return ref[:, sl].astype(F32)

    row = lax.broadcasted_iota(jnp.int32, (C, GW), 0)
    lane = lax.broadcasted_iota(jnp.int32, (C, GW), 1)
    sidx = lane % C
    head = lane // C
    strict = sidx < row
    incl = sidx <= row
    eye = (sidx == row).astype(F32)
    ones = _head_ones(GW)

    def bd(x):
        z = jnp.zeros_like(x)
        return jnp.concatenate([jnp.where(head == h, x, z) for h in range(GH)], axis=0)

    def diag_blocks(zz):
        out = zz[:C]
        for h in range(1, GH):
            out = jnp.where(head == h, zz[h * C:(h + 1) * C], out)
        return out

    lw = lw_ref[...]
    rows_full = lax.broadcasted_iota(jnp.int32, lw.shape, 0)
    lc = lw
    sh = 1
    while sh < C:
        lc = lc + jnp.where(rows_full >= sh, pltpu.roll(lc, sh, axis=0), 0.0)
        sh *= 2

    groups = range(NG)
    sls = [slice(gi * GW, (gi + 1) * GW) for gi in groups]
    lhs, bh_kh, v_bd, ltot, sp = [], [], [], [], []
    gm = []
    for sl in sls:
        lwp, lp = lw[:, sl], lc[:, sl]
        lt = lp[C - 1:C, :]
        kp, kkp = ld(k_ref, sl), ld(kk_ref, sl)
        bp = kkp * ld(a_ref, sl)
        e_neg = jnp.exp(-lp)
        e_rem = jnp.exp(lt - lp)
        at = -(kkp * jnp.exp(lp - lwp))
        rt = ld(r_ref, sl) * jnp.exp(lp)
        lhs.append(jnp.concatenate([at, rt], axis=0).astype(BF16))
        gm.append(_mm_nt(lhs[-1], jnp.concatenate([bd(bp * e_neg), bd(kp * e_neg)], axis=0)))
        bh_kh.append(jnp.concatenate([bp * e_rem, kp * e_rem], axis=0).astype(BF16))
        v_bd.append(bd(ld(v_ref, sl)).astype(BF16))
        ltot.append(lt)
    n_ab = [jnp.where(strict, g[:C, :GW], 0.0) for g in gm]
    a_rb = [jnp.where(incl, g[C:, :GW], 0.0) for g in gm]
    a_k = [jnp.concatenate([jnp.where(strict, g[:C, GW:], 0.0), jnp.where(incl, g[C:, GW:], 0.0)], axis=0)
           for g in gm]
    for pp in groups:
        sp.append(st_scr[pp])
    s_prod = [_mm_nt(lhs[pp], bd(sp[pp])) for pp in groups]
    av = [_mm(a_k[pp], v_bd[pp]) for pp in groups]
    blk = [(row // (INV_BLOCK << lvl)) == (sidx // (INV_BLOCK << lvl)) for lvl in range(3)]
    assert INV_BLOCK << 2 == C
    pw = [jnp.where(blk[0], m, 0.0) for m in n_ab]
    tm = [eye + m for m in pw]
    pw_bd = [bd(m).astype(BF16) for m in pw]
    n = 2
    while n < INV_BLOCK:
        pw = [_mm(pw[pp], pw_bd[pp]) for pp in groups]
        pw_bd = [bd(m).astype(BF16) for m in pw]
        tm = [tm[pp] + _mm(tm[pp], pw_bd[pp]) for pp in groups]
        n *= 2
    for lvl in (1, 2):
        off = jnp.logical_and(blk[lvl], jnp.logical_not(blk[lvl - 1]))
        t_bd = [bd(m).astype(BF16) for m in tm]
        tl = [_mm(tm[pp], bd(jnp.where(off, n_ab[pp], 0.0))) for pp in groups]
        tm = [tm[pp] + _mm(tl[pp], t_bd[pp]) for pp in groups]
    u = [_mm(tm[pp], bd(s_prod[pp][:C] + av[pp][:C])) for pp in groups]
    y = [s_prod[pp][C:] + av[pp][C:] + _mm(a_rb[pp], bd(u[pp])) for pp in groups]
    for pp in groups:
        zz = _mm_tn(jnp.concatenate([u[pp], ld(v_ref, sls[pp])], axis=0), bh_kh[pp])
        st_scr[pp] = sp[pp] * jnp.exp(ltot[pp]) + diag_blocks(zz)
    def head_sums(parts):
        stacked = jnp.concatenate(parts, axis=0) if NG > 1 else parts[0]
        m = _mm(stacked, ones)
        return [m[pp * C:(pp + 1) * C] for pp in groups]

    mu = head_sums([t * (1.0 / HEAD_SIZE) for t in y])
    yc = [y[pp] - mu[pp] for pp in groups]
    var = head_sums([t * t * (1.0 / HEAD_SIZE) for t in yc])
    rk_sum = head_sums([ld(r_ref, sl) * ld(k_ref, sl) * rk_ref[:, sl] for sl in sls])
    for pp, sl in enumerate(sls):
        yn = yc[pp] * lax.rsqrt(var[pp] + GN_EPS) * lng_ref[:, sl] + lnb_ref[:, sl]
        z_ref[:, sl] = ((yn + rk_sum[pp] * ld(v_ref, sl)) * ld(g_ref, sl)).astype(z_ref.dtype)

    @pl.when(c == nc - 1)
    def _():
        for gi in range(NG):
            for h in range(GH):
                so_ref[0, GH * gi + h] = st_scr[gi, :, h * C:(h + 1) * C]


def _wkv_call(rkvkal, gate, r_k, ln_g, ln_b, state, *, seq_len, layer):
    T, D = gate.shape
    B = T // seq_len
    C = WKV_CHUNK
    NC = seq_len // C
    GH = min(WKV_GROUP_HEADS, D // HEAD_SIZE)
    GW = GH * HEAD_SIZE
    NG = min(WKV_GROUPS, D // GW)
    Q = D // (NG * GW)
    assert seq_len % C == 0 and D % (NG * GW) == 0 and GW % LANES == 0
    tile = pl.BlockSpec((C, NG * GW), lambda b, q, c: (b * NC + c, q))
    row = pl.BlockSpec((1, NG * GW), lambda b, q, c: (0, q))
    st = pl.BlockSpec((1, NG * GH, HEAD_SIZE, HEAD_SIZE), lambda b, q, c: (b, q, 0, 0))
    st_in = pl.BlockSpec((None, 1, NG * GH, HEAD_SIZE, HEAD_SIZE), lambda b, q, c: (layer, b, q, 0, 0))
    return pl.pallas_call(
        functools.partial(_wkv_kernel, NG=NG, GH=GH),
        grid=(B, Q, NC),
        in_specs=[tile] * 7 + [row] * 3 + [st_in],
        out_specs=[tile, st],
        out_shape=[jax.ShapeDtypeStruct((T, D), BF16), jax.ShapeDtypeStruct(state.shape[1:], F32)],
        scratch_shapes=[pltpu.VMEM((NG, HEAD_SIZE, GW), F32)],
        compiler_params=_params("arbitrary", "arbitrary", "arbitrary"),
        name="rwkv_wkv",
    )(*rkvkal, gate, _row(r_k.reshape(-1)), _row(ln_g), _row(ln_b), state)


def _pad_rank(w, axis):
    pad = [(0, 0)] * w.ndim
    pad[axis] = (0, -w.shape[axis] % LORA_PAD)
    return jnp.pad(w, pad)


def _rwkv_in_call(x, g, p, shift, v_first, *, seq_len, layer):
    T, D = x.shape
    carry = shift is None
    vres = v_first is not None
    vl = layer - 1
    TM, L, S = _tiling(T, seq_len, carry, RWKV_IN_TILE)
    tn = min(RWKV_IN_COLS, D)
    nj = D // tn
    row_d = pl.BlockSpec((1, D), lambda i, j: (0, 0))
    row_n = pl.BlockSpec((1, tn), lambda i, j: (0, j))
    tile_n = pl.BlockSpec((TM, tn), lambda i, j: (i, j))

    def col_w(lyr):
        return pl.BlockSpec((None, D, tn), lambda i, j: (lyr, 0, j))

    def lora1(a, lyr):
        return pl.BlockSpec((None,) + a.shape[1:], lambda i, j: (lyr, 0, 0))

    def lora2(a, lyr):
        return pl.BlockSpec((None, a.shape[1], tn), lambda i, j: (lyr, 0, j))

    in_specs = [pl.BlockSpec((TM, D), lambda i, j: (i, 0)), row_d, pl.BlockSpec((6, D), lambda i, j: (0, 0)),
                col_w(layer), col_w(layer), col_w(layer),
                lora1(p['w1'], layer), lora1(p['a1'], layer), lora1(p['g1'], layer),
                lora2(p['w2'], layer), lora2(p['a2'], layer), lora2(p['g2'], layer),
                row_n, row_n, row_n, row_n]
    args = [x, _row(g), p['mix'][layer], p['w_r'], p['w_k'], p['w_v'], p['w1'], p['a1'], p['g1'],
            p['w2'], p['a2'], p['g2'],
            _row(p['w0'][layer]), _row(p['a0'][layer]), _row(p['k_k'][layer]), _row(p['k_a'][layer])]
    if vres:
        in_specs += [lora1(p['v1'], vl), lora2(p['v2'], vl), row_n, tile_n]
        args += [p['v1'], p['v2'], _row(p['v0'][vl]), v_first]
    if not carry:
        in_specs.append(pl.BlockSpec((S, 1, D), lambda i, j: (i, 0, 0)))
        args.append(shift.reshape(-1, 1, D))
    nseg = (T // TM) * S
    out_specs = [tile_n] * 7 + [pl.BlockSpec((S, 1, D), lambda i, j: (i, 0, 0))]
    out_shape = ([jax.ShapeDtypeStruct((T, D), BF16)] * 5 + [jax.ShapeDtypeStruct((T, D), F32)]
                 + [jax.ShapeDtypeStruct((T, D), BF16), jax.ShapeDtypeStruct((nseg, 1, D), F32)])
    scratch = [pltpu.VMEM((3, TM, D), BF16), pltpu.VMEM((TM, p['w1'].shape[2]), BF16),
               pltpu.VMEM((TM, p['a1'].shape[2]), BF16), pltpu.VMEM((TM, p['g1'].shape[2]), BF16)]
    if vres:
        scratch.append(pltpu.VMEM((TM, p['v1'].shape[2]), BF16))
    if carry:
        scratch.append(pltpu.VMEM((1, D), F32))
    outs = pl.pallas_call(
        functools.partial(_rwkv_in_kernel, S=S, L=L, carry=carry, vres=vres),
        grid=(T // TM, nj),
        in_specs=in_specs,
        out_specs=out_specs,
        out_shape=out_shape,
        scratch_shapes=scratch,
        compiler_params=_params("arbitrary", "arbitrary"),
        name="rwkv_in_carry" if carry else "rwkv_in_state",
    )(*args)
    r, k, v, kk, a, lw, gate, hl = outs
    h_last = hl[-1:, 0, :] if carry else hl[:, 0, :]
    return (r, k, v, kk, a, lw), gate, h_last


def _proj_call(z, x, w, g, *, layer, ln_g=None, ln_b=None, b=None):
    T, D = x.shape
    TM = min(PROJ_TILE, T)
    assert T % TM == 0
    row = pl.BlockSpec((1, D), lambda i: (0, 0))
    tile = pl.BlockSpec((TM, D), lambda i: (i, 0))
    in_specs = [tile, tile, pl.BlockSpec((None, D, D), lambda i: (layer, 0, 0)), row]
    args = [z, x, w, _row(g)]
    if ln_g is not None:
        in_specs += [row, row]
        args += [_row(ln_g), _row(ln_b)]
    if b is not None:
        in_specs.append(row)
        args.append(_row(b))
    return pl.pallas_call(
        functools.partial(_proj_kernel, ln=ln_g is not None, bias=b is not None),
        grid=(T // TM,),
        in_specs=in_specs,
        out_specs=tile,
        out_shape=jax.ShapeDtypeStruct((T, D), F32),
        compiler_params=_params("arbitrary"),
        name="proj_ln" if ln_g is not None else "proj",
    )(*args)


def _trunk(x3, conv_st, shift_st, wkv_st, ffn_st, p):
    B, T, D = x3.shape
    x = x3.reshape(B * T, D)
    depth = p['norm_g'].shape[0]
    new_conv, new_shift, new_wkv, new_ffn = [], [], [], []
    v_first = None
    for i in range(depth):
        g = p['norm_g'][i]
        j = i // 2
        if i % 2 == 0:
            u = _glu_call(x, g[0], p['conv_pw1_w'], p['conv_pw1_b'][j], layer=j)
            c = _dwconv_call(u, p['conv_dw_w'][j], p['conv_dw_b'][j], conv_st, seq_len=T, layer=j)
            new_conv.append(u.reshape(B, T, D)[:, T - (CONV_WIDTH - 1):])
            x = _proj_call(c, x, p['conv_pw2_w'], g[1], layer=j, ln_g=p['conv_ln_g'][j], ln_b=p['conv_ln_b'][j],
                           b=p['conv_pw2_b'][j])
        else:
            q = {k[5:]: v for k, v in p.items() if k.startswith('rwkv_')}
            rkvkal, gate, h_last = _rwkv_in_call(x, g[0], q, None if shift_st is None else shift_st[j],
                                                 v_first if j > 0 else None, seq_len=T, layer=j)
            if j == 0:
                v_first = rkvkal[2]
            if wkv_st is None:
                s0, sl = jnp.zeros((1, B, D // HEAD_SIZE, HEAD_SIZE, HEAD_SIZE), F32), 0
            else:
                s0, sl = wkv_st, j
            z, s_new = _wkv_call(rkvkal, gate, q['r_k'][j], q['ln_g'][j], q['ln_b'][j], s0, seq_len=T, layer=sl)
            new_shift.append(h_last)
            new_wkv.append(s_new)
            x = _proj_call(z, x, q['w_o'], g[1], layer=j)
        x, nb = _ffn_call(x, g[2], g[3], p['ffn_w_in'], p['ffn_dw_w'][i], p['ffn_dw_b'][i], p['ffn_w_out'],
                          ffn_st, seq_len=T, layer=i)
        new_ffn.append(nb)
    return (x.reshape(B, T, D), jnp.stack(new_conv), jnp.stack(new_shift), jnp.stack(new_wkv), jnp.stack(new_ffn))


def kernel(x_prompt, x_sample, state_conv_mix, state_rwkv_shift, state_rwkv_wkv, state_ffn_conv, norm_g,
           conv_pw1_w, conv_pw1_b, conv_dw_w, conv_dw_b, conv_ln_g, conv_ln_b, conv_pw2_w, conv_pw2_b,
           rwkv_mix, rwkv_w_r, rwkv_w_k, rwkv_w_v, rwkv_w_o, rwkv_w0, rwkv_w1, rwkv_w2, rwkv_a0, rwkv_a1,
           rwkv_a2, rwkv_v0, rwkv_v1, rwkv_v2, rwkv_g1, rwkv_g2, rwkv_k_k, rwkv_k_a, rwkv_r_k, rwkv_ln_g,
           rwkv_ln_b, ffn_w_in, ffn_dw_w, ffn_dw_b, ffn_w_out):
    bf = lambda w: w.astype(BF16)
    pad_c = lambda w: _pad_rank(w, 2).astype(BF16)
    pad_r = lambda w: _pad_rank(w, 1).astype(BF16)
    p = dict(norm_g=norm_g, conv_pw1_w=bf(conv_pw1_w), conv_pw1_b=conv_pw1_b, conv_dw_w=conv_dw_w,
             conv_dw_b=conv_dw_b, conv_ln_g=conv_ln_g, conv_ln_b=conv_ln_b, conv_pw2_w=bf(conv_pw2_w),
             conv_pw2_b=conv_pw2_b, rwkv_mix=rwkv_mix, rwkv_w_r=bf(rwkv_w_r), rwkv_w_k=bf(rwkv_w_k),
             rwkv_w_v=bf(rwkv_w_v), rwkv_w_o=bf(rwkv_w_o), rwkv_w0=rwkv_w0, rwkv_w1=pad_c(rwkv_w1),
             rwkv_w2=pad_r(rwkv_w2), rwkv_a0=rwkv_a0, rwkv_a1=pad_c(rwkv_a1), rwkv_a2=pad_r(rwkv_a2),
             rwkv_v0=rwkv_v0, rwkv_v1=pad_c(rwkv_v1), rwkv_v2=pad_r(rwkv_v2), rwkv_g1=pad_c(rwkv_g1),
             rwkv_g2=pad_r(rwkv_g2), rwkv_k_k=rwkv_k_k, rwkv_k_a=rwkv_k_a, rwkv_r_k=rwkv_r_k,
             rwkv_ln_g=rwkv_ln_g, rwkv_ln_b=rwkv_ln_b, ffn_w_in=bf(ffn_w_in), ffn_dw_w=ffn_dw_w,
             ffn_dw_b=ffn_dw_b, ffn_w_out=bf(ffn_w_out))
    y_p, conv_p, shift_p, wkv_p, ffn_p = _trunk(x_prompt, None, None, None, None, p)
    y_s, conv_s, shift_s, wkv_s, ffn_s = _trunk(x_sample, state_conv_mix, state_rwkv_shift, state_rwkv_wkv,
                                                state_ffn_conv, p)
    return (y_p, y_s, conv_p, conv_s, shift_p, shift_s, wkv_p, wkv_s, ffn_p, ffn_s)
```

```python
import functools

import jax
import jax.numpy as jnp
from jax import lax
from jax.experimental import pallas as pl
from jax.experimental.pallas import tpu as pltpu

F32 = jnp.float32
BF16 = jnp.bfloat16

RMS_EPS = 1e-6
LN_EPS = 1e-5
GN_EPS = 64e-5
HEAD_SIZE = 64
CONV_WIDTH = 31
FFN_CONV_WIDTH = 3

LANES = 128
SUBLANES = 8
VMEM_LIMIT_BYTES = 56 * 1024 * 1024

TOKEN_TILE = 512
WKV_CHUNK = 64


def _params(*semantics):
    return pltpu.CompilerParams(dimension_semantics=semantics, vmem_limit_bytes=VMEM_LIMIT_BYTES)


def _rmsnorm(x, g):
    return x * lax.rsqrt(jnp.mean(x * x, axis=-1, keepdims=True) + RMS_EPS) * g


def _row(v):
    return v.reshape(1, -1)


GELU_C = 0.7978845608028654
GELU_K = 0.044715


def _gelu_tanh_times(x, u):
    inner = x * (GELU_C + (GELU_C * GELU_K) * (x * x))
    return (x * u) * (0.5 + 0.5 * jnp.tanh(inner))


def _ffn_kernel(*refs, S, L, carry):
    if carry:
        (x_ref, g2_ref, g3_ref, wu_ref, wg_ref, dww_ref, dwb_ref, wo_ref,
         o_ref, nb_ref, h_scr, gpx_scr, carry_scr) = refs
    else:
        (x_ref, g2_ref, g3_ref, wu_ref, wg_ref, dww_ref, dwb_ref, wo_ref, halo_ref,
         o_ref, nb_ref, h_scr, gpx_scr) = refs
    i = pl.program_id(0)
    j = pl.program_id(1)
    nj = pl.num_programs(1)
    K = FFN_CONV_WIDTH - 1
    tf = wu_ref.shape[-1]

    @pl.when(j == 0)
    def _():
        h_scr[...] = _rmsnorm(x_ref[...], g2_ref[...]).astype(BF16)
        o_ref[...] = jnp.zeros(o_ref.shape, F32)

    if carry:
        @pl.when(i == 0)
        def _():
            carry_scr[j] = jnp.zeros((SUBLANES, tf), F32)

    h = h_scr[...]
    w = dww_ref[...]
    nsub = 2 if tf % (2 * LANES) == 0 else 1
    ts = tf // nsub
    cols = [slice(s * ts, (s + 1) * ts) for s in range(nsub)]
    up = [(jnp.dot(h, wu_ref[:, cs], preferred_element_type=F32),
           jnp.dot(h, wg_ref[:, cs], preferred_element_type=F32)) for cs in cols]
    contrib = None
    for cs, (u, gp) in zip(cols, up):
        gp3 = gp.reshape(S, L, ts)
        gpx_scr[:, SUBLANES:SUBLANES + L, cs] = gp3
        if carry:
            gpx_scr[0, 0:SUBLANES, cs] = carry_scr[j, :, cs]
        else:
            gpx_scr[:, SUBLANES - K:SUBLANES, cs] = halo_ref[:, :, cs]
        gc = (gpx_scr[:, SUBLANES - 2:SUBLANES - 2 + L, cs] * w[0:1, cs]
              + gpx_scr[:, SUBLANES - 1:SUBLANES - 1 + L, cs] * w[1:2, cs]
              + gp3 * w[2:3, cs] + dwb_ref[:, cs])
        act = _gelu_tanh_times(gc, u.reshape(S, L, ts)).reshape(S * L, ts).astype(BF16)
        part = jnp.dot(act, wo_ref[cs, :], preferred_element_type=F32)
        contrib = part if contrib is None else contrib + part
    o_ref[...] += contrib

    nb_ref[...] = gpx_scr[:, SUBLANES + L - K:SUBLANES + L, :]
    if carry:
        carry_scr[j] = gpx_scr[0, L:L + SUBLANES, :]

    @pl.when(j == nj - 1)
    def _():
        o_ref[...] = x_ref[...] + _rmsnorm(o_ref[...], g3_ref[...])


def _ffn_chunk(d_ff):
    for tf in (512, 256, 128):
        if d_ff % tf == 0:
            return tf
    raise ValueError(f"d_ff={d_ff} is not a multiple of {LANES}")


def _ffn_call(x, g2, g3, w_in, dw_w, dw_b, w_out, halo, *, seq_len, layer):
    T, D = x.shape
    d_ff = w_out.shape[1]
    tf = _ffn_chunk(d_ff)
    nj = d_ff // tf
    carry = halo is None
    TM = min(TOKEN_TILE, T)
    L = TM if carry else seq_len
    S = TM // L
    assert T % TM == 0 and TM % L == 0 and (carry or L == seq_len)
    nt = T // TM
    in_specs = [
        pl.BlockSpec((TM, D), lambda i, j: (i, 0)),
        pl.BlockSpec((1, D), lambda i, j: (0, 0)),
        pl.BlockSpec((1, D), lambda i, j: (0, 0)),
        pl.BlockSpec((None, D, tf), lambda i, j: (layer, 0, j)),
        pl.BlockSpec((None, D, tf), lambda i, j: (layer, 0, nj + j)),
        pl.BlockSpec((FFN_CONV_WIDTH, tf), lambda i, j: (0, j)),
        pl.BlockSpec((1, tf), lambda i, j: (0, j)),
        pl.BlockSpec((None, tf, D), lambda i, j: (layer, j, 0)),
    ]
    args = [x, _row(g2), _row(g3), w_in, w_in, dw_w, _row(dw_b), w_out]
    scratch = [pltpu.VMEM((TM, D), BF16), pltpu.VMEM((S, SUBLANES + L, tf), F32)]
    nb_spec = pl.BlockSpec((S, FFN_CONV_WIDTH - 1, tf), lambda i, j: (i, 0, j))
    if carry:
        scratch.append(pltpu.VMEM((nj, SUBLANES, tf), F32))
    else:
        in_specs.append(pl.BlockSpec((None, S, FFN_CONV_WIDTH - 1, tf), lambda i, j: (layer, i, 0, j)))
        args.append(halo)
    out, nb = pl.pallas_call(
        functools.partial(_ffn_kernel, S=S, L=L, carry=carry),
        grid=(nt, nj),
        in_specs=in_specs,
        out_specs=[pl.BlockSpec((TM, D), lambda i, j: (i, 0)), nb_spec],
        out_shape=[jax.ShapeDtypeStruct((T, D), F32),
                   jax.ShapeDtypeStruct((nt * S, FFN_CONV_WIDTH - 1, d_ff), F32)],
        scratch_shapes=scratch,
        compiler_params=_params("arbitrary", "arbitrary"),
        name="ffn_carry" if carry else "ffn_state",
    )(*args)
    return out, (nb[-1:] if carry else nb)


def _tiling(T, seq_len, carry, tile=TOKEN_TILE):
    TM = min(tile, T)
    L = TM if carry else seq_len
    assert T % TM == 0 and TM % L == 0 and L % SUBLANES == 0
    return TM, L, TM // L


def _glu_kernel(x_ref, g_ref, wa_ref, wb_ref, ba_ref, bb_ref, u_ref, h_scr):
    @pl.when(pl.program_id(1) == 0)
    def _():
        h_scr[...] = _rmsnorm(x_ref[...], g_ref[...]).astype(BF16)

    h = h_scr[...]
    tn = wa_ref.shape[-1]
    ts = 2 * LANES if tn % (2 * LANES) == 0 else tn
    cols = [slice(c0, c0 + ts) for c0 in range(0, tn, ts)]
    ab = [(jnp.dot(h, wa_ref[:, cs], preferred_element_type=F32),
           jnp.dot(h, wb_ref[:, cs], preferred_element_type=F32)) for cs in cols]
    for cs, (a, b) in zip(cols, ab):
        u_ref[:, cs] = (a + ba_ref[:, cs]) * jax.nn.sigmoid(b + bb_ref[:, cs])


def _glu_call(x, g, w1, b1, *, layer):
    T, D = x.shape
    TM = min(TOKEN_TILE, T)
    tn = min(GLU_COLS, D)
    nj = D // tn
    assert T % TM == 0 and D % tn == 0
    return pl.pallas_call(
        _glu_kernel,
        grid=(T // TM, nj),
        in_specs=[
            pl.BlockSpec((TM, D), lambda i, j: (i, 0)),
            pl.BlockSpec((1, D), lambda i, j: (0, 0)),
            pl.BlockSpec((None, D, tn), lambda i, j: (layer, 0, j)),
            pl.BlockSpec((None, D, tn), lambda i, j: (layer, 0, nj + j)),
            pl.BlockSpec((1, tn), lambda i, j: (0, j)),
            pl.BlockSpec((1, tn), lambda i, j: (0, nj + j)),
        ],
        out_specs=pl.BlockSpec((TM, tn), lambda i, j: (i, j)),
        out_shape=jax.ShapeDtypeStruct((T, D), F32),
        scratch_shapes=[pltpu.VMEM((TM, D), BF16)],
        compiler_params=_params("arbitrary", "arbitrary"),
        name="conv_glu",
    )(x, _row(g), w1, w1, _row(b1), _row(b1))


HIST_PAD = 32
DW_ROWS = 32
DW_COLS = 512
GLU_COLS = 1024


def _dwconv_kernel(*refs, S, L, carry):
    if carry:
        u_ref, w_ref, b_ref, c_ref, ux_scr, sh_scr, carry_scr = refs
    else:
        u_ref, w_ref, b_ref, st_ref, c_ref, ux_scr, sh_scr = refs
    i = pl.program_id(0)
    j = pl.program_id(1)
    tc = u_ref.shape[-1]
    K = CONV_WIDTH - 1
    ux_scr[:, HIST_PAD:HIST_PAD + L, :] = u_ref[...].reshape(S, L, tc)
    if carry:
        @pl.when(i == 0)
        def _():
            carry_scr[j] = jnp.zeros((HIST_PAD, tc), F32)
        ux_scr[0, 0:HIST_PAD, :] = carry_scr[j]
    else:
        ux_scr[:, HIST_PAD - K:HIST_PAD, :] = st_ref[...]
    nr = HIST_PAD + L - SUBLANES
    for r in range(1, SUBLANES):
        sh_scr[r - 1] = ux_scr[:, r:r + nr, :]
    w = w_ref[...]
    rb = min(DW_ROWS, L)
    for s in range(S):
        for r0 in range(0, L, rb):
            acc = jnp.broadcast_to(b_ref[...], (rb, tc))
            for k in range(CONV_WIDTH):
                m = HIST_PAD - K + k
                r, lo = m % SUBLANES, r0 + m - m % SUBLANES
                tap = ux_scr[s, lo:lo + rb, :] if r == 0 else sh_scr[r - 1, s, lo:lo + rb, :]
                acc = acc + tap * w[k:k + 1, :]
            c_ref[s * L + r0:s * L + r0 + rb, :] = acc
    if carry:
        carry_scr[j] = ux_scr[0, L:L + HIST_PAD, :]


def _dwconv_call(u, w, b, state, *, seq_len, layer):
    T, D = u.shape
    carry = state is None
    TM, L, S = _tiling(T, seq_len, carry)
    tc = min(DW_COLS, D)
    nc = D // tc
    in_specs = [
        pl.BlockSpec((TM, tc), lambda i, j: (i, j)),
        pl.BlockSpec((CONV_WIDTH, tc), lambda i, j: (0, j)),
        pl.BlockSpec((1, tc), lambda i, j: (0, j)),
    ]
    args = [u, w, _row(b)]
    scratch = [pltpu.VMEM((S, HIST_PAD + L, tc), F32),
               pltpu.VMEM((SUBLANES - 1, S, HIST_PAD + L - SUBLANES, tc), F32)]
    if carry:
        scratch.append(pltpu.VMEM((nc, HIST_PAD, tc), F32))
    else:
        in_specs.append(pl.BlockSpec((None, S, CONV_WIDTH - 1, tc), lambda i, j: (layer, i, 0, j)))
        args.append(state)
    return pl.pallas_call(
        functools.partial(_dwconv_kernel, S=S, L=L, carry=carry),
        grid=(T // TM, nc),
        in_specs=in_specs,
        out_specs=pl.BlockSpec((TM, tc), lambda i, j: (i, j)),
        out_shape=jax.ShapeDtypeStruct((T, D), F32),
        scratch_shapes=scratch,
        compiler_params=_params("arbitrary", "arbitrary"),
        name="conv_dw_carry" if carry else "conv_dw_state",
    )(*args)


def _proj_kernel(*refs, ln, bias):
    refs = list(refs)
    z_ref, x_ref, w_ref, g_ref = refs[:4]
    rest = refs[4:]
    if ln:
        lg_ref, lb_ref = rest[:2]
        rest = rest[2:]
    if bias:
        b_ref = rest[0]
        rest = rest[1:]
    o_ref, = rest
    z = z_ref[...].astype(F32)
    if ln:
        mu = jnp.mean(z, axis=-1, keepdims=True)
        zc = z - mu
        var = jnp.mean(zc * zc, axis=-1, keepdims=True)
        z = zc * lax.rsqrt(var + LN_EPS) * lg_ref[...] + lb_ref[...]
        z = z * jax.nn.sigmoid(z)
    out = jnp.dot(z.astype(BF16), w_ref[...], preferred_element_type=F32)
    if bias:
        out = out + b_ref[...]
    o_ref[...] = x_ref[...] + _rmsnorm(out, g_ref[...])


PROJ_TILE = 512


def _head_ones(width=LANES):
    r = lax.broadcasted_iota(jnp.int32, (width, width), 0) // HEAD_SIZE
    c = lax.broadcasted_iota(jnp.int32, (width, width), 1) // HEAD_SIZE
    return (r == c).astype(BF16)


def _head_sum(x, ones):
    outs = []
    for p in range(x.shape[1] // LANES):
        xp = x[:, p * LANES:(p + 1) * LANES]
        hi = xp.astype(BF16)
        lo = (xp - hi.astype(F32)).astype(BF16)
        outs.append(jnp.dot(hi, ones, preferred_element_type=F32) + jnp.dot(lo, ones, preferred_element_type=F32))
    return outs[0] if len(outs) == 1 else jnp.concatenate(outs, axis=1)


LORA_PAD = 128
RWKV_IN_TILE = 512
RWKV_IN_COLS = 512


def _softplus(z):
    return jnp.maximum(z, 0.0) + jnp.log(1.0 + jnp.exp(-jnp.abs(z)))


MIX_CHUNK = 512


def _rwkv_in_kernel(*refs, S, L, carry, vres):
    refs = list(refs)
    x_ref, g_ref, mix_ref, wr_ref, wk_ref, wv_ref, w1_ref, a1_ref, g1_ref = refs[:9]
    w2_ref, a2_ref, g2_ref, w0_ref, a0_ref, kk_ref, ka_ref = refs[9:16]
    rest = refs[16:]
    if vres:
        v1_ref, v2_ref, v0_ref, vf_ref = rest[:4]
        rest = rest[4:]
    if not carry:
        sh_ref = rest[0]
        rest = rest[1:]
    r_ref, k_ref, v_ref, kkn_ref, a_ref, lw_ref, go_ref, hl_ref = rest[:8]
    rest = rest[8:]
    mx_scr, tw_scr, ta_scr, tg_scr = rest[:4]
    rest = rest[4:]
    if vres:
        tv_scr = rest[0]
        rest = rest[1:]
    if carry:
        carry_scr, = rest
    i = pl.program_id(0)
    j = pl.program_id(1)
    TM, D = x_ref.shape

    @pl.when(j == 0)
    def _():
        x = x_ref[...]
        rs = lax.rsqrt(jnp.mean(x * x, axis=-1, keepdims=True) + RMS_EPS)
        if carry:
            @pl.when(i == 0)
            def _():
                carry_scr[...] = jnp.zeros((1, D), F32)
        mc = min(MIX_CHUNK, D)
        seg_start = lax.broadcasted_iota(jnp.int32, (TM, mc), 0) % L == 0
        lora = {}

        def acc(name, xm, w_ref, cs):
            part = jnp.dot(xm, w_ref[cs, :], preferred_element_type=F32)
            lora[name] = part if name not in lora else lora[name] + part

        for c0 in range(0, D, mc):
            cs = slice(c0, c0 + mc)
            h = x_ref[:, cs] * rs * g_ref[:, cs]
            if carry:
                first = jnp.broadcast_to(carry_scr[:, cs], (TM, mc))
                carry_scr[:, cs] = h[TM - 1:TM, :]
            else:
                first = jnp.broadcast_to(sh_ref[:, :, cs], (S, L, mc)).reshape(TM, mc)
            for s in range(S):
                hl_ref[s, :, cs] = h[s * L + L - 1:s * L + L, :]
            xx = jnp.where(seg_start, first, pltpu.roll(h, 1, axis=0)) - h
            mixed = [(h + xx * mix_ref[m:m + 1, cs]).astype(BF16) for m in range(6)]
            mx_scr[0, :, cs] = mixed[0]
            mx_scr[1, :, cs] = mixed[2]
            mx_scr[2, :, cs] = mixed[3]
            acc('w', mixed[1], w1_ref, cs)
            acc('a', mixed[4], a1_ref, cs)
            acc('g', mixed[5], g1_ref, cs)
            if vres:
                acc('v', mixed[3], v1_ref, cs)
        tw_scr[...] = jnp.tanh(lora['w']).astype(BF16)
        ta_scr[...] = lora['a'].astype(BF16)
        tg_scr[...] = jax.nn.sigmoid(lora['g']).astype(BF16)
        if vres:
            tv_scr[...] = lora['v'].astype(BF16)

    tn = wr_ref.shape[-1]
    ts = 2 * LANES if tn % (2 * LANES) == 0 else tn
    ones = _head_ones()
    for c0 in range(0, tn, ts):
        cs = slice(c0, c0 + ts)
        r = jnp.dot(mx_scr[0], wr_ref[:, cs], preferred_element_type=F32)
        k = jnp.dot(mx_scr[1], wk_ref[:, cs], preferred_element_type=F32)
        v = jnp.dot(mx_scr[2], wv_ref[:, cs], preferred_element_type=F32)
        wl = jnp.dot(tw_scr[...], w2_ref[:, cs], preferred_element_type=F32)
        al = jnp.dot(ta_scr[...], a2_ref[:, cs], preferred_element_type=F32)
        go_ref[:, cs] = jnp.dot(tg_scr[...], g2_ref[:, cs], preferred_element_type=F32).astype(go_ref.dtype)
        w = -_softplus(-(w0_ref[:, cs] + wl)) - 0.5
        lw_ref[:, cs] = -jnp.exp(w)
        a = jax.nn.sigmoid(a0_ref[:, cs] + al)
        if vres:
            vl = jnp.dot(tv_scr[...], v2_ref[:, cs], preferred_element_type=F32)
            v = v + (vf_ref[:, cs].astype(F32) - v) * jax.nn.sigmoid(v0_ref[:, cs] + vl)
        kk = k * kk_ref[:, cs]
        norm = jnp.sqrt(_head_sum(kk * kk, ones))
        kkn_ref[:, cs] = (kk / jnp.maximum(norm, 1e-12)).astype(kkn_ref.dtype)
        r_ref[:, cs] = r.astype(r_ref.dtype)
        k_ref[:, cs] = (k * (1.0 + (a - 1.0) * ka_ref[:, cs])).astype(k_ref.dtype)
        v_ref[:, cs] = v.astype(v_ref.dtype)
        a_ref[:, cs] = a.astype(a_ref.dtype)


WKV_GROUP_HEADS = LANES // HEAD_SIZE
WKV_GROUPS = 16
INV_BLOCK = 16


def _mm(a, b):
    return jnp.dot(a.astype(BF16), b.astype(BF16), preferred_element_type=F32)


def _mm_nt(a, b):
    return lax.dot_general(a.astype(BF16), b.astype(BF16), (((1,), (1,)), ((), ())), preferred_element_type=F32)


def _mm_tn(a, b):
    return lax.dot_general(a.astype(BF16), b.astype(BF16), (((0,), (0,)), ((), ())), preferred_element_type=F32)


def _wkv_kernel(r_ref, k_ref, v_ref, kk_ref, a_ref, lw_ref, g_ref, rk_ref, lng_ref, lnb_ref, s_ref,
                z_ref, so_ref, st_scr, *, NG, GH):
    c = pl.program_id(2)
    nc = pl.num_programs(2)
    C = r_ref.shape[0]
    GW = GH * C
    assert C == HEAD_SIZE

    @pl.when(c == 0)
    def _():
        for gi in range(NG):
            st_scr[gi] = jnp.concatenate([s_ref[0, GH * gi + h] for h in range(GH)], axis=1)

    def ld(ref, sl):
        return ref[:, sl].astype(F32)

    row = lax.broadcasted_iota(jnp.int32, (C, GW), 0)
    lane = lax.broadcasted_iota(jnp.int32, (C, GW), 1)
    sidx = lane % C
    head = lane // C
    strict = sidx < row
    incl = sidx <= row
    eye = (sidx == row).astype(F32)
    ones = _head_ones(GW)

    def bd(x):
        z = jnp.zeros_like(x)
        return jnp.concatenate([jnp.where(head == h, x, z) for h in range(GH)], axis=0)

    def diag_blocks(zz):
        out = zz[:C]
        for h in range(1, GH):
            out = jnp.where(head == h, zz[h * C:(h + 1) * C], out)
        return out

    lw = lw_ref[...]
    rows_full = lax.broadcasted_iota(jnp.int32, lw.shape, 0)
    lc = lw
    sh = 1
    while sh < C:
        lc = lc + jnp.where(rows_full >= sh, pltpu.roll(lc, sh, axis=0), 0.0)
        sh *= 2

    groups = range(NG)
    sls = [slice(gi * GW, (gi + 1) * GW) for gi in groups]
    lhs, bh_kh, v_bd, ltot, sp = [], [], [], [], []
    gm = []
    for sl in sls:
        lwp, lp = lw[:, sl], lc[:, sl]
        lt = lp[C - 1:C, :]
        kp, kkp = ld(k_ref, sl), ld(kk_ref, sl)
        bp = kkp * ld(a_ref, sl)
        e_neg = jnp.exp(-lp)
        e_rem = jnp.exp(lt - lp)
        at = -(kkp * jnp.exp(lp - lwp))
        rt = ld(r_ref, sl) * jnp.exp(lp)
        lhs.append(jnp.concatenate([at, rt], axis=0).astype(BF16))
        gm.append(_mm_nt(lhs[-1], jnp.concatenate([bd(bp * e_neg), bd(kp * e_neg)], axis=0)))
        bh_kh.append(jnp.concatenate([bp * e_rem, kp * e_rem], axis=0).astype(BF16))
        v_bd.append(bd(ld(v_ref, sl)).astype(BF16))
        ltot.append(lt)
    n_ab = [jnp.where(strict, g[:C, :GW], 0.0) for g in gm]
    a_rb = [jnp.where(incl, g[C:, :GW], 0.0) for g in gm]
    a_k = [jnp.concatenate([jnp.where(strict, g[:C, GW:], 0.0), jnp.where(incl, g[C:, GW:], 0.0)], axis=0)
           for g in gm]
    for pp in groups:
        sp.append(st_scr[pp])
    s_prod = [_mm_nt(lhs[pp], bd(sp[pp])) for pp in groups]
    av = [_mm(a_k[pp], v_bd[pp]) for pp in groups]
    blk = [(row // (INV_BLOCK << lvl)) == (sidx // (INV_BLOCK << lvl)) for lvl in range(3)]
    assert INV_BLOCK << 2 == C
    pw = [jnp.where(blk[0], m, 0.0) for m in n_ab]
    tm = [eye + m for m in pw]
    pw_bd = [bd(m).astype(BF16) for m in pw]
    n = 2
    while n < INV_BLOCK:
        pw = [_mm(pw[pp], pw_bd[pp]) for pp in groups]
        pw_bd = [bd(m).astype(BF16) for m in pw]
        tm = [tm[pp] + _mm(tm[pp], pw_bd[pp]) for pp in groups]
        n *= 2
    for lvl in (1, 2):
        off = jnp.logical_and(blk[lvl], jnp.logical_not(blk[lvl - 1]))
        t_bd = [bd(m).astype(BF16) for m in tm]
        tl = [_mm(tm[pp], bd(jnp.where(off, n_ab[pp], 0.0))) for pp in groups]
        tm = [tm[pp] + _mm(tl[pp], t_bd[pp]) for pp in groups]
    u = [_mm(tm[pp], bd(s_prod[pp][:C] + av[pp][:C])) for pp in groups]
    y = [s_prod[pp][C:] + av[pp][C:] + _mm(a_rb[pp], bd(u[pp])) for pp in groups]
    for pp in groups:
        zz = _mm_tn(jnp.concatenate([u[pp], ld(v_ref, sls[pp])], axis=0), bh_kh[pp])
        st_scr[pp] = sp[pp] * jnp.exp(ltot[pp]) + diag_blocks(zz)
    def head_sums(parts):
        stacked = jnp.concatenate(parts, axis=0) if NG > 1 else parts[0]
        m = _mm(stacked, ones)
        return [m[pp * C:(pp + 1) * C] for pp in groups]

    mu = head_sums([t * (1.0 / HEAD_SIZE) for t in y])
    yc = [y[pp] - mu[pp] for pp in groups]
    var = head_sums([t * t * (1.0 / HEAD_SIZE) for t in yc])
    rk_sum = head_sums([ld(r_ref, sl) * ld(k_ref, sl) * rk_ref[:, sl] for sl in sls])
    for pp, sl in enumerate(sls):
        yn = yc[pp] * lax.rsqrt(var[pp] + GN_EPS) * lng_ref[:, sl] + lnb_ref[:, sl]
        z_ref[:, sl] = ((yn + rk_sum[pp] * ld(v_ref, sl)) * ld(g_ref, sl)).astype(z_ref.dtype)

    @pl.when(c == nc - 1)
    def _():
        for gi in range(NG):
            for h in range(GH):
                so_ref[0, GH * gi + h] = st_scr[gi, :, h * C:(h + 1) * C]


def _wkv_call(rkvkal, gate, r_k, ln_g, ln_b, state, *, seq_len, layer):
    T, D = gate.shape
    B = T // seq_len
    C = WKV_CHUNK
    NC = seq_len // C
    GH = min(WKV_GROUP_HEADS, D // HEAD_SIZE)
    GW = GH * HEAD_SIZE
    NG = min(WKV_GROUPS, D // GW)
    Q = D // (NG * GW)
    assert seq_len % C == 0 and D % (NG * GW) == 0 and GW % LANES == 0
    tile = pl.BlockSpec((C, NG * GW), lambda b, q, c: (b * NC + c, q))
    row = pl.BlockSpec((1, NG * GW), lambda b, q, c: (0, q))
    st = pl.BlockSpec((1, NG * GH, HEAD_SIZE, HEAD_SIZE), lambda b, q, c: (b, q, 0, 0))
    st_in = pl.BlockSpec((None, 1, NG * GH, HEAD_SIZE, HEAD_SIZE), lambda b, q, c: (layer, b, q, 0, 0))
    return pl.pallas_call(
        functools.partial(_wkv_kernel, NG=NG, GH=GH),
        grid=(B, Q, NC),
        in_specs=[tile] * 7 + [row] * 3 + [st_in],
        out_specs=[tile, st],
        out_shape=[jax.ShapeDtypeStruct((T, D), BF16), jax.ShapeDtypeStruct(state.shape[1:], F32)],
        scratch_shapes=[pltpu.VMEM((NG, HEAD_SIZE, GW), F32)],
        compiler_params=_params("arbitrary", "arbitrary", "arbitrary"),
        name="rwkv_wkv",
    )(*rkvkal, gate, _row(r_k.reshape(-1)), _row(ln_g), _row(ln_b), state)


def _pad_rank(w, axis):
    pad = [(0, 0)] * w.ndim
    pad[axis] = (0, -w.shape[axis] % LORA_PAD)
    return jnp.pad(w, pad)


def _rwkv_in_call(x, g, p, shift, v_first, *, seq_len, layer):
    T, D = x.shape
    carry = shift is None
    vres = v_first is not None
    vl = layer - 1
    TM, L, S = _tiling(T, seq_len, carry, RWKV_IN_TILE)
    tn = min(RWKV_IN_COLS, D)
    nj = D // tn
    row_d = pl.BlockSpec((1, D), lambda i, j: (0, 0))
    row_n = pl.BlockSpec((1, tn), lambda i, j: (0, j))
    tile_n = pl.BlockSpec((TM, tn), lambda i, j: (i, j))

    def col_w(lyr):
        return pl.BlockSpec((None, D, tn), lambda i, j: (lyr, 0, j))

    def lora1(a, lyr):
        return pl.BlockSpec((None,) + a.shape[1:], lambda i, j: (lyr, 0, 0))

    def lora2(a, lyr):
        return pl.BlockSpec((None, a.shape[1], tn), lambda i, j: (lyr, 0, j))

    in_specs = [pl.BlockSpec((TM, D), lambda i, j: (i, 0)), row_d, pl.BlockSpec((6, D), lambda i, j: (0, 0)),
                col_w(layer), col_w(layer), col_w(layer),
                lora1(p['w1'], layer), lora1(p['a1'], layer), lora1(p['g1'], layer),
                lora2(p['w2'], layer), lora2(p['a2'], layer), lora2(p['g2'], layer),
                row_n, row_n, row_n, row_n]
    args = [x, _row(g), p['mix'][layer], p['w_r'], p['w_k'], p['w_v'], p['w1'], p['a1'], p['g1'],
            p['w2'], p['a2'], p['g2'],
            _row(p['w0'][layer]), _row(p['a0'][layer]), _row(p['k_k'][layer]), _row(p['k_a'][layer])]
    if vres:
        in_specs += [lora1(p['v1'], vl), lora2(p['v2'], vl), row_n, tile_n]
        args += [p['v1'], p['v2'], _row(p['v0'][vl]), v_first]
    if not carry:
        in_specs.append(pl.BlockSpec((S, 1, D), lambda i, j: (i, 0, 0)))
        args.append(shift.reshape(-1, 1, D))
    nseg = (T // TM) * S
    out_specs = [tile_n] * 7 + [pl.BlockSpec((S, 1, D), lambda i, j: (i, 0, 0))]
    out_shape = ([jax.ShapeDtypeStruct((T, D), BF16)] * 5 + [jax.ShapeDtypeStruct((T, D), F32)]
                 + [jax.ShapeDtypeStruct((T, D), BF16), jax.ShapeDtypeStruct((nseg, 1, D), F32)])
    scratch = [pltpu.VMEM((3, TM, D), BF16), pltpu.VMEM((TM, p['w1'].shape[2]), BF16),
               pltpu.VMEM((TM, p['a1'].shape[2]), BF16), pltpu.VMEM((TM, p['g1'].shape[2]), BF16)]
    if vres:
        scratch.append(pltpu.VMEM((TM, p['v1'].shape[2]), BF16))
    if carry:
        scratch.append(pltpu.VMEM((1, D), F32))
    outs = pl.pallas_call(
        functools.partial(_rwkv_in_kernel, S=S, L=L, carry=carry, vres=vres),
        grid=(T // TM, nj),
        in_specs=in_specs,
        out_specs=out_specs,
        out_shape=out_shape,
        scratch_shapes=scratch,
        compiler_params=_params("arbitrary", "arbitrary"),
        name="rwkv_in_carry" if carry else "rwkv_in_state",
    )(*args)
    r, k, v, kk, a, lw, gate, hl = outs
    h_last = hl[-1:, 0, :] if carry else hl[:, 0, :]
    return (r, k, v, kk, a, lw), gate, h_last


def _proj_call(z, x, w, g, *, layer, ln_g=None, ln_b=None, b=None):
    T, D = x.shape
    TM = min(PROJ_TILE, T)
    assert T % TM == 0
    row = pl.BlockSpec((1, D), lambda i: (0, 0))
    tile = pl.BlockSpec((TM, D), lambda i: (i, 0))
    in_specs = [tile, tile, pl.BlockSpec((None, D, D), lambda i: (layer, 0, 0)), row]
    args = [z, x, w, _row(g)]
    if ln_g is not None:
        in_specs += [row, row]
        args += [_row(ln_g), _row(ln_b)]
    if b is not None:
        in_specs.append(row)
        args.append(_row(b))
    return pl.pallas_call(
        functools.partial(_proj_kernel, ln=ln_g is not None, bias=b is not None),
        grid=(T // TM,),
        in_specs=in_specs,
        out_specs=tile,
        out_shape=jax.ShapeDtypeStruct((T, D), F32),
        compiler_params=_params("arbitrary"),
        name="proj_ln" if ln_g is not None else "proj",
    )(*args)


def _trunk(x3, conv_st, shift_st, wkv_st, ffn_st, p):
    B, T, D = x3.shape
    x = x3.reshape(B * T, D)
    depth = p['norm_g'].shape[0]
    new_conv, new_shift, new_wkv, new_ffn = [], [], [], []
    v_first = None
    for i in range(depth):
        g = p['norm_g'][i]
        j = i // 2
        if i % 2 == 0:
            u = _glu_call(x, g[0], p['conv_pw1_w'], p['conv_pw1_b'][j], layer=j)
            c = _dwconv_call(u, p['conv_dw_w'][j], p['conv_dw_b'][j], conv_st, seq_len=T, layer=j)
            new_conv.append(u.reshape(B, T, D)[:, T - (CONV_WIDTH - 1):])
            x = _proj_call(c, x, p['conv_pw2_w'], g[1], layer=j, ln_g=p['conv_ln_g'][j], ln_b=p['conv_ln_b'][j],
                           b=p['conv_pw2_b'][j])
        else:
            q = {k[5:]: v for k, v in p.items() if k.startswith('rwkv_')}
            rkvkal, gate, h_last = _rwkv_in_call(x, g[0], q, None if shift_st is None else shift_st[j],
                                                 v_first if j > 0 else None, seq_len=T, layer=j)
            if j == 0:
                v_first = rkvkal[2]
            if wkv_st is None:
                s0, sl = jnp.zeros((1, B, D // HEAD_SIZE, HEAD_SIZE, HEAD_SIZE), F32), 0
            else:
                s0, sl = wkv_st, j
            z, s_new = _wkv_call(rkvkal, gate, q['r_k'][j], q['ln_g'][j], q['ln_b'][j], s0, seq_len=T, layer=sl)
            new_shift.append(h_last)
            new_wkv.append(s_new)
            x = _proj_call(z, x, q['w_o'], g[1], layer=j)
        x, nb = _ffn_call(x, g[2], g[3], p['ffn_w_in'], p['ffn_dw_w'][i], p['ffn_dw_b'][i], p['ffn_w_out'],
                          ffn_st, seq_len=T, layer=i)
        new_ffn.append(nb)
    return (x.reshape(B, T, D), jnp.stack(new_conv), jnp.stack(new_shift), jnp.stack(new_wkv), jnp.stack(new_ffn))


def kernel(x_prompt, x_sample, state_conv_mix, state_rwkv_shift, state_rwkv_wkv, state_ffn_conv, norm_g,
           conv_pw1_w, conv_pw1_b, conv_dw_w, conv_dw_b, conv_ln_g, conv_ln_b, conv_pw2_w, conv_pw2_b,
           rwkv_mix, rwkv_w_r, rwkv_w_k, rwkv_w_v, rwkv_w_o, rwkv_w0, rwkv_w1, rwkv_w2, rwkv_a0, rwkv_a1,
           rwkv_a2, rwkv_v0, rwkv_v1, rwkv_v2, rwkv_g1, rwkv_g2, rwkv_k_k, rwkv_k_a, rwkv_r_k, rwkv_ln_g,
           rwkv_ln_b, ffn_w_in, ffn_dw_w, ffn_dw_b, ffn_w_out):
    bf = lambda w: w.astype(BF16)
    pad_c = lambda w: _pad_rank(w, 2).astype(BF16)
    pad_r = lambda w: _pad_rank(w, 1).astype(BF16)
    p = dict(norm_g=norm_g, conv_pw1_w=bf(conv_pw1_w), conv_pw1_b=conv_pw1_b, conv_dw_w=conv_dw_w,
             conv_dw_b=conv_dw_b, conv_ln_g=conv_ln_g, conv_ln_b=conv_ln_b, conv_pw2_w=bf(conv_pw2_w),
             conv_pw2_b=conv_pw2_b, rwkv_mix=rwkv_mix, rwkv_w_r=bf(rwkv_w_r), rwkv_w_k=bf(rwkv_w_k),
             rwkv_w_v=bf(rwkv_w_v), rwkv_w_o=bf(rwkv_w_o), rwkv_w0=rwkv_w0, rwkv_w1=pad_c(rwkv_w1),
             rwkv_w2=pad_r(rwkv_w2), rwkv_a0=rwkv_a0, rwkv_a1=pad_c(rwkv_a1), rwkv_a2=pad_r(rwkv_a2),
             rwkv_v0=rwkv_v0, rwkv_v1=pad_c(rwkv_v1), rwkv_v2=pad_r(rwkv_v2), rwkv_g1=pad_c(rwkv_g1),
             rwkv_g2=pad_r(rwkv_g2), rwkv_k_k=rwkv_k_k, rwkv_k_a=rwkv_k_a, rwkv_r_k=rwkv_r_k,
             rwkv_ln_g=rwkv_ln_g, rwkv_ln_b=rwkv_ln_b, ffn_w_in=bf(ffn_w_in), ffn_dw_w=ffn_dw_w,
             ffn_dw_b=ffn_dw_b, ffn_w_out=bf(ffn_w_out))
    y_p, conv_p, shift_p, wkv_p, ffn_p = _trunk(x_prompt, None, None, None, None, p)
    y_s, conv_s, shift_s, wkv_s, ffn_s = _trunk(x_sample, state_conv_mix, state_rwkv_shift, state_rwkv_wkv,
                                                state_ffn_conv, p)
    return (y_p, y_s, conv_p, conv_s, shift_p, shift_s, wkv_p, wkv_s, ffn_p, ffn_s)
```

```python
import functools

import jax
import jax.numpy as jnp
from jax import lax
from jax.experimental import pallas as pl
from jax.experimental.pallas import tpu as pltpu

F32 = jnp.float32
BF16 = jnp.bfloat16

RMS_EPS = 1e-6
LN_EPS = 1e-5
GN_EPS = 64e-5
HEAD_SIZE = 64
CONV_WIDTH = 31
FFN_CONV_WIDTH = 3

LANES = 128
SUBLANES = 8
VMEM_LIMIT_BYTES = 56 * 1024 * 1024

TOKEN_TILE = 512
WKV_CHUNK = 64


def _params(*semantics):
    return pltpu.CompilerParams(dimension_semantics=semantics, vmem_limit_bytes=VMEM_LIMIT_BYTES)


def _rmsnorm(x, g):
    return x * lax.rsqrt(jnp.mean(x * x, axis=-1, keepdims=True) + RMS_EPS) * g


def _row(v):
    return v.reshape(1, -1)


GELU_C = 0.7978845608028654
GELU_K = 0.044715


def _gelu_tanh_times(x, u):
    inner = x * (GELU_C + (GELU_C * GELU_K) * (x * x))
    return (x * u) * (0.5 + 0.5 * jnp.tanh(inner))


def _ffn_kernel(*refs, S, L, carry):
    if carry:
        (x_ref, g2_ref, g3_ref, wu_ref, wg_ref, dww_ref, dwb_ref, wo_ref,
         o_ref, nb_ref, h_scr, gpx_scr, carry_scr) = refs
    else:
        (x_ref, g2_ref, g3_ref, wu_ref, wg_ref, dww_ref, dwb_ref, wo_ref, halo_ref,
         o_ref, nb_ref, h_scr, gpx_scr) = refs
    i = pl.program_id(0)
    j = pl.program_id(1)
    nj = pl.num_programs(1)
    K = FFN_CONV_WIDTH - 1
    tf = wu_ref.shape[-1]

    @pl.when(j == 0)
    def _():
        h_scr[...] = _rmsnorm(x_ref[...], g2_ref[...]).astype(BF16)
        o_ref[...] = jnp.zeros(o_ref.shape, F32)

    if carry:
        @pl.when(i == 0)
        def _():
            carry_scr[j] = jnp.zeros((SUBLANES, tf), F32)

    h = h_scr[...]
    w = dww_ref[...]
    nsub = 2 if tf % (2 * LANES) == 0 else 1
    ts = tf // nsub
    cols = [slice(s * ts, (s + 1) * ts) for s in range(nsub)]
    up = [(jnp.dot(h, wu_ref[:, cs], preferred_element_type=F32),
           jnp.dot(h, wg_ref[:, cs], preferred_element_type=F32)) for cs in cols]
    contrib = None
    for cs, (u, gp) in zip(cols, up):
        gp3 = gp.reshape(S, L, ts)
        gpx_scr[:, SUBLANES:SUBLANES + L, cs] = gp3
        if carry:
            gpx_scr[0, 0:SUBLANES, cs] = carry_scr[j, :, cs]
        else:
            gpx_scr[:, SUBLANES - K:SUBLANES, cs] = halo_ref[:, :, cs]
        gc = (gpx_scr[:, SUBLANES - 2:SUBLANES - 2 + L, cs] * w[0:1, cs]
              + gpx_scr[:, SUBLANES - 1:SUBLANES - 1 + L, cs] * w[1:2, cs]
              + gp3 * w[2:3, cs] + dwb_ref[:, cs])
        act = _gelu_tanh_times(gc, u.reshape(S, L, ts)).reshape(S * L, ts).astype(BF16)
        part = jnp.dot(act, wo_ref[cs, :], preferred_element_type=F32)
        contrib = part if contrib is None else contrib + part
    o_ref[...] += contrib

    nb_ref[...] = gpx_scr[:, SUBLANES + L - K:SUBLANES + L, :]
    if carry:
        carry_scr[j] = gpx_scr[0, L:L + SUBLANES, :]

    @pl.when(j == nj - 1)
    def _():
        o_ref[...] = x_ref[...] + _rmsnorm(o_ref[...], g3_ref[...])


def _ffn_chunk(d_ff):
    for tf in (512, 256, 128):
        if d_ff % tf == 0:
            return tf
    raise ValueError(f"d_ff={d_ff} is not a multiple of {LANES}")


def _ffn_call(x, g2, g3, w_in, dw_w, dw_b, w_out, halo, *, seq_len, layer):
    T, D = x.shape
    d_ff = w_out.shape[1]
    tf = _ffn_chunk(d_ff)
    nj = d_ff // tf
    carry = halo is None
    TM = min(TOKEN_TILE, T)
    L = TM if carry else seq_len
    S = TM // L
    assert T % TM == 0 and TM % L == 0 and (carry or L == seq_len)
    nt = T // TM
    in_specs = [
        pl.BlockSpec((TM, D), lambda i, j: (i, 0)),
        pl.BlockSpec((1, D), lambda i, j: (0, 0)),
        pl.BlockSpec((1, D), lambda i, j: (0, 0)),
        pl.BlockSpec((None, D, tf), lambda i, j: (layer, 0, j)),
        pl.BlockSpec((None, D, tf), lambda i, j: (layer, 0, nj + j)),
        pl.BlockSpec((FFN_CONV_WIDTH, tf), lambda i, j: (0, j)),
        pl.BlockSpec((1, tf), lambda i, j: (0, j)),
        pl.BlockSpec((None, tf, D), lambda i, j: (layer, j, 0)),
    ]
    args = [x, _row(g2), _row(g3), w_in, w_in, dw_w, _row(dw_b), w_out]
    scratch = [pltpu.VMEM((TM, D), BF16), pltpu.VMEM((S, SUBLANES + L, tf), F32)]
    nb_spec = pl.BlockSpec((S, FFN_CONV_WIDTH - 1, tf), lambda i, j: (i, 0, j))
    if carry:
        scratch.append(pltpu.VMEM((nj, SUBLANES, tf), F32))
    else:
        in_specs.append(pl.BlockSpec((None, S, FFN_CONV_WIDTH - 1, tf), lambda i, j: (layer, i, 0, j)))
        args.append(halo)
    out, nb = pl.pallas_call(
        functools.partial(_ffn_kernel, S=S, L=L, carry=carry),
        grid=(nt, nj),
        in_specs=in_specs,
        out_specs=[pl.BlockSpec((TM, D), lambda i, j: (i, 0)), nb_spec],
        out_shape=[jax.ShapeDtypeStruct((T, D), F32),
                   jax.ShapeDtypeStruct((nt * S, FFN_CONV_WIDTH - 1, d_ff), F32)],
        scratch_shapes=scratch,
        compiler_params=_params("arbitrary", "arbitrary"),
        name="ffn_carry" if carry else "ffn_state",
    )(*args)
    return out, (nb[-1:] if carry else nb)


def _tiling(T, seq_len, carry, tile=TOKEN_TILE):
    TM = min(tile, T)
    L = TM if carry else seq_len
    assert T % TM == 0 and TM % L == 0 and L % SUBLANES == 0
    return TM, L, TM // L


GLU_COLS = 1024


def _glu_kernel(x_ref, g_ref, wa_ref, wb_ref, ba_ref, bb_ref, u_ref, h_scr):
    @pl.when(pl.program_id(1) == 0)
    def _():
        h_scr[...] = _rmsnorm(x_ref[...], g_ref[...]).astype(BF16)

    h = h_scr[...]
    tn = wa_ref.shape[-1]
    ts = 2 * LANES if tn % (2 * LANES) == 0 else tn
    cols = [slice(c0, c0 + ts) for c0 in range(0, tn, ts)]
    ab = [(jnp.dot(h, wa_ref[:, cs], preferred_element_type=F32),
           jnp.dot(h, wb_ref[:, cs], preferred_element_type=F32)) for cs in cols]
    for cs, (a, b) in zip(cols, ab):
        u_ref[:, cs] = (a + ba_ref[:, cs]) * jax.nn.sigmoid(b + bb_ref[:, cs])


def _glu_call(x, g, w1, b1, *, layer):
    T, D = x.shape
    TM = min(TOKEN_TILE, T)
    tn = min(GLU_COLS, D)
    nj = D // tn
    assert T % TM == 0 and D % tn == 0
    return pl.pallas_call(
        _glu_kernel,
        grid=(T // TM, nj),
        in_specs=[
            pl.BlockSpec((TM, D), lambda i, j: (i, 0)),
            pl.BlockSpec((1, D), lambda i, j: (0, 0)),
            pl.BlockSpec((None, D, tn), lambda i, j: (layer, 0, j)),
            pl.BlockSpec((None, D, tn), lambda i, j: (layer, 0, nj + j)),
            pl.BlockSpec((1, tn), lambda i, j: (0, j)),
            pl.BlockSpec((1, tn), lambda i, j: (0, nj + j)),
        ],
        out_specs=pl.BlockSpec((TM, tn), lambda i, j: (i, j)),
        out_shape=jax.ShapeDtypeStruct((T, D), F32),
        scratch_shapes=[pltpu.VMEM((TM, D), BF16)],
        compiler_params=_params("arbitrary", "arbitrary"),
        name="conv_glu",
    )(x, _row(g), w1, w1, _row(b1), _row(b1))


HIST_PAD = 32
DW_ROWS = 32
DW_COLS = 512


def _dwconv_kernel(*refs, S, L, carry):
    if carry:
        u_ref, w_ref, b_ref, c_ref, ux_scr, sh_scr, carry_scr = refs
    else:
        u_ref, w_ref, b_ref, st_ref, c_ref, ux_scr, sh_scr = refs
    i = pl.program_id(0)
    j = pl.program_id(1)
    tc = u_ref.shape[-1]
    K = CONV_WIDTH - 1
    ux_scr[:, HIST_PAD:HIST_PAD + L, :] = u_ref[...].reshape(S, L, tc)
    if carry:
        @pl.when(i == 0)
        def _():
            carry_scr[j] = jnp.zeros((HIST_PAD, tc), F32)
        ux_scr[0, 0:HIST_PAD, :] = carry_scr[j]
    else:
        ux_scr[:, HIST_PAD - K:HIST_PAD, :] = st_ref[...]
    nr = HIST_PAD + L - SUBLANES
    for r in range(1, SUBLANES):
        sh_scr[r - 1] = ux_scr[:, r:r + nr, :]
    w = w_ref[...]
    rb = min(DW_ROWS, L)
    for s in range(S):
        for r0 in range(0, L, rb):
            acc = jnp.broadcast_to(b_ref[...], (rb, tc))
            for k in range(CONV_WIDTH):
                m = HIST_PAD - K + k
                r, lo = m % SUBLANES, r0 + m - m % SUBLANES
                tap = ux_scr[s, lo:lo + rb, :] if r == 0 else sh_scr[r - 1, s, lo:lo + rb, :]
                acc = acc + tap * w[k:k + 1, :]
            c_ref[s * L + r0:s * L + r0 + rb, :] = acc
    if carry:
        carry_scr[j] = ux_scr[0, L:L + HIST_PAD, :]


def _dwconv_call(u, w, b, state, *, seq_len, layer):
    T, D = u.shape
    carry = state is None
    TM, L, S = _tiling(T, seq_len, carry)
    tc = min(DW_COLS, D)
    nc = D // tc
    in_specs = [
        pl.BlockSpec((TM, tc), lambda i, j: (i, j)),
        pl.BlockSpec((CONV_WIDTH, tc), lambda i, j: (0, j)),
        pl.BlockSpec((1, tc), lambda i, j: (0, j)),
    ]
    args = [u, w, _row(b)]
    scratch = [pltpu.VMEM((S, HIST_PAD + L, tc), F32),
               pltpu.VMEM((SUBLANES - 1, S, HIST_PAD + L - SUBLANES, tc), F32)]
    if carry:
        scratch.append(pltpu.VMEM((nc, HIST_PAD, tc), F32))
    else:
        in_specs.append(pl.BlockSpec((None, S, CONV_WIDTH - 1, tc), lambda i, j: (layer, i, 0, j)))
        args.append(state)
    return pl.pallas_call(
        functools.partial(_dwconv_kernel, S=S, L=L, carry=carry),
        grid=(T // TM, nc),
        in_specs=in_specs,
        out_specs=pl.BlockSpec((TM, tc), lambda i, j: (i, j)),
        out_shape=jax.ShapeDtypeStruct((T, D), F32),
        scratch_shapes=scratch,
        compiler_params=_params("arbitrary", "arbitrary"),
        name="conv_dw_carry" if carry else "conv_dw_state",
    )(*args)


def _proj_kernel(*refs, ln, bias):
    refs = list(refs)
    z_ref, x_ref, w_ref, g_ref = refs[:4]
    rest = refs[4:]
    if ln:
        lg_ref, lb_ref = rest[:2]
        rest = rest[2:]
    if bias:
        b_ref = rest[0]
        rest = rest[1:]
    o_ref, = rest
    z = z_ref[...].astype(F32)
    if ln:
        mu = jnp.mean(z, axis=-1, keepdims=True)
        zc = z - mu
        var = jnp.mean(zc * zc, axis=-1, keepdims=True)
        z = zc * lax.rsqrt(var + LN_EPS) * lg_ref[...] + lb_ref[...]
        z = z * jax.nn.sigmoid(z)
    out = jnp.dot(z.astype(BF16), w_ref[...], preferred_element_type=F32)
    if bias:
        out = out + b_ref[...]
    o_ref[...] = x_ref[...] + _rmsnorm(out, g_ref[...])


PROJ_TILE = 512


def _head_ones(width=LANES):
    r = lax.broadcasted_iota(jnp.int32, (width, width), 0) // HEAD_SIZE
    c = lax.broadcasted_iota(jnp.int32, (width, width), 1) // HEAD_SIZE
    return (r == c).astype(BF16)


def _head_sum(x, ones):
    outs = []
    for p in range(x.shape[1] // LANES):
        xp = x[:, p * LANES:(p + 1) * LANES]
        hi = xp.astype(BF16)
        lo = (xp - hi.astype(F32)).astype(BF16)
        outs.append(jnp.dot(hi, ones, preferred_element_type=F32) + jnp.dot(lo, ones, preferred_element_type=F32))
    return outs[0] if len(outs) == 1 else jnp.concatenate(outs, axis=1)


LORA_PAD = 128
RWKV_IN_TILE = 512
RWKV_IN_COLS = 512


def _softplus(z):
    return jnp.maximum(z, 0.0) + jnp.log(1.0 + jnp.exp(-jnp.abs(z)))


MIX_CHUNK = 512


def _rwkv_in_kernel(*refs, S, L, carry, vres):
    refs = list(refs)
    x_ref, g_ref, mix_ref, wr_ref, wk_ref, wv_ref, w1_ref, a1_ref, g1_ref = refs[:9]
    w2_ref, a2_ref, g2_ref, w0_ref, a0_ref, kk_ref, ka_ref = refs[9:16]
    rest = refs[16:]
    if vres:
        v1_ref, v2_ref, v0_ref, vf_ref = rest[:4]
        rest = rest[4:]
    if not carry:
        sh_ref = rest[0]
        rest = rest[1:]
    r_ref, k_ref, v_ref, kkn_ref, a_ref, lw_ref, go_ref, hl_ref = rest[:8]
    rest = rest[8:]
    mx_scr, tw_scr, ta_scr, tg_scr = rest[:4]
    rest = rest[4:]
    if vres:
        tv_scr = rest[0]
        rest = rest[1:]
    if carry:
        carry_scr, = rest
    i = pl.program_id(0)
    j = pl.program_id(1)
    TM, D = x_ref.shape

    @pl.when(j == 0)
    def _():
        x = x_ref[...]
        rs = lax.rsqrt(jnp.mean(x * x, axis=-1, keepdims=True) + RMS_EPS)
        if carry:
            @pl.when(i == 0)
            def _():
                carry_scr[...] = jnp.zeros((1, D), F32)
        mc = min(MIX_CHUNK, D)
        seg_start = lax.broadcasted_iota(jnp.int32, (TM, mc), 0) % L == 0
        lora = {}

        def acc(name, xm, w_ref, cs):
            part = jnp.dot(xm, w_ref[cs, :], preferred_element_type=F32)
            lora[name] = part if name not in lora else lora[name] + part

        for c0 in range(0, D, mc):
            cs = slice(c0, c0 + mc)
            h = x_ref[:, cs] * rs * g_ref[:, cs]
            if carry:
                first = jnp.broadcast_to(carry_scr[:, cs], (TM, mc))
                carry_scr[:, cs] = h[TM - 1:TM, :]
            else:
                first = jnp.broadcast_to(sh_ref[:, :, cs], (S, L, mc)).reshape(TM, mc)
            for s in range(S):
                hl_ref[s, :, cs] = h[s * L + L - 1:s * L + L, :]
            xx = jnp.where(seg_start, first, pltpu.roll(h, 1, axis=0)) - h
            mixed = [(h + xx * mix_ref[m:m + 1, cs]).astype(BF16) for m in range(6)]
            mx_scr[0, :, cs] = mixed[0]
            mx_scr[1, :, cs] = mixed[2]
            mx_scr[2, :, cs] = mixed[3]
            acc('w', mixed[1], w1_ref, cs)
            acc('a', mixed[4], a1_ref, cs)
            acc('g', mixed[5], g1_ref, cs)
            if vres:
                acc('v', mixed[3], v1_ref, cs)
        tw_scr[...] = jnp.tanh(lora['w']).astype(BF16)
        ta_scr[...] = lora['a'].astype(BF16)
        tg_scr[...] = jax.nn.sigmoid(lora['g']).astype(BF16)
        if vres:
            tv_scr[...] = lora['v'].astype(BF16)

    tn = wr_ref.shape[-1]
    ts = 2 * LANES if tn % (2 * LANES) == 0 else tn
    ones = _head_ones()
    for c0 in range(0, tn, ts):
        cs = slice(c0, c0 + ts)
        r = jnp.dot(mx_scr[0], wr_ref[:, cs], preferred_element_type=F32)
        k = jnp.dot(mx_scr[1], wk_ref[:, cs], preferred_element_type=F32)
        v = jnp.dot(mx_scr[2], wv_ref[:, cs], preferred_element_type=F32)
        wl = jnp.dot(tw_scr[...], w2_ref[:, cs], preferred_element_type=F32)
        al = jnp.dot(ta_scr[...], a2_ref[:, cs], preferred_element_type=F32)
        go_ref[:, cs] = jnp.dot(tg_scr[...], g2_ref[:, cs], preferred_element_type=F32).astype(go_ref.dtype)
        w = -_softplus(-(w0_ref[:, cs] + wl)) - 0.5
        lw_ref[:, cs] = -jnp.exp(w)
        a = jax.nn.sigmoid(a0_ref[:, cs] + al)
        if vres:
            vl = jnp.dot(tv_scr[...], v2_ref[:, cs], preferred_element_type=F32)
            v = v + (vf_ref[:, cs].astype(F32) - v) * jax.nn.sigmoid(v0_ref[:, cs] + vl)
        kk = k * kk_ref[:, cs]
        norm = jnp.sqrt(_head_sum(kk * kk, ones))
        kkn_ref[:, cs] = (kk / jnp.maximum(norm, 1e-12)).astype(kkn_ref.dtype)
        r_ref[:, cs] = r.astype(r_ref.dtype)
        k_ref[:, cs] = (k * (1.0 + (a - 1.0) * ka_ref[:, cs])).astype(k_ref.dtype)
        v_ref[:, cs] = v.astype(v_ref.dtype)
        a_ref[:, cs] = a.astype(a_ref.dtype)


WKV_GROUP_HEADS = LANES // HEAD_SIZE
WKV_GROUPS = 16
INV_BLOCK = 16


def _mm(a, b):
    return jnp.dot(a.astype(BF16), b.astype(BF16), preferred_element_type=F32)


def _mm_nt(a, b):
    return lax.dot_general(a.astype(BF16), b.astype(BF16), (((1,), (1,)), ((), ())), preferred_element_type=F32)


def _mm_tn(a, b):
    return lax.dot_general(a.astype(BF16), b.astype(BF16), (((0,), (0,)), ((), ())), preferred_element_type=F32)


def _wkv_kernel(r_ref, k_ref, v_ref, kk_ref, a_ref, lw_ref, g_ref, rk_ref, lng_ref, lnb_ref, s_ref,
                z_ref, so_ref, st_scr, *, NG, GH):
    c = pl.program_id(2)
    nc = pl.num_programs(2)
    C = r_ref.shape[0]
    GW = GH * C
    assert C == HEAD_SIZE

    @pl.when(c == 0)
    def _():
        for gi in range(NG):
            st_scr[gi] = jnp.concatenate([s_ref[0, GH * gi + h] for h in range(GH)], axis=1)

    def ld(ref, sl):
        return ref[:, sl].astype(F32)

    row = lax.broadcasted_iota(jnp.int32, (C, GW), 0)
    lane = lax.broadcasted_iota(jnp.int32, (C, GW), 1)
    sidx = lane % C
    head = lane // C
    strict = sidx < row
    incl = sidx <= row
    eye = (sidx == row).astype(F32)
    ones = _head_ones(GW)

    def bd(x):
        z = jnp.zeros_like(x)
        return jnp.concatenate([jnp.where(head == h, x, z) for h in range(GH)], axis=0)

    def diag_blocks(zz):
        out = zz[:C]
        for h in range(1, GH):
            out = jnp.where(head == h, zz[h * C:(h + 1) * C], out)
        return out

    lw = lw_ref[...]
    rows_full = lax.broadcasted_iota(jnp.int32, lw.shape, 0)
    lc = lw
    sh = 1
    while sh < C:
        lc = lc + jnp.where(rows_full >= sh, pltpu.roll(lc, sh, axis=0), 0.0)
        sh *= 2

    groups = range(NG)
    sls = [slice(gi * GW, (gi + 1) * GW) for gi in groups]
    lhs, bh_kh, v_bd, ltot, sp = [], [], [], [], []
    gm = []
    for sl in sls:
        lwp, lp = lw[:, sl], lc[:, sl]
        lt = lp[C - 1:C, :]
        kp, kkp = ld(k_ref, sl), ld(kk_ref, sl)
        bp = kkp * ld(a_ref, sl)
        e_neg = jnp.exp(-lp)
        e_rem = jnp.exp(lt - lp)
        at = -(kkp * jnp.exp(lp - lwp))
        rt = ld(r_ref, sl) * jnp.exp(lp)
        lhs.append(jnp.concatenate([at, rt], axis=0).astype(BF16))
        gm.append(_mm_nt(lhs[-1], jnp.concatenate([bd(bp * e_neg), bd(kp * e_neg)], axis=0)))
        bh_kh.append(jnp.concatenate([bp * e_rem, kp * e_rem], axis=0).astype(BF16))
        v_bd.append(bd(ld(v_ref, sl)).astype(BF16))
        ltot.append(lt)
    n_ab = [jnp.where(strict, g[:C, :GW], 0.0) for g in gm]
    a_rb = [jnp.where(incl, g[C:, :GW], 0.0) for g in gm]
    a_k = [jnp.concatenate([jnp.where(strict, g[:C, GW:], 0.0), jnp.where(incl, g[C:, GW:], 0.0)], axis=0)
           for g in gm]
    for pp in groups:
        sp.append(st_scr[pp])
    s_prod = [_mm_nt(lhs[pp], bd(sp[pp])) for pp in groups]
    av = [_mm(a_k[pp], v_bd[pp]) for pp in groups]
    blk = [(row // (INV_BLOCK << lvl)) == (sidx // (INV_BLOCK << lvl)) for lvl in range(3)]
    assert INV_BLOCK << 2 == C
    pw = [jnp.where(blk[0], m, 0.0) for m in n_ab]
    tm = [eye + m for m in pw]
    pw_bd = [bd(m).astype(BF16) for m in pw]
    n = 2
    while n < INV_BLOCK:
        pw = [_mm(pw[pp], pw_bd[pp]) for pp in groups]
        pw_bd = [bd(m).astype(BF16) for m in pw]
        tm = [tm[pp] + _mm(tm[pp], pw_bd[pp]) for pp in groups]
        n *= 2
    for lvl in (1, 2):
        off = jnp.logical_and(blk[lvl], jnp.logical_not(blk[lvl - 1]))
        t_bd = [bd(m).astype(BF16) for m in tm]
        tl = [_mm(tm[pp], bd(jnp.where(off, n_ab[pp], 0.0))) for pp in groups]
        tm = [tm[pp] + _mm(tl[pp], t_bd[pp]) for pp in groups]
    u = [_mm(tm[pp], bd(s_prod[pp][:C] + av[pp][:C])) for pp in groups]
    y = [s_prod[pp][C:] + av[pp][C:] + _mm(a_rb[pp], bd(u[pp])) for pp in groups]
    for pp in groups:
        zz = _mm_tn(jnp.concatenate([u[pp], ld(v_ref, sls[pp])], axis=0), bh_kh[pp])
        st_scr[pp] = sp[pp] * jnp.exp(ltot[pp]) + diag_blocks(zz)
    def head_sums(parts):
        stacked = jnp.concatenate(parts, axis=0) if NG > 1 else parts[0]
        m = _mm(stacked, ones)
        return [m[pp * C:(pp + 1) * C] for pp in groups]

    mu = head_sums([t * (1.0 / HEAD_SIZE) for t in y])
    yc = [y[pp] - mu[pp] for pp in groups]
    var = head_sums([t * t * (1.0 / HEAD_SIZE) for t in yc])
    rk_sum = head_sums([ld(r_ref, sl) * ld(k_ref, sl) * rk_ref[:, sl] for sl in sls])
    for pp, sl in enumerate(sls):
        yn = yc[pp] * lax.rsqrt(var[pp] + GN_EPS) * lng_ref[:, sl] + lnb_ref[:, sl]
        z_ref[:, sl] = ((yn + rk_sum[pp] * ld(v_ref, sl)) * ld(g_ref, sl)).astype(z_ref.dtype)

    @pl.when(c == nc - 1)
    def _():
        for gi in range(NG):
            for h in range(GH):
                so_ref[0, GH * gi + h] = st_scr[gi, :, h * C:(h + 1) * C]


def _wkv_call(rkvkal, gate, r_k, ln_g, ln_b, state, *, seq_len, layer):
    T, D = gate.shape
    B = T // seq_len
    C = WKV_CHUNK
    NC = seq_len // C
    GH = min(WKV_GROUP_HEADS, D // HEAD_SIZE)
    GW = GH * HEAD_SIZE
    NG = min(WKV_GROUPS, D // GW)
    Q = D // (NG * GW)
    assert seq_len % C == 0 and D % (NG * GW) == 0 and GW % LANES == 0
    tile = pl.BlockSpec((C, NG * GW), lambda b, q, c: (b * NC + c, q))
    row = pl.BlockSpec((1, NG * GW), lambda b, q, c: (0, q))
    st = pl.BlockSpec((1, NG * GH, HEAD_SIZE, HEAD_SIZE), lambda b, q, c: (b, q, 0, 0))
    st_in = pl.BlockSpec((None, 1, NG * GH, HEAD_SIZE, HEAD_SIZE), lambda b, q, c: (layer, b, q, 0, 0))
    return pl.pallas_call(
        functools.partial(_wkv_kernel, NG=NG, GH=GH),
        grid=(B, Q, NC),
        in_specs=[tile] * 7 + [row] * 3 + [st_in],
        out_specs=[tile, st],
        out_shape=[jax.ShapeDtypeStruct((T, D), BF16), jax.ShapeDtypeStruct(state.shape[1:], F32)],
        scratch_shapes=[pltpu.VMEM((NG, HEAD_SIZE, GW), F32)],
        compiler_params=_params("arbitrary", "arbitrary", "arbitrary"),
        name="rwkv_wkv",
    )(*rkvkal, gate, _row(r_k.reshape(-1)), _row(ln_g), _row(ln_b), state)


def _pad_rank(w, axis):
    pad = [(0, 0)] * w.ndim
    pad[axis] = (0, -w.shape[axis] % LORA_PAD)
    return jnp.pad(w, pad)


def _rwkv_in_call(x, g, p, shift, v_first, *, seq_len, layer):
    T, D = x.shape
    carry = shift is None
    vres = v_first is not None
    vl = layer - 1
    TM, L, S = _tiling(T, seq_len, carry, RWKV_IN_TILE)
    tn = min(RWKV_IN_COLS, D)
    nj = D // tn
    row_d = pl.BlockSpec((1, D), lambda i, j: (0, 0))
    row_n = pl.BlockSpec((1, tn), lambda i, j: (0, j))
    tile_n = pl.BlockSpec((TM, tn), lambda i, j: (i, j))

    def col_w(lyr):
        return pl.BlockSpec((None, D, tn), lambda i, j: (lyr, 0, j))

    def lora1(a, lyr):
        return pl.BlockSpec((None,) + a.shape[1:], lambda i, j: (lyr, 0, 0))

    def lora2(a, lyr):
        return pl.BlockSpec((None, a.shape[1], tn), lambda i, j: (lyr, 0, j))

    in_specs = [pl.BlockSpec((TM, D), lambda i, j: (i, 0)), row_d, pl.BlockSpec((6, D), lambda i, j: (0, 0)),
                col_w(layer), col_w(layer), col_w(layer),
                lora1(p['w1'], layer), lora1(p['a1'], layer), lora1(p['g1'], layer),
                lora2(p['w2'], layer), lora2(p['a2'], layer), lora2(p['g2'], layer),
                row_n, row_n, row_n, row_n]
    args = [x, _row(g), p['mix'][layer], p['w_r'], p['w_k'], p['w_v'], p['w1'], p['a1'], p['g1'],
            p['w2'], p['a2'], p['g2'],
            _row(p['w0'][layer]), _row(p['a0'][layer]), _row(p['k_k'][layer]), _row(p['k_a'][layer])]
    if vres:
        in_specs += [lora1(p['v1'], vl), lora2(p['v2'], vl), row_n, tile_n]
        args += [p['v1'], p['v2'], _row(p['v0'][vl]), v_first]
    if not carry:
        in_specs.append(pl.BlockSpec((S, 1, D), lambda i, j: (i, 0, 0)))
        args.append(shift.reshape(-1, 1, D))
    nseg = (T // TM) * S
    out_specs = [tile_n] * 7 + [pl.BlockSpec((S, 1, D), lambda i, j: (i, 0, 0))]
    out_shape = ([jax.ShapeDtypeStruct((T, D), BF16)] * 5 + [jax.ShapeDtypeStruct((T, D), F32)]
                 + [jax.ShapeDtypeStruct((T, D), BF16), jax.ShapeDtypeStruct((nseg, 1, D), F32)])
    scratch = [pltpu.VMEM((3, TM, D), BF16), pltpu.VMEM((TM, p['w1'].shape[2]), BF16),
               pltpu.VMEM((TM, p['a1'].shape[2]), BF16), pltpu.VMEM((TM, p['g1'].shape[2]), BF16)]
    if vres:
        scratch.append(pltpu.VMEM((TM, p['v1'].shape[2]), BF16))
    if carry:
        scratch.append(pltpu.VMEM((1, D), F32))
    outs = pl.pallas_call(
        functools.partial(_rwkv_in_kernel, S=S, L=L, carry=carry, vres=vres),
        grid=(T // TM, nj),
        in_specs=in_specs,
        out_specs=out_specs,
        out_shape=out_shape,
        scratch_shapes=scratch,
        compiler_params=_params("arbitrary", "arbitrary"),
        name="rwkv_in_carry" if carry else "rwkv_in_state",
    )(*args)
    r, k, v, kk, a, lw, gate, hl = outs
    h_last = hl[-1:, 0, :] if carry else hl[:, 0, :]
    return (r, k, v, kk, a, lw), gate, h_last


def _proj_call(z, x, w, g, *, layer, ln_g=None, ln_b=None, b=None):
    T, D = x.shape
    TM = min(PROJ_TILE, T)
    assert T % TM == 0
    row = pl.BlockSpec((1, D), lambda i: (0, 0))
    tile = pl.BlockSpec((TM, D), lambda i: (i, 0))
    in_specs = [tile, tile, pl.BlockSpec((None, D, D), lambda i: (layer, 0, 0)), row]
    args = [z, x, w, _row(g)]
    if ln_g is not None:
        in_specs += [row, row]
        args += [_row(ln_g), _row(ln_b)]
    if b is not None:
        in_specs.append(row)
        args.append(_row(b))
    return pl.pallas_call(
        functools.partial(_proj_kernel, ln=ln_g is not None, bias=b is not None),
        grid=(T // TM,),
        in_specs=in_specs,
        out_specs=tile,
        out_shape=jax.ShapeDtypeStruct((T, D), F32),
        compiler_params=_params("arbitrary"),
        name="proj_ln" if ln_g is not None else "proj",
    )(*args)


def _trunk(x3, conv_st, shift_st, wkv_st, ffn_st, p):
    B, T, D = x3.shape
    x = x3.reshape(B * T, D)
    depth = p['norm_g'].shape[0]
    new_conv, new_shift, new_wkv, new_ffn = [], [], [], []
    v_first = None
    for i in range(depth):
        g = p['norm_g'][i]
        j = i // 2
        if i % 2 == 0:
            u = _glu_call(x, g[0], p['conv_pw1_w'], p['conv_pw1_b'][j], layer=j)
            c = _dwconv_call(u, p['conv_dw_w'][j], p['conv_dw_b'][j], conv_st, seq_len=T, layer=j)
            new_conv.append(u.reshape(B, T, D)[:, T - (CONV_WIDTH - 1):])
            x = _proj_call(c, x, p['conv_pw2_w'], g[1], layer=j, ln_g=p['conv_ln_g'][j], ln_b=p['conv_ln_b'][j],
                           b=p['conv_pw2_b'][j])
        else:
            q = {k[5:]: v for k, v in p.items() if k.startswith('rwkv_')}
            rkvkal, gate, h_last = _rwkv_in_call(x, g[0], q, None if shift_st is None else shift_st[j],
                                                 v_first if j > 0 else None, seq_len=T, layer=j)
            if j == 0:
                v_first = rkvkal[2]
            if wkv_st is None:
                s0, sl = jnp.zeros((1, B, D // HEAD_SIZE, HEAD_SIZE, HEAD_SIZE), F32), 0
            else:
                s0, sl = wkv_st, j
            z, s_new = _wkv_call(rkvkal, gate, q['r_k'][j], q['ln_g'][j], q['ln_b'][j], s0, seq_len=T, layer=sl)
            new_shift.append(h_last)
            new_wkv.append(s_new)
            x = _proj_call(z, x, q['w_o'], g[1], layer=j)
        x, nb = _ffn_call(x, g[2], g[3], p['ffn_w_in'], p['ffn_dw_w'][i], p['ffn_dw_b'][i], p['ffn_w_out'],
                          ffn_st, seq_len=T, layer=i)
        new_ffn.append(nb)
    return (x.reshape(B, T, D), jnp.stack(new_conv), jnp.stack(new_shift), jnp.stack(new_wkv), jnp.stack(new_ffn))


def kernel(x_prompt, x_sample, state_conv_mix, state_rwkv_shift, state_rwkv_wkv, state_ffn_conv, norm_g,
           conv_pw1_w, conv_pw1_b, conv_dw_w, conv_dw_b, conv_ln_g, conv_ln_b, conv_pw2_w, conv_pw2_b,
           rwkv_mix, rwkv_w_r, rwkv_w_k, rwkv_w_v, rwkv_w_o, rwkv_w0, rwkv_w1, rwkv_w2, rwkv_a0, rwkv_a1,
           rwkv_a2, rwkv_v0, rwkv_v1, rwkv_v2, rwkv_g1, rwkv_g2, rwkv_k_k, rwkv_k_a, rwkv_r_k, rwkv_ln_g,
           rwkv_ln_b, ffn_w_in, ffn_dw_w, ffn_dw_b, ffn_w_out):
    bf = lambda w: w.astype(BF16)
    pad_c = lambda w: _pad_rank(w, 2).astype(BF16)
    pad_r = lambda w: _pad_rank(w, 1).astype(BF16)
    p = dict(norm_g=norm_g, conv_pw1_w=bf(conv_pw1_w), conv_pw1_b=conv_pw1_b, conv_dw_w=conv_dw_w,
             conv_dw_b=conv_dw_b, conv_ln_g=conv_ln_g, conv_ln_b=conv_ln_b, conv_pw2_w=bf(conv_pw2_w),
             conv_pw2_b=conv_pw2_b, rwkv_mix=rwkv_mix, rwkv_w_r=bf(rwkv_w_r), rwkv_w_k=bf(rwkv_w_k),
             rwkv_w_v=bf(rwkv_w_v), rwkv_w_o=bf(rwkv_w_o), rwkv_w0=rwkv_w0, rwkv_w1=pad_c(rwkv_w1),
             rwkv_w2=pad_r(rwkv_w2), rwkv_a0=rwkv_a0, rwkv_a1=pad_c(rwkv_a1), rwkv_a2=pad_r(rwkv_a2),
             rwkv_v0=rwkv_v0, rwkv_v1=pad_c(rwkv_v1), rwkv_v2=pad_r(rwkv_v2), rwkv_g1=pad_c(rwkv_g1),
             rwkv_g2=pad_r(rwkv_g2), rwkv_k_k=rwkv_k_k, rwkv_k_a=rwkv_k_a, rwkv_r_k=rwkv_r_k,
             rwkv_ln_g=rwkv_ln_g, rwkv_ln_b=rwkv_ln_b, ffn_w_in=bf(ffn_w_in), ffn_dw_w=ffn_dw_w,
             ffn_dw_b=ffn_dw_b, ffn_w_out=bf(ffn_w_out))
    y_p, conv_p, shift_p, wkv_p, ffn_p = _trunk(x_prompt, None, None, None, None, p)
    y_s, conv_s, shift_s, wkv_s, ffn_s = _trunk(x_sample, state_conv_mix, state_rwkv_shift, state_rwkv_wkv,
                                                state_ffn_conv, p)
    return (y_p, y_s, conv_p, conv_s, shift_p, shift_s, wkv_p, wkv_s, ffn_p, ffn_s)
```

```python
import functools

import jax
import jax.numpy as jnp
from jax import lax
from jax.experimental import pallas as pl
from jax.experimental.pallas import tpu as pltpu

F32 = jnp.float32
BF16 = jnp.bfloat16

RMS_EPS = 1e-6
LN_EPS = 1e-5
GN_EPS = 64e-5
HEAD_SIZE = 64
CONV_WIDTH = 31
FFN_CONV_WIDTH = 3

LANES = 128
SUBLANES = 8
VMEM_LIMIT_BYTES = 56 * 1024 * 1024

TOKEN_TILE = 512
WKV_CHUNK = 64


def _params(*semantics):
    return pltpu.CompilerParams(dimension_semantics=semantics, vmem_limit_bytes=VMEM_LIMIT_BYTES)


def _rmsnorm(x, g):
    return x * lax.rsqrt(jnp.mean(x * x, axis=-1, keepdims=True) + RMS_EPS) * g


def _row(v):
    return v.reshape(1, -1)


GELU_C = 0.7978845608028654
GELU_K = 0.044715


def _gelu_tanh_times(x, u):
    inner = x * (GELU_C + (GELU_C * GELU_K) * (x * x))
    return (x * u) * (0.5 + 0.5 * jnp.tanh(inner))


def _ffn_kernel(*refs, S, L, carry):
    if carry:
        (x_ref, g2_ref, g3_ref, wu_ref, wg_ref, dww_ref, dwb_ref, wo_ref,
         o_ref, nb_ref, h_scr, gpx_scr, carry_scr) = refs
    else:
        (x_ref, g2_ref, g3_ref, wu_ref, wg_ref, dww_ref, dwb_ref, wo_ref, halo_ref,
         o_ref, nb_ref, h_scr, gpx_scr) = refs
    i = pl.program_id(0)
    j = pl.program_id(1)
    nj = pl.num_programs(1)
    K = FFN_CONV_WIDTH - 1
    tf = wu_ref.shape[-1]

    @pl.when(j == 0)
    def _():
        h_scr[...] = _rmsnorm(x_ref[...], g2_ref[...]).astype(BF16)
        o_ref[...] = jnp.zeros(o_ref.shape, F32)

    if carry:
        @pl.when(i == 0)
        def _():
            carry_scr[j] = jnp.zeros((SUBLANES, tf), F32)

    h = h_scr[...]
    w = dww_ref[...]
    nsub = 2 if tf % (2 * LANES) == 0 else 1
    ts = tf // nsub
    cols = [slice(s * ts, (s + 1) * ts) for s in range(nsub)]
    up = [(jnp.dot(h, wu_ref[:, cs], preferred_element_type=F32),
           jnp.dot(h, wg_ref[:, cs], preferred_element_type=F32)) for cs in cols]
    contrib = None
    for cs, (u, gp) in zip(cols, up):
        gp3 = gp.reshape(S, L, ts)
        gpx_scr[:, SUBLANES:SUBLANES + L, cs] = gp3
        if carry:
            gpx_scr[0, 0:SUBLANES, cs] = carry_scr[j, :, cs]
        else:
            gpx_scr[:, SUBLANES - K:SUBLANES, cs] = halo_ref[:, :, cs]
        gc = (gpx_scr[:, SUBLANES - 2:SUBLANES - 2 + L, cs] * w[0:1, cs]
              + gpx_scr[:, SUBLANES - 1:SUBLANES - 1 + L, cs] * w[1:2, cs]
              + gp3 * w[2:3, cs] + dwb_ref[:, cs])
        act = _gelu_tanh_times(gc.astype(BF16), u.reshape(S, L, ts).astype(BF16)).reshape(S * L, ts)
        part = jnp.dot(act, wo_ref[cs, :], preferred_element_type=F32)
        contrib = part if contrib is None else contrib + part
    o_ref[...] += contrib

    nb_ref[...] = gpx_scr[:, SUBLANES + L - K:SUBLANES + L, :]
    if carry:
        carry_scr[j] = gpx_scr[0, L:L + SUBLANES, :]

    @pl.when(j == nj - 1)
    def _():
        o_ref[...] = x_ref[...] + _rmsnorm(o_ref[...], g3_ref[...])


def _ffn_chunk(d_ff):
    for tf in (512, 256, 128):
        if d_ff % tf == 0:
            return tf
    raise ValueError(f"d_ff={d_ff} is not a multiple of {LANES}")


def _ffn_call(x, g2, g3, w_in, dw_w, dw_b, w_out, halo, *, seq_len, layer):
    T, D = x.shape
    d_ff = w_out.shape[1]
    tf = _ffn_chunk(d_ff)
    nj = d_ff // tf
    carry = halo is None
    TM = min(TOKEN_TILE, T)
    L = TM if carry else seq_len
    S = TM // L
    assert T % TM == 0 and TM % L == 0 and (carry or L == seq_len)
    nt = T // TM
    in_specs = [
        pl.BlockSpec((TM, D), lambda i, j: (i, 0)),
        pl.BlockSpec((1, D), lambda i, j: (0, 0)),
        pl.BlockSpec((1, D), lambda i, j: (0, 0)),
        pl.BlockSpec((None, D, tf), lambda i, j: (layer, 0, j)),
        pl.BlockSpec((None, D, tf), lambda i, j: (layer, 0, nj + j)),
        pl.BlockSpec((FFN_CONV_WIDTH, tf), lambda i, j: (0, j)),
        pl.BlockSpec((1, tf), lambda i, j: (0, j)),
        pl.BlockSpec((None, tf, D), lambda i, j: (layer, j, 0)),
    ]
    args = [x, _row(g2), _row(g3), w_in, w_in, dw_w, _row(dw_b), w_out]
    scratch = [pltpu.VMEM((TM, D), BF16), pltpu.VMEM((S, SUBLANES + L, tf), F32)]
    nb_spec = pl.BlockSpec((S, FFN_CONV_WIDTH - 1, tf), lambda i, j: (i, 0, j))
    if carry:
        scratch.append(pltpu.VMEM((nj, SUBLANES, tf), F32))
    else:
        in_specs.append(pl.BlockSpec((None, S, FFN_CONV_WIDTH - 1, tf), lambda i, j: (layer, i, 0, j)))
        args.append(halo)
    out, nb = pl.pallas_call(
        functools.partial(_ffn_kernel, S=S, L=L, carry=carry),
        grid=(nt, nj),
        in_specs=in_specs,
        out_specs=[pl.BlockSpec((TM, D), lambda i, j: (i, 0)), nb_spec],
        out_shape=[jax.ShapeDtypeStruct((T, D), F32),
                   jax.ShapeDtypeStruct((nt * S, FFN_CONV_WIDTH - 1, d_ff), F32)],
        scratch_shapes=scratch,
        compiler_params=_params("arbitrary", "arbitrary"),
        name="ffn_carry" if carry else "ffn_state",
    )(*args)
    return out, (nb[-1:] if carry else nb)


def _tiling(T, seq_len, carry, tile=TOKEN_TILE):
    TM = min(tile, T)
    L = TM if carry else seq_len
    assert T % TM == 0 and TM % L == 0 and L % SUBLANES == 0
    return TM, L, TM // L


GLU_COLS = 1024


def _glu_kernel(x_ref, g_ref, wa_ref, wb_ref, ba_ref, bb_ref, u_ref, h_scr):
    @pl.when(pl.program_id(1) == 0)
    def _():
        h_scr[...] = _rmsnorm(x_ref[...], g_ref[...]).astype(BF16)

    h = h_scr[...]
    tn = wa_ref.shape[-1]
    ts = 2 * LANES if tn % (2 * LANES) == 0 else tn
    cols = [slice(c0, c0 + ts) for c0 in range(0, tn, ts)]
    ab = [(jnp.dot(h, wa_ref[:, cs], preferred_element_type=F32),
           jnp.dot(h, wb_ref[:, cs], preferred_element_type=F32)) for cs in cols]
    for cs, (a, b) in zip(cols, ab):
        u_ref[:, cs] = (a + ba_ref[:, cs]) * jax.nn.sigmoid(b + bb_ref[:, cs])


def _glu_call(x, g, w1, b1, *, layer):
    T, D = x.shape
    TM = min(TOKEN_TILE, T)
    tn = min(GLU_COLS, D)
    nj = D // tn
    assert T % TM == 0 and D % tn == 0
    return pl.pallas_call(
        _glu_kernel,
        grid=(T // TM, nj),
        in_specs=[
            pl.BlockSpec((TM, D), lambda i, j: (i, 0)),
            pl.BlockSpec((1, D), lambda i, j: (0, 0)),
            pl.BlockSpec((None, D, tn), lambda i, j: (layer, 0, j)),
            pl.BlockSpec((None, D, tn), lambda i, j: (layer, 0, nj + j)),
            pl.BlockSpec((1, tn), lambda i, j: (0, j)),
            pl.BlockSpec((1, tn), lambda i, j: (0, nj + j)),
        ],
        out_specs=pl.BlockSpec((TM, tn), lambda i, j: (i, j)),
        out_shape=jax.ShapeDtypeStruct((T, D), F32),
        scratch_shapes=[pltpu.VMEM((TM, D), BF16)],
        compiler_params=_params("arbitrary", "arbitrary"),
        name="conv_glu",
    )(x, _row(g), w1, w1, _row(b1), _row(b1))


HIST_PAD = 32
DW_ROWS = 32
DW_COLS = 512


def _dwconv_kernel(*refs, S, L, carry):
    if carry:
        u_ref, w_ref, b_ref, c_ref, ux_scr, sh_scr, carry_scr = refs
    else:
        u_ref, w_ref, b_ref, st_ref, c_ref, ux_scr, sh_scr = refs
    i = pl.program_id(0)
    j = pl.program_id(1)
    tc = u_ref.shape[-1]
    K = CONV_WIDTH - 1
    ux_scr[:, HIST_PAD:HIST_PAD + L, :] = u_ref[...].reshape(S, L, tc)
    if carry:
        @pl.when(i == 0)
        def _():
            carry_scr[j] = jnp.zeros((HIST_PAD, tc), F32)
        ux_scr[0, 0:HIST_PAD, :] = carry_scr[j]
    else:
        ux_scr[:, HIST_PAD - K:HIST_PAD, :] = st_ref[...]
    nr = HIST_PAD + L - SUBLANES
    for r in range(1, SUBLANES):
        sh_scr[r - 1] = ux_scr[:, r:r + nr, :]
    w = w_ref[...]
    rb = min(DW_ROWS, L)
    for s in range(S):
        for r0 in range(0, L, rb):
            acc = jnp.broadcast_to(b_ref[...], (rb, tc))
            for k in range(CONV_WIDTH):
                m = HIST_PAD - K + k
                r, lo = m % SUBLANES, r0 + m - m % SUBLANES
                tap = ux_scr[s, lo:lo + rb, :] if r == 0 else sh_scr[r - 1, s, lo:lo + rb, :]
                acc = acc + tap * w[k:k + 1, :]
            c_ref[s * L + r0:s * L + r0 + rb, :] = acc
    if carry:
        carry_scr[j] = ux_scr[0, L:L + HIST_PAD, :]


def _dwconv_call(u, w, b, state, *, seq_len, layer):
    T, D = u.shape
    carry = state is None
    TM, L, S = _tiling(T, seq_len, carry)
    tc = min(DW_COLS, D)
    nc = D // tc
    in_specs = [
        pl.BlockSpec((TM, tc), lambda i, j: (i, j)),
        pl.BlockSpec((CONV_WIDTH, tc), lambda i, j: (0, j)),
        pl.BlockSpec((1, tc), lambda i, j: (0, j)),
    ]
    args = [u, w, _row(b)]
    scratch = [pltpu.VMEM((S, HIST_PAD + L, tc), F32),
               pltpu.VMEM((SUBLANES - 1, S, HIST_PAD + L - SUBLANES, tc), F32)]
    if carry:
        scratch.append(pltpu.VMEM((nc, HIST_PAD, tc), F32))
    else:
        in_specs.append(pl.BlockSpec((None, S, CONV_WIDTH - 1, tc), lambda i, j: (layer, i, 0, j)))
        args.append(state)
    return pl.pallas_call(
        functools.partial(_dwconv_kernel, S=S, L=L, carry=carry),
        grid=(T // TM, nc),
        in_specs=in_specs,
        out_specs=pl.BlockSpec((TM, tc), lambda i, j: (i, j)),
        out_shape=jax.ShapeDtypeStruct((T, D), F32),
        scratch_shapes=scratch,
        compiler_params=_params("arbitrary", "arbitrary"),
        name="conv_dw_carry" if carry else "conv_dw_state",
    )(*args)


def _proj_kernel(*refs, ln, bias):
    refs = list(refs)
    z_ref, x_ref, w_ref, g_ref = refs[:4]
    rest = refs[4:]
    if ln:
        lg_ref, lb_ref = rest[:2]
        rest = rest[2:]
    if bias:
        b_ref = rest[0]
        rest = rest[1:]
    o_ref, = rest
    z = z_ref[...].astype(F32)
    if ln:
        mu = jnp.mean(z, axis=-1, keepdims=True)
        zc = z - mu
        var = jnp.mean(zc * zc, axis=-1, keepdims=True)
        z = zc * lax.rsqrt(var + LN_EPS) * lg_ref[...] + lb_ref[...]
        z = z * jax.nn.sigmoid(z)
    out = jnp.dot(z.astype(BF16), w_ref[...], preferred_element_type=F32)
    if bias:
        out = out + b_ref[...]
    o_ref[...] = x_ref[...] + _rmsnorm(out, g_ref[...])


PROJ_TILE = 512


def _head_ones(width=LANES):
    r = lax.broadcasted_iota(jnp.int32, (width, width), 0) // HEAD_SIZE
    c = lax.broadcasted_iota(jnp.int32, (width, width), 1) // HEAD_SIZE
    return (r == c).astype(BF16)


def _head_sum(x, ones):
    outs = []
    for p in range(x.shape[1] // LANES):
        xp = x[:, p * LANES:(p + 1) * LANES]
        hi = xp.astype(BF16)
        lo = (xp - hi.astype(F32)).astype(BF16)
        outs.append(jnp.dot(hi, ones, preferred_element_type=F32) + jnp.dot(lo, ones, preferred_element_type=F32))
    return outs[0] if len(outs) == 1 else jnp.concatenate(outs, axis=1)


LORA_PAD = 128
RWKV_IN_TILE = 512
RWKV_IN_COLS = 512


def _softplus(z):
    return jnp.maximum(z, 0.0) + jnp.log(1.0 + jnp.exp(-jnp.abs(z)))


MIX_CHUNK = 512


def _rwkv_in_kernel(*refs, S, L, carry, vres):
    refs = list(refs)
    x_ref, g_ref, mix_ref, wr_ref, wk_ref, wv_ref, w1_ref, a1_ref, g1_ref = refs[:9]
    w2_ref, a2_ref, g2_ref, w0_ref, a0_ref, kk_ref, ka_ref = refs[9:16]
    rest = refs[16:]
    if vres:
        v1_ref, v2_ref, v0_ref, vf_ref = rest[:4]
        rest = rest[4:]
    if not carry:
        sh_ref = rest[0]
        rest = rest[1:]
    r_ref, k_ref, v_ref, kkn_ref, a_ref, lw_ref, go_ref, hl_ref = rest[:8]
    rest = rest[8:]
    mx_scr, tw_scr, ta_scr, tg_scr = rest[:4]
    rest = rest[4:]
    if vres:
        tv_scr = rest[0]
        rest = rest[1:]
    if carry:
        carry_scr, = rest
    i = pl.program_id(0)
    j = pl.program_id(1)
    TM, D = x_ref.shape

    @pl.when(j == 0)
    def _():
        x = x_ref[...]
        rs = lax.rsqrt(jnp.mean(x * x, axis=-1, keepdims=True) + RMS_EPS)
        if carry:
            @pl.when(i == 0)
            def _():
                carry_scr[...] = jnp.zeros((1, D), F32)
        mc = min(MIX_CHUNK, D)
        seg_start = lax.broadcasted_iota(jnp.int32, (TM, mc), 0) % L == 0
        lora = {}

        def acc(name, xm, w_ref, cs):
            part = jnp.dot(xm, w_ref[cs, :], preferred_element_type=F32)
            lora[name] = part if name not in lora else lora[name] + part

        for c0 in range(0, D, mc):
            cs = slice(c0, c0 + mc)
            h = x_ref[:, cs] * rs * g_ref[:, cs]
            if carry:
                first = jnp.broadcast_to(carry_scr[:, cs], (TM, mc))
                carry_scr[:, cs] = h[TM - 1:TM, :]
            else:
                first = jnp.broadcast_to(sh_ref[:, :, cs], (S, L, mc)).reshape(TM, mc)
            for s in range(S):
                hl_ref[s, :, cs] = h[s * L + L - 1:s * L + L, :]
            xx = jnp.where(seg_start, first, pltpu.roll(h, 1, axis=0)) - h
            mixed = [(h + xx * mix_ref[m:m + 1, cs]).astype(BF16) for m in range(6)]
            mx_scr[0, :, cs] = mixed[0]
            mx_scr[1, :, cs] = mixed[2]
            mx_scr[2, :, cs] = mixed[3]
            acc('w', mixed[1], w1_ref, cs)
            acc('a', mixed[4], a1_ref, cs)
            acc('g', mixed[5], g1_ref, cs)
            if vres:
                acc('v', mixed[3], v1_ref, cs)
        tw_scr[...] = jnp.tanh(lora['w']).astype(BF16)
        ta_scr[...] = lora['a'].astype(BF16)
        tg_scr[...] = jax.nn.sigmoid(lora['g']).astype(BF16)
        if vres:
            tv_scr[...] = lora['v'].astype(BF16)

    tn = wr_ref.shape[-1]
    ts = 2 * LANES if tn % (2 * LANES) == 0 else tn
    ones = _head_ones()
    for c0 in range(0, tn, ts):
        cs = slice(c0, c0 + ts)
        r = jnp.dot(mx_scr[0], wr_ref[:, cs], preferred_element_type=F32)
        k = jnp.dot(mx_scr[1], wk_ref[:, cs], preferred_element_type=F32)
        v = jnp.dot(mx_scr[2], wv_ref[:, cs], preferred_element_type=F32)
        wl = jnp.dot(tw_scr[...], w2_ref[:, cs], preferred_element_type=F32)
        al = jnp.dot(ta_scr[...], a2_ref[:, cs], preferred_element_type=F32)
        go_ref[:, cs] = jnp.dot(tg_scr[...], g2_ref[:, cs], preferred_element_type=F32).astype(go_ref.dtype)
        w = -_softplus(-(w0_ref[:, cs] + wl)) - 0.5
        lw_ref[:, cs] = -jnp.exp(w)
        a = jax.nn.sigmoid(a0_ref[:, cs] + al)
        if vres:
            vl = jnp.dot(tv_scr[...], v2_ref[:, cs], preferred_element_type=F32)
            v = v + (vf_ref[:, cs].astype(F32) - v) * jax.nn.sigmoid(v0_ref[:, cs] + vl)
        kk = k * kk_ref[:, cs]
        norm = jnp.sqrt(_head_sum(kk * kk, ones))
        kkn_ref[:, cs] = (kk / jnp.maximum(norm, 1e-12)).astype(kkn_ref.dtype)
        r_ref[:, cs] = r.astype(r_ref.dtype)
        k_ref[:, cs] = (k * (1.0 + (a - 1.0) * ka_ref[:, cs])).astype(k_ref.dtype)
        v_ref[:, cs] = v.astype(v_ref.dtype)
        a_ref[:, cs] = a.astype(a_ref.dtype)


WKV_GROUP_HEADS = LANES // HEAD_SIZE
WKV_GROUPS = 16
INV_BLOCK = 16


def _mm(a, b):
    return jnp.dot(a.astype(BF16), b.astype(BF16), preferred_element_type=F32)


def _mm_nt(a, b):
    return lax.dot_general(a.astype(BF16), b.astype(BF16), (((1,), (1,)), ((), ())), preferred_element_type=F32)


def _mm_tn(a, b):
    return lax.dot_general(a.astype(BF16), b.astype(BF16), (((0,), (0,)), ((), ())), preferred_element_type=F32)


def _wkv_kernel(r_ref, k_ref, v_ref, kk_ref, a_ref, lw_ref, g_ref, rk_ref, lng_ref, lnb_ref, s_ref,
                z_ref, so_ref, st_scr, *, NG, GH):
    c = pl.program_id(2)
    nc = pl.num_programs(2)
    C = r_ref.shape[0]
    GW = GH * C
    assert C == HEAD_SIZE

    @pl.when(c == 0)
    def _():
        for gi in range(NG):
            st_scr[gi] = jnp.concatenate([s_ref[0, GH * gi + h] for h in range(GH)], axis=1)

    def ld(ref, sl):
        return ref[:, sl].astype(F32)

    row = lax.broadcasted_iota(jnp.int32, (C, GW), 0)
    lane = lax.broadcasted_iota(jnp.int32, (C, GW), 1)
    sidx = lane % C
    head = lane // C
    strict = sidx < row
    incl = sidx <= row
    eye = (sidx == row).astype(F32)
    ones = _head_ones(GW)

    def bd(x):
        z = jnp.zeros_like(x)
        return jnp.concatenate([jnp.where(head == h, x, z) for h in range(GH)], axis=0)

    def diag_blocks(zz):
        out = zz[:C]
        for h in range(1, GH):
            out = jnp.where(head == h, zz[h * C:(h + 1) * C], out)
        return out

    lw = lw_ref[...]
    rows_full = lax.broadcasted_iota(jnp.int32, lw.shape, 0)
    lc = lw
    sh = 1
    while sh < C:
        lc = lc + jnp.where(rows_full >= sh, pltpu.roll(lc, sh, axis=0), 0.0)
        sh *= 2

    groups = range(NG)
    sls = [slice(gi * GW, (gi + 1) * GW) for gi in groups]
    lhs, bh_kh, v_bd, ltot, sp = [], [], [], [], []
    gm = []
    for sl in sls:
        lwp, lp = lw[:, sl], lc[:, sl]
        lt = lp[C - 1:C, :]
        kp, kkp = ld(k_ref, sl), ld(kk_ref, sl)
        bp = kkp * ld(a_ref, sl)
        e_neg = jnp.exp(-lp)
        e_rem = jnp.exp(lt - lp)
        at = -(kkp * jnp.exp(lp - lwp))
        rt = ld(r_ref, sl) * jnp.exp(lp)
        lhs.append(jnp.concatenate([at, rt], axis=0).astype(BF16))
        gm.append(_mm_nt(lhs[-1], jnp.concatenate([bd(bp * e_neg), bd(kp * e_neg)], axis=0)))
        bh_kh.append(jnp.concatenate([bp * e_rem, kp * e_rem], axis=0).astype(BF16))
        v_bd.append(bd(ld(v_ref, sl)).astype(BF16))
        ltot.append(lt)
    n_ab = [jnp.where(strict, g[:C, :GW], 0.0) for g in gm]
    a_rb = [jnp.where(incl, g[C:, :GW], 0.0) for g in gm]
    a_k = [jnp.concatenate([jnp.where(strict, g[:C, GW:], 0.0), jnp.where(incl, g[C:, GW:], 0.0)], axis=0)
           for g in gm]
    for pp in groups:
        sp.append(st_scr[pp])
    s_prod = [_mm_nt(lhs[pp], bd(sp[pp])) for pp in groups]
    av = [_mm(a_k[pp], v_bd[pp]) for pp in groups]
    blk = [(row // (INV_BLOCK << lvl)) == (sidx // (INV_BLOCK << lvl)) for lvl in range(3)]
    assert INV_BLOCK << 2 == C
    pw = [jnp.where(blk[0], m, 0.0) for m in n_ab]
    tm = [eye + m for m in pw]
    pw_bd = [bd(m).astype(BF16) for m in pw]
    n = 2
    while n < INV_BLOCK:
        pw = [_mm(pw[pp], pw_bd[pp]) for pp in groups]
        pw_bd = [bd(m).astype(BF16) for m in pw]
        tm = [tm[pp] + _mm(tm[pp], pw_bd[pp]) for pp in groups]
        n *= 2
    for lvl in (1, 2):
        off = jnp.logical_and(blk[lvl], jnp.logical_not(blk[lvl - 1]))
        t_bd = [bd(m).astype(BF16) for m in tm]
        tl = [_mm(tm[pp], bd(jnp.where(off, n_ab[pp], 0.0))) for pp in groups]
        tm = [tm[pp] + _mm(tl[pp], t_bd[pp]) for pp in groups]
    u = [_mm(tm[pp], bd(s_prod[pp][:C] + av[pp][:C])) for pp in groups]
    y = [s_prod[pp][C:] + av[pp][C:] + _mm(a_rb[pp], bd(u[pp])) for pp in groups]
    for pp in groups:
        zz = _mm_tn(jnp.concatenate([u[pp], ld(v_ref, sls[pp])], axis=0), bh_kh[pp])
        st_scr[pp] = sp[pp] * jnp.exp(ltot[pp]) + diag_blocks(zz)
    def head_sums(parts):
        stacked = jnp.concatenate(parts, axis=0) if NG > 1 else parts[0]
        m = _mm(stacked, ones)
        return [m[pp * C:(pp + 1) * C] for pp in groups]

    mu = head_sums([t * (1.0 / HEAD_SIZE) for t in y])
    yc = [y[pp] - mu[pp] for pp in groups]
    var = head_sums([t * t * (1.0 / HEAD_SIZE) for t in yc])
    rk_sum = head_sums([ld(r_ref, sl) * ld(k_ref, sl) * rk_ref[:, sl] for sl in sls])
    for pp, sl in enumerate(sls):
        yn = yc[pp] * lax.rsqrt(var[pp] + GN_EPS) * lng_ref[:, sl] + lnb_ref[:, sl]
        z_ref[:, sl] = ((yn + rk_sum[pp] * ld(v_ref, sl)) * ld(g_ref, sl)).astype(z_ref.dtype)

    @pl.when(c == nc - 1)
    def _():
        for gi in range(NG):
            for h in range(GH):
                so_ref[0, GH * gi + h] = st_scr[gi, :, h * C:(h + 1) * C]


def _wkv_call(rkvkal, gate, r_k, ln_g, ln_b, state, *, seq_len, layer):
    T, D = gate.shape
    B = T // seq_len
    C = WKV_CHUNK
    NC = seq_len // C
    GH = min(WKV_GROUP_HEADS, D // HEAD_SIZE)
    GW = GH * HEAD_SIZE
    NG = min(WKV_GROUPS, D // GW)
    Q = D // (NG * GW)
    assert seq_len % C == 0 and D % (NG * GW) == 0 and GW % LANES == 0
    tile = pl.BlockSpec((C, NG * GW), lambda b, q, c: (b * NC + c, q))
    row = pl.BlockSpec((1, NG * GW), lambda b, q, c: (0, q))
    st = pl.BlockSpec((1, NG * GH, HEAD_SIZE, HEAD_SIZE), lambda b, q, c: (b, q, 0, 0))
    st_in = pl.BlockSpec((None, 1, NG * GH, HEAD_SIZE, HEAD_SIZE), lambda b, q, c: (layer, b, q, 0, 0))
    return pl.pallas_call(
        functools.partial(_wkv_kernel, NG=NG, GH=GH),
        grid=(B, Q, NC),
        in_specs=[tile] * 7 + [row] * 3 + [st_in],
        out_specs=[tile, st],
        out_shape=[jax.ShapeDtypeStruct((T, D), BF16), jax.ShapeDtypeStruct(state.shape[1:], F32)],
        scratch_shapes=[pltpu.VMEM((NG, HEAD_SIZE, GW), F32)],
        compiler_params=_params("arbitrary", "arbitrary", "arbitrary"),
        name="rwkv_wkv",
    )(*rkvkal, gate, _row(r_k.reshape(-1)), _row(ln_g), _row(ln_b), state)


def _pad_rank(w, axis):
    pad = [(0, 0)] * w.ndim
    pad[axis] = (0, -w.shape[axis] % LORA_PAD)
    return jnp.pad(w, pad)


def _rwkv_in_call(x, g, p, shift, v_first, *, seq_len, layer):
    T, D = x.shape
    carry = shift is None
    vres = v_first is not None
    vl = layer - 1
    TM, L, S = _tiling(T, seq_len, carry, RWKV_IN_TILE)
    tn = min(RWKV_IN_COLS, D)
    nj = D // tn
    row_d = pl.BlockSpec((1, D), lambda i, j: (0, 0))
    row_n = pl.BlockSpec((1, tn), lambda i, j: (0, j))
    tile_n = pl.BlockSpec((TM, tn), lambda i, j: (i, j))

    def col_w(lyr):
        return pl.BlockSpec((None, D, tn), lambda i, j: (lyr, 0, j))

    def lora1(a, lyr):
        return pl.BlockSpec((None,) + a.shape[1:], lambda i, j: (lyr, 0, 0))

    def lora2(a, lyr):
        return pl.BlockSpec((None, a.shape[1], tn), lambda i, j: (lyr, 0, j))

    in_specs = [pl.BlockSpec((TM, D), lambda i, j: (i, 0)), row_d, pl.BlockSpec((6, D), lambda i, j: (0, 0)),
                col_w(layer), col_w(layer), col_w(layer),
                lora1(p['w1'], layer), lora1(p['a1'], layer), lora1(p['g1'], layer),
                lora2(p['w2'], layer), lora2(p['a2'], layer), lora2(p['g2'], layer),
                row_n, row_n, row_n, row_n]
    args = [x, _row(g), p['mix'][layer], p['w_r'], p['w_k'], p['w_v'], p['w1'], p['a1'], p['g1'],
            p['w2'], p['a2'], p['g2'],
            _row(p['w0'][layer]), _row(p['a0'][layer]), _row(p['k_k'][layer]), _row(p['k_a'][layer])]
    if vres:
        in_specs += [lora1(p['v1'], vl), lora2(p['v2'], vl), row_n, tile_n]
        args += [p['v1'], p['v2'], _row(p['v0'][vl]), v_first]
    if not carry:
        in_specs.append(pl.BlockSpec((S, 1, D), lambda i, j: (i, 0, 0)))
        args.append(shift.reshape(-1, 1, D))
    nseg = (T // TM) * S
    out_specs = [tile_n] * 7 + [pl.BlockSpec((S, 1, D), lambda i, j: (i, 0, 0))]
    out_shape = ([jax.ShapeDtypeStruct((T, D), BF16)] * 5 + [jax.ShapeDtypeStruct((T, D), F32)]
                 + [jax.ShapeDtypeStruct((T, D), BF16), jax.ShapeDtypeStruct((nseg, 1, D), F32)])
    scratch = [pltpu.VMEM((3, TM, D), BF16), pltpu.VMEM((TM, p['w1'].shape[2]), BF16),
               pltpu.VMEM((TM, p['a1'].shape[2]), BF16), pltpu.VMEM((TM, p['g1'].shape[2]), BF16)]
    if vres:
        scratch.append(pltpu.VMEM((TM, p['v1'].shape[2]), BF16))
    if carry:
        scratch.append(pltpu.VMEM((1, D), F32))
    outs = pl.pallas_call(
        functools.partial(_rwkv_in_kernel, S=S, L=L, carry=carry, vres=vres),
        grid=(T // TM, nj),
        in_specs=in_specs,
        out_specs=out_specs,
        out_shape=out_shape,
        scratch_shapes=scratch,
        compiler_params=_params("arbitrary", "arbitrary"),
        name="rwkv_in_carry" if carry else "rwkv_in_state",
    )(*args)
    r, k, v, kk, a, lw, gate, hl = outs
    h_last = hl[-1:, 0, :] if carry else hl[:, 0, :]
    return (r, k, v, kk, a, lw), gate, h_last


def _proj_call(z, x, w, g, *, layer, ln_g=None, ln_b=None, b=None):
    T, D = x.shape
    TM = min(PROJ_TILE, T)
    assert T % TM == 0
    row = pl.BlockSpec((1, D), lambda i: (0, 0))
    tile = pl.BlockSpec((TM, D), lambda i: (i, 0))
    in_specs = [tile, tile, pl.BlockSpec((None, D, D), lambda i: (layer, 0, 0)), row]
    args = [z, x, w, _row(g)]
    if ln_g is not None:
        in_specs += [row, row]
        args += [_row(ln_g), _row(ln_b)]
    if b is not None:
        in_specs.append(row)
        args.append(_row(b))
    return pl.pallas_call(
        functools.partial(_proj_kernel, ln=ln_g is not None, bias=b is not None),
        grid=(T // TM,),
        in_specs=in_specs,
        out_specs=tile,
        out_shape=jax.ShapeDtypeStruct((T, D), F32),
        compiler_params=_params("arbitrary"),
        name="proj_ln" if ln_g is not None else "proj",
    )(*args)


def _trunk(x3, conv_st, shift_st, wkv_st, ffn_st, p):
    B, T, D = x3.shape
    x = x3.reshape(B * T, D)
    depth = p['norm_g'].shape[0]
    new_conv, new_shift, new_wkv, new_ffn = [], [], [], []
    v_first = None
    for i in range(depth):
        g = p['norm_g'][i]
        j = i // 2
        if i % 2 == 0:
            u = _glu_call(x, g[0], p['conv_pw1_w'], p['conv_pw1_b'][j], layer=j)
            c = _dwconv_call(u, p['conv_dw_w'][j], p['conv_dw_b'][j], conv_st, seq_len=T, layer=j)
            new_conv.append(u.reshape(B, T, D)[:, T - (CONV_WIDTH - 1):])
            x = _proj_call(c, x, p['conv_pw2_w'], g[1], layer=j, ln_g=p['conv_ln_g'][j], ln_b=p['conv_ln_b'][j],
                           b=p['conv_pw2_b'][j])
        else:
            q = {k[5:]: v for k, v in p.items() if k.startswith('rwkv_')}
            rkvkal, gate, h_last = _rwkv_in_call(x, g[0], q, None if shift_st is None else shift_st[j],
                                                 v_first if j > 0 else None, seq_len=T, layer=j)
            if j == 0:
                v_first = rkvkal[2]
            if wkv_st is None:
                s0, sl = jnp.zeros((1, B, D // HEAD_SIZE, HEAD_SIZE, HEAD_SIZE), F32), 0
            else:
                s0, sl = wkv_st, j
            z, s_new = _wkv_call(rkvkal, gate, q['r_k'][j], q['ln_g'][j], q['ln_b'][j], s0, seq_len=T, layer=sl)
            new_shift.append(h_last)
            new_wkv.append(s_new)
            x = _proj_call(z, x, q['w_o'], g[1], layer=j)
        x, nb = _ffn_call(x, g[2], g[3], p['ffn_w_in'], p['ffn_dw_w'][i], p['ffn_dw_b'][i], p['ffn_w_out'],
                          ffn_st, seq_len=T, layer=i)
        new_ffn.append(nb)
    return (x.reshape(B, T, D), jnp.stack(new_conv), jnp.stack(new_shift), jnp.stack(new_wkv), jnp.stack(new_ffn))


def kernel(x_prompt, x_sample, state_conv_mix, state_rwkv_shift, state_rwkv_wkv, state_ffn_conv, norm_g,
           conv_pw1_w, conv_pw1_b, conv_dw_w, conv_dw_b, conv_ln_g, conv_ln_b, conv_pw2_w, conv_pw2_b,
           rwkv_mix, rwkv_w_r, rwkv_w_k, rwkv_w_v, rwkv_w_o, rwkv_w0, rwkv_w1, rwkv_w2, rwkv_a0, rwkv_a1,
           rwkv_a2, rwkv_v0, rwkv_v1, rwkv_v2, rwkv_g1, rwkv_g2, rwkv_k_k, rwkv_k_a, rwkv_r_k, rwkv_ln_g,
           rwkv_ln_b, ffn_w_in, ffn_dw_w, ffn_dw_b, ffn_w_out):
    bf = lambda w: w.astype(BF16)
    pad_c = lambda w: _pad_rank(w, 2).astype(BF16)
    pad_r = lambda w: _pad_rank(w, 1).astype(BF16)
    p = dict(norm_g=norm_g, conv_pw1_w=bf(conv_pw1_w), conv_pw1_b=conv_pw1_b, conv_dw_w=conv_dw_w,
             conv_dw_b=conv_dw_b, conv_ln_g=conv_ln_g, conv_ln_b=conv_ln_b, conv_pw2_w=bf(conv_pw2_w),
             conv_pw2_b=conv_pw2_b, rwkv_mix=rwkv_mix, rwkv_w_r=bf(rwkv_w_r), rwkv_w_k=bf(rwkv_w_k),
             rwkv_w_v=bf(rwkv_w_v), rwkv_w_o=bf(rwkv_w_o), rwkv_w0=rwkv_w0, rwkv_w1=pad_c(rwkv_w1),
             rwkv_w2=pad_r(rwkv_w2), rwkv_a0=rwkv_a0, rwkv_a1=pad_c(rwkv_a1), rwkv_a2=pad_r(rwkv_a2),
             rwkv_v0=rwkv_v0, rwkv_v1=pad_c(rwkv_v1), rwkv_v2=pad_r(rwkv_v2), rwkv_g1=pad_c(rwkv_g1),
             rwkv_g2=pad_r(rwkv_g2), rwkv_k_k=rwkv_k_k, rwkv_k_a=rwkv_k_a, rwkv_r_k=rwkv_r_k,
             rwkv_ln_g=rwkv_ln_g, rwkv_ln_b=rwkv_ln_b, ffn_w_in=bf(ffn_w_in), ffn_dw_w=ffn_dw_w,
             ffn_dw_b=ffn_dw_b, ffn_w_out=bf(ffn_w_out))
    y_p, conv_p, shift_p, wkv_p, ffn_p = _trunk(x_prompt, None, None, None, None, p)
    y_s, conv_s, shift_s, wkv_s, ffn_s = _trunk(x_sample, state_conv_mix, state_rwkv_shift, state_rwkv_wkv,
                                                state_ffn_conv, p)
    return (y_p, y_s, conv_p, conv_s, shift_p, shift_s, wkv_p, wkv_s, ffn_p, ffn_s)
```
